```python
import math
import jax, jax.numpy as jnp
from jax import lax
import numpy as np

D_MODEL = 2048
BATCH = 16
SEQ = 2048
DEPTH = 1
DEC_BATCH = 32
DEC_SEQ = 64
PAST_LEN = 2048

CHUNK = 64
D_MIX = D_MODEL
FOX_HEADS = 8
FOX_HEAD_DIM = 128
FOX_WIDTH = FOX_HEADS * FOX_HEAD_DIM
LRU_WIDTH = D_MIX - FOX_WIDTH
LRU_BLOCKS = 16
LRU_BLOCK_DIM = LRU_WIDTH // LRU_BLOCKS
CONV_WIDTH = 4
LRU_C = 8.0
N_EXPERTS = 32
TOP_K = 4
D_FF = D_MODEL
SWIGLU_LIMIT = 7.0
SWIGLU_ALPHA = 1.702
Q_BLOCK = 128
MOE_ROW_BLOCK = 256
LN_EPS = 1e-5
DEEPNORM_ALPHA = (2.0 * DEPTH) ** 0.25
DEEPNORM_BETA = (8.0 * DEPTH) ** -0.25

Q0 = 0
K0 = FOX_WIDTH
V0 = 2 * FOX_WIDTH
F0 = 3 * FOX_WIDTH
LX0 = F0 + FOX_HEADS
LY0 = LX0 + LRU_WIDTH
IN_COLS = LY0 + LRU_WIDTH

kernel_name = "fox_rglru_moe_streaming_step"


def layer_norm(x, g, b):
    xf = x.astype(jnp.float32)
    mu = jnp.mean(xf, axis=-1, keepdims=True)
    var = jnp.mean(jnp.square(xf - mu), axis=-1, keepdims=True)
    return ((xf - mu) * lax.rsqrt(var + LN_EPS) * g + b).astype(x.dtype)


def fox_attend(q, k, v, cq, ck, q_pos, k_pos):
    s = jnp.einsum('bqhe,bkhe->bhqk', q, k, preferred_element_type=jnp.float32) * (FOX_HEAD_DIM ** -0.5)
    s = s + jnp.transpose(cq, (0, 2, 1))[..., :, None] - jnp.transpose(ck, (0, 2, 1))[..., None, :]
    mask = k_pos[None, :] <= q_pos[:, None]
    p = jax.nn.softmax(jnp.where(mask, s, -jnp.inf), axis=-1)
    return jnp.einsum('bhqk,bkhe->bqhe', p.astype(v.dtype), v)


def causal_conv(xr, conv_state, w_conv, b_conv):
    T = xr.shape[1]
    xp = jnp.concatenate([conv_state.astype(xr.dtype), xr], axis=1)
    out = b_conv + sum(xp[:, j:j + T] * w_conv[j] for j in range(CONV_WIDTH))
    return out, xp[:, -(CONV_WIDTH - 1):]


def rg_lru(xc, h0, w_a, b_a, w_i, b_i, lam):
    B, T, C = xc.shape
    xb = xc.reshape(B, T, LRU_BLOCKS, LRU_BLOCK_DIM)
    r = jax.nn.sigmoid((jnp.einsum('btnc,ncd->btnd', xb, w_a).reshape(B, T, C) + b_a).astype(jnp.float32))
    i = jax.nn.sigmoid((jnp.einsum('btnc,ncd->btnd', xb, w_i).reshape(B, T, C) + b_i).astype(jnp.float32))
    log_a = -LRU_C * r * jax.nn.softplus(-lam.astype(jnp.float32))
    a = jnp.exp(log_a)
    mult = jnp.sqrt(-jnp.expm1(2.0 * log_a))
    bterm = mult * i * xc.astype(jnp.float32)
    bterm = bterm.at[:, 0].add(a[:, 0] * h0.astype(jnp.float32))

    def combine(left, right):
        al, bl = left
        ar, br = right
        return ar * al, ar * bl + br

    _, h = lax.associative_scan(combine, (a, bterm), axis=1)
    return h, h[:, -1]


def moe(x, w_router, b_router, w_gu, b_gu, w_down, b_down):
    B, T, D = x.shape
    N = B * T
    NK = N * TOP_K
    xt = x.reshape(N, D)
    logits = (xt @ w_router + b_router).astype(jnp.float32)
    top_val, top_idx = lax.top_k(logits, TOP_K)
    gates = jax.nn.softmax(top_val, axis=-1)
    e_flat = top_idx.reshape(-1)
    tok_flat = jnp.arange(NK, dtype=jnp.int32) // TOP_K
    g_flat = gates.reshape(-1)
    order = jnp.argsort(e_flat, stable=True)
    e_sorted = e_flat[order]
    sizes = jnp.bincount(e_flat, length=N_EXPERTS)
    padded = ((sizes + MOE_ROW_BLOCK - 1) // MOE_ROW_BLOCK) * MOE_ROW_BLOCK
    starts = jnp.cumsum(sizes) - sizes
    pstarts = jnp.cumsum(padded) - padded
    pends = pstarts + padded
    dest = pstarts[e_sorted] + (jnp.arange(NK, dtype=jnp.int32) - starts[e_sorted])
    n_blocks = -(-(NK + N_EXPERTS * (MOE_ROW_BLOCK - 1)) // MOE_ROW_BLOCK)
    P = n_blocks * MOE_ROW_BLOCK
    row_tok = jnp.full((P,), N, jnp.int32).at[dest].set(tok_flat[order])
    row_gate = jnp.zeros((P,), jnp.float32).at[dest].set(g_flat[order])
    x_pad = jnp.concatenate([xt, jnp.zeros((1, D), xt.dtype)], axis=0)[row_tok]
    block_e = jnp.minimum(jnp.searchsorted(pends, jnp.arange(n_blocks) * MOE_ROW_BLOCK, side='right'),
                          N_EXPERTS - 1)

    def expert_block(args):
        xb, gb, e = args
        h = xb @ w_gu[e] + b_gu[e]
        gate = jnp.minimum(h[:, :D_FF], SWIGLU_LIMIT)
        up = jnp.clip(h[:, D_FF:], -SWIGLU_LIMIT, SWIGLU_LIMIT)
        act = (up + 1.0) * gate * jax.nn.sigmoid(SWIGLU_ALPHA * gate)
        out = act @ w_down[e] + b_down[e]
        return out * gb[:, None].astype(out.dtype)

    y_pad = lax.map(expert_block, (x_pad.reshape(n_blocks, MOE_ROW_BLOCK, D),
                                   row_gate.reshape(n_blocks, MOE_ROW_BLOCK), block_e))
    y = jax.ops.segment_sum(y_pad.reshape(P, D), row_tok, num_segments=N + 1)[:N]
    return y.reshape(B, T, D)


def encoder_layer(x, conv_state, lru_state, fox_past, p):
    (w_in, b_f, w_conv, b_conv, w_a, b_a, w_i, b_i, lam, w_out, ln1_g, ln1_b,
     w_router, b_router, w_gu, b_gu, w_down, b_down, ln2_g, ln2_b) = p
    B, T, _ = x.shape
    z = jnp.einsum('btd,dc->btc', x, w_in)
    q = z[..., Q0:K0].reshape(B, T, FOX_HEADS, FOX_HEAD_DIM)
    k = z[..., K0:V0].reshape(B, T, FOX_HEADS, FOX_HEAD_DIM)
    v = z[..., V0:F0].reshape(B, T, FOX_HEADS, FOX_HEAD_DIM)
    logf = jax.nn.log_sigmoid((z[..., F0:LX0] + b_f).astype(jnp.float32))
    xr = z[..., LX0:LY0]
    yg = z[..., LY0:IN_COLS]

    if fox_past is None:
        c = jnp.cumsum(logf, axis=1)
        pos = jnp.arange(T)
        outs = []
        for blk in range(T // Q_BLOCK):
            qs, qe = blk * Q_BLOCK, (blk + 1) * Q_BLOCK
            outs.append(fox_attend(q[:, qs:qe], k[:, :qe], v[:, :qe], c[:, qs:qe], c[:, :qe],
                                   pos[qs:qe], pos[:qe]))
        fox_o = jnp.concatenate(outs, axis=1)
    else:
        k_cache, v_cache, logf_cache = fox_past
        past = k_cache.shape[1]
        k_all = jnp.concatenate([k_cache.astype(k.dtype), k], axis=1)
        v_all = jnp.concatenate([v_cache.astype(v.dtype), v], axis=1)
        c_all = jnp.cumsum(jnp.concatenate([logf_cache.astype(jnp.float32), logf], axis=1), axis=1)
        fox_o = fox_attend(q, k_all, v_all, c_all[:, past:], c_all,
                           past + jnp.arange(T), jnp.arange(past + T))

    xc, new_conv = causal_conv(xr, conv_state, w_conv, b_conv)
    h, h_last = rg_lru(xc, lru_state, w_a, b_a, w_i, b_i, lam)
    lru_o = (h * jax.nn.gelu(yg.astype(jnp.float32))).astype(x.dtype)

    mix = jnp.concatenate([fox_o.reshape(B, T, FOX_WIDTH).astype(x.dtype), lru_o], axis=-1) @ w_out
    x1 = layer_norm(DEEPNORM_ALPHA * x + mix, ln1_g, ln1_b)
    y = layer_norm(DEEPNORM_ALPHA * x1 + moe(x1, w_router, b_router, w_gu, b_gu, w_down, b_down),
                   ln2_g, ln2_b)
    return y, k, v, logf, new_conv, h_last


def setup_inputs(seed: int = 0) -> dict:
    key = jax.random.key(seed)
    ks = jax.random.split(key, 32)
    f32 = jnp.float32

    def nrm(k, shape, s=1.0):
        return jax.random.normal(k, shape, f32) * s

    L = DEPTH
    x_prompt = nrm(ks[0], (BATCH, SEQ, D_MODEL))
    x_sample = nrm(ks[1], (DEC_BATCH, DEC_SEQ, D_MODEL))
    cache_k = nrm(ks[2], (L, DEC_BATCH, PAST_LEN, FOX_HEADS, FOX_HEAD_DIM))
    cache_v = nrm(ks[3], (L, DEC_BATCH, PAST_LEN, FOX_HEADS, FOX_HEAD_DIM), DEEPNORM_BETA)
    cache_logf = jax.nn.log_sigmoid(2.5 + nrm(ks[4], (L, DEC_BATCH, PAST_LEN, FOX_HEADS)))
    state_conv = nrm(ks[5], (L, DEC_BATCH, CONV_WIDTH - 1, LRU_WIDTH))
    state_lru = nrm(ks[6], (L, DEC_BATCH, LRU_WIDTH), 0.5)

    w_in = nrm(ks[7], (L, D_MODEL, IN_COLS), D_MODEL ** -0.5)
    w_in = w_in.at[:, :, V0:F0].multiply(DEEPNORM_BETA)
    b_f = jax.random.uniform(ks[8], (L, FOX_HEADS), f32, 1.0, 4.0)
    w_conv = nrm(ks[9], (L, CONV_WIDTH, LRU_WIDTH), CONV_WIDTH ** -0.5)
    b_conv = nrm(ks[10], (L, LRU_WIDTH), 0.01)
    w_a = nrm(ks[11], (L, LRU_BLOCKS, LRU_BLOCK_DIM, LRU_BLOCK_DIM), LRU_BLOCK_DIM ** -0.5)
    b_a = nrm(ks[12], (L, LRU_WIDTH), 0.01)
    w_i = nrm(ks[13], (L, LRU_BLOCKS, LRU_BLOCK_DIM, LRU_BLOCK_DIM), LRU_BLOCK_DIM ** -0.5)
    b_i = nrm(ks[14], (L, LRU_WIDTH), 0.01)
    u = jax.random.uniform(ks[15], (L, LRU_WIDTH), f32, 0.9, 0.999)
    s = u ** (1.0 / LRU_C)
    lam = jnp.log(s) - jnp.log1p(-s)
    w_out = nrm(ks[16], (L, D_MIX, D_MODEL), D_MIX ** -0.5 * DEEPNORM_BETA)
    ln1_g = 1.0 + nrm(ks[17], (L, D_MODEL), 0.01)
    ln1_b = nrm(ks[18], (L, D_MODEL), 0.01)
    w_router = nrm(ks[19], (L, D_MODEL, N_EXPERTS), D_MODEL ** -0.5)
    b_router = nrm(ks[20], (L, N_EXPERTS), 0.01)
    w_gu = nrm(ks[21], (L, N_EXPERTS, D_MODEL, 2 * D_FF), D_MODEL ** -0.5)
    b_gu = nrm(ks[22], (L, N_EXPERTS, 2 * D_FF), 0.01)
    w_down = nrm(ks[23], (L, N_EXPERTS, D_FF, D_MODEL), D_FF ** -0.5 * DEEPNORM_BETA)
    b_down = nrm(ks[24], (L, N_EXPERTS, D_MODEL), 0.01)
    ln2_g = 1.0 + nrm(ks[25], (L, D_MODEL), 0.01)
    ln2_b = nrm(ks[26], (L, D_MODEL), 0.01)
    return {"x_prompt": x_prompt, "x_sample": x_sample,
            "cache_k": cache_k, "cache_v": cache_v, "cache_logf": cache_logf,
            "state_conv": state_conv, "state_lru": state_lru,
            "w_in": w_in, "b_f": b_f, "w_conv": w_conv, "b_conv": b_conv,
            "w_a": w_a, "b_a": b_a, "w_i": w_i, "b_i": b_i, "lam": lam, "w_out": w_out,
            "ln1_g": ln1_g, "ln1_b": ln1_b, "w_router": w_router, "b_router": b_router,
            "w_gu": w_gu, "b_gu": b_gu, "w_down": w_down, "b_down": b_down,
            "ln2_g": ln2_g, "ln2_b": ln2_b}


def reference(x_prompt, x_sample, cache_k, cache_v, cache_logf, state_conv, state_lru,
              w_in, b_f, w_conv, b_conv, w_a, b_a, w_i, b_i, lam, w_out, ln1_g, ln1_b,
              w_router, b_router, w_gu, b_gu, w_down, b_down, ln2_g, ln2_b):
    weights = (w_in, b_f, w_conv, b_conv, w_a, b_a, w_i, b_i, lam, w_out, ln1_g, ln1_b,
               w_router, b_router, w_gu, b_gu, w_down, b_down, ln2_g, ln2_b)
    xp = x_prompt
    xs = x_sample
    kp, vp, lp, cp, hp = [], [], [], [], []
    kss, vss, lss, css, hss = [], [], [], [], []
    for layer in range(DEPTH):
        p = tuple(w[layer] for w in weights)
        conv0 = jnp.zeros((xp.shape[0], CONV_WIDTH - 1, LRU_WIDTH), xp.dtype)
        h0 = jnp.zeros((xp.shape[0], LRU_WIDTH), jnp.float32)
        xp, k_new, v_new, lf_new, conv_new, h_new = encoder_layer(xp, conv0, h0, None, p)
        kp.append(k_new); vp.append(v_new); lp.append(lf_new); cp.append(conv_new); hp.append(h_new)
        xs, k_new, v_new, lf_new, conv_new, h_new = encoder_layer(
            xs, state_conv[layer], state_lru[layer],
            (cache_k[layer], cache_v[layer], cache_logf[layer]), p)
        kss.append(k_new); vss.append(v_new); lss.append(lf_new); css.append(conv_new); hss.append(h_new)
    return (xp, xs,
            jnp.stack(kp), jnp.stack(vp), jnp.stack(lp), jnp.stack(cp), jnp.stack(hp),
            jnp.stack(kss), jnp.stack(vss), jnp.stack(lss), jnp.stack(css), jnp.stack(hss))
```

```python
import functools

import jax
import jax.numpy as jnp
from jax import lax
from jax.experimental import pallas as pl
from jax.experimental.pallas import tpu as pltpu

F32 = jnp.float32
BF16 = jnp.bfloat16

FOX_HEADS = 8
FOX_HEAD_DIM = 128
FOX_WIDTH = FOX_HEADS * FOX_HEAD_DIM
LRU_BLOCKS = 16
CONV_WIDTH = 4
LRU_C = 8.0
N_EXPERTS = 32
TOP_K = 4
SWIGLU_LIMIT = 7.0
SWIGLU_ALPHA = 1.702
LN_EPS = 1e-5
DEPTH = 1
DEEPNORM_ALPHA = (2.0 * DEPTH) ** 0.25

VMEM_LIMIT = 56 * 1024 * 1024


def _cparams(sem):
    return pltpu.CompilerParams(dimension_semantics=sem, vmem_limit_bytes=VMEM_LIMIT)


def _log_sigmoid(x):
    return jnp.minimum(x, 0.0) - jnp.log1p(jnp.exp(-jnp.abs(x)))


def _layer_norm(z, g, b):
    mu = jnp.mean(z, axis=-1, keepdims=True)
    zc = z - mu
    var = jnp.mean(zc * zc, axis=-1, keepdims=True)
    return zc * lax.rsqrt(var + LN_EPS) * g + b


def _in_proj_kernel(x_ref, w_ref, wf_ref, bf_ref, q_ref, k_ref, v_ref, xr_ref, yg_ref, lf_ref, xb_ref):
    j = pl.program_id(1)

    @pl.when(j == 0)
    def _():
        xb_ref[...] = x_ref[...].astype(BF16)
        zf = jnp.dot(xb_ref[...], wf_ref[...], preferred_element_type=F32)
        lf_ref[...] = _log_sigmoid(zf[:, :FOX_HEADS] + bf_ref[...])

    z = jnp.dot(xb_ref[...], w_ref[...], preferred_element_type=F32)
    for jj, ref in enumerate((q_ref, k_ref, v_ref, xr_ref, yg_ref)):
        @pl.when(j == jj)
        def _(ref=ref):
            ref[...] = z.astype(ref.dtype)


def _in_proj(x2d, w_main, w_f, b_f, tm):
    n, d = x2d.shape
    wcol = FOX_WIDTH
    row = lambda i, j: (i, 0)
    outs = [jax.ShapeDtypeStruct((n, wcol), BF16)] + [jax.ShapeDtypeStruct((n, wcol), F32)] * 4
    outs.append(jax.ShapeDtypeStruct((n, FOX_HEADS), F32))
    return pl.pallas_call(
        _in_proj_kernel,
        out_shape=outs,
        grid=(n // tm, 5),
        in_specs=[
            pl.BlockSpec((tm, d), row),
            pl.BlockSpec((d, wcol), lambda i, j: (0, j)),
            pl.BlockSpec((d, 128), lambda i, j: (0, 0)),
            pl.BlockSpec((1, FOX_HEADS), lambda i, j: (0, 0)),
        ],
        out_specs=[pl.BlockSpec((tm, wcol), row)] * 5 + [pl.BlockSpec((tm, FOX_HEADS), row)],
        scratch_shapes=[pltpu.VMEM((tm, d), BF16)],
        compiler_params=_cparams(("parallel", "arbitrary")),
        name="in_proj",
    )(x2d, w_main, w_f, b_f)


def _attn_heads_step(q_ref, k_ref, v_ref, cq_ref, ck_ref, m_ref, l_ref, acc_ref, masked):
    tq = q_ref.shape[0]
    tk = k_ref.shape[0]
    scale = FOX_HEAD_DIM ** -0.5

    def head(h, carry):
        sl = pl.ds(pl.multiple_of(h * FOX_HEAD_DIM, FOX_HEAD_DIM), FOX_HEAD_DIM)
        q = q_ref[:, sl]
        k = k_ref[:, sl].astype(BF16)
        v = v_ref[:, sl].astype(BF16)
        s = lax.dot_general(q, k, (((1,), (1,)), ((), ())), preferred_element_type=F32) * scale
        s = s + cq_ref[h] - ck_ref[h]
        if masked:
            rows = lax.broadcasted_iota(jnp.int32, (tq, tk), 0)
            cols = lax.broadcasted_iota(jnp.int32, (tq, tk), 1)
            s = jnp.where(cols <= rows, s, -jnp.inf)
        m_prev = m_ref[h]
        m_new = jnp.maximum(m_prev, jnp.max(s, axis=-1, keepdims=True))
        alpha = jnp.exp(m_prev - m_new)
        p = jnp.exp(s - m_new)
        l_ref[h] = alpha * l_ref[h] + jnp.sum(p, axis=-1, keepdims=True)
        acc_ref[:, sl] = alpha * acc_ref[:, sl] + jnp.dot(p.astype(BF16), v, preferred_element_type=F32)
        m_ref[h] = m_new
        return carry

    lax.fori_loop(0, FOX_HEADS, head, 0)


def _attn_init(m_ref, l_ref, acc_ref):
    m_ref[...] = jnp.full(m_ref.shape, -jnp.inf, F32)
    l_ref[...] = jnp.zeros(l_ref.shape, F32)
    acc_ref[...] = jnp.zeros(acc_ref.shape, F32)


def _attn_finish(o_ref, l_ref, acc_ref):
    for h in range(FOX_HEADS):
        sl = slice(h * FOX_HEAD_DIM, (h + 1) * FOX_HEAD_DIM)
        o_ref[:, sl] = (acc_ref[:, sl] / l_ref[h]).astype(o_ref.dtype)


def _fox_prompt_kernel(q_ref, k_ref, v_ref, cq_ref, ck_ref, o_ref, m_ref, l_ref, acc_ref):
    qi = pl.program_id(1)
    ki = pl.program_id(2)

    @pl.when(ki == 0)
    def _():
        _attn_init(m_ref, l_ref, acc_ref)

    @pl.when(ki < qi)
    def _():
        _attn_heads_step(q_ref, k_ref, v_ref, cq_ref.at[0], ck_ref.at[0], m_ref, l_ref, acc_ref, False)

    @pl.when(ki == qi)
    def _():
        _attn_heads_step(q_ref, k_ref, v_ref, cq_ref.at[0], ck_ref.at[0], m_ref, l_ref, acc_ref, True)
        _attn_finish(o_ref, l_ref, acc_ref)


def _fox_prompt(q, k, v, c, batch, seq, tq):
    nq = seq // tq
    width = q.shape[1]
    cq = jnp.transpose(c, (0, 2, 1))[..., None]
    ck = jnp.transpose(c, (0, 2, 1))[:, :, None, :]
    kv_map = lambda b, qi, ki: (b * nq + jnp.minimum(ki, qi), 0)
    return pl.pallas_call(
        _fox_prompt_kernel,
        out_shape=jax.ShapeDtypeStruct(q.shape, BF16),
        grid=(batch, nq, nq),
        in_specs=[
            pl.BlockSpec((tq, width), lambda b, qi, ki: (b * nq + qi, 0)),
            pl.BlockSpec((tq, width), kv_map),
            pl.BlockSpec((tq, width), kv_map),
            pl.BlockSpec((1, FOX_HEADS, tq, 1), lambda b, qi, ki: (b, 0, qi, 0)),
            pl.BlockSpec((1, FOX_HEADS, 1, tq), lambda b, qi, ki: (b, 0, 0, jnp.minimum(ki, qi))),
        ],
        out_specs=pl.BlockSpec((tq, width), lambda b, qi, ki: (b * nq + qi, 0)),
        scratch_shapes=[
            pltpu.VMEM((FOX_HEADS, tq, 1), F32),
            pltpu.VMEM((FOX_HEADS, tq, 1), F32),
            pltpu.VMEM((tq, width), F32),
        ],
        compiler_params=_cparams(("parallel", "parallel", "arbitrary")),
        name="fox_prompt",
    )(q, k, v, cq, ck)


def _fox_sample_kernel(q_ref, kc_ref, vc_ref, kn_ref, vn_ref, cq_ref, ckc_ref, ckn_ref, o_ref,
                       m_ref, l_ref, acc_ref):
    j = pl.program_id(1)

    @pl.when(j == 0)
    def _():
        _attn_init(m_ref, l_ref, acc_ref)

    _attn_heads_step(q_ref, kc_ref.at[0], vc_ref.at[0], cq_ref.at[0], ckc_ref.at[0], m_ref, l_ref, acc_ref, False)

    @pl.when(j == pl.num_programs(1) - 1)
    def _():
        _attn_heads_step(q_ref, kn_ref, vn_ref, cq_ref.at[0], ckn_ref.at[0], m_ref, l_ref, acc_ref, True)
        _attn_finish(o_ref, l_ref, acc_ref)


def _fox_sample(q, k, v, cache_k, cache_v, c_all, batch, seq, tk):
    past = cache_k.shape[1]
    width = q.shape[1]
    ct = jnp.transpose(c_all, (0, 2, 1))
    cq = ct[:, :, past:, None]
    ckc = ct[:, :, None, :past]
    ckn = ct[:, :, None, past:]
    new_map = lambda b, j: (b, 0)
    return pl.pallas_call(
        _fox_sample_kernel,
        out_shape=jax.ShapeDtypeStruct(q.shape, BF16),
        grid=(batch, past // tk),
        in_specs=[
            pl.BlockSpec((seq, width), new_map),
            pl.BlockSpec((1, tk, width), lambda b, j: (b, j, 0)),
            pl.BlockSpec((1, tk, width), lambda b, j: (b, j, 0)),
            pl.BlockSpec((seq, width), new_map),
            pl.BlockSpec((seq, width), new_map),
            pl.BlockSpec((1, FOX_HEADS, seq, 1), lambda b, j: (b, 0, 0, 0)),
            pl.BlockSpec((1, FOX_HEADS, 1, tk), lambda b, j: (b, 0, 0, j)),
            pl.BlockSpec((1, FOX_HEADS, 1, seq), lambda b, j: (b, 0, 0, 0)),
        ],
        out_specs=pl.BlockSpec((seq, width), new_map),
        scratch_shapes=[
            pltpu.VMEM((FOX_HEADS, seq, 1), F32),
            pltpu.VMEM((FOX_HEADS, seq, 1), F32),
            pltpu.VMEM((seq, width), F32),
        ],
        compiler_params=_cparams(("parallel", "arbitrary")),
        name="fox_sample",
    )(q, cache_k, cache_v, k, v, cq, ckc, ckn)


_HALO = 8


def _lru_kernel(xr_ref, yg_ref, conv0_ref, h0_ref, wc_ref, bc_ref, wg_ref, ba_ref, bi_ref, lam_ref,
                o_ref, convo_ref, hlast_ref, xp_ref, a_ref, b_ref, h_ref):
    t = pl.program_id(1)
    tt, width = xr_ref.shape
    tail = CONV_WIDTH - 1
    lo = _HALO - tail

    @pl.when(t == 0)
    def _():
        xp_ref[lo:_HALO, :] = conv0_ref[0]
        h_ref[...] = h0_ref[0]

    xp_ref[_HALO:_HALO + tt, :] = xr_ref[...]
    xc = bc_ref[...] + xp_ref[lo:lo + tt, :] * wc_ref[0:1, :]
    for j in range(1, CONV_WIDTH):
        xc = xc + xp_ref[lo + j:lo + j + tt, :] * wc_ref[j:j + 1, :]
    new_tail = xp_ref[lo + tt:_HALO + tt, :]
    xp_ref[lo:_HALO, :] = new_tail

    xcb = xc.astype(BF16)
    gw = wg_ref.shape[1]
    lam = lam_ref[...]
    neg_sp = -(jnp.maximum(-lam, 0.0) + jnp.log1p(jnp.exp(-jnp.abs(lam))))
    for g in range(width // gw):
        sl = slice(g * gw, (g + 1) * gw)
        z = jnp.dot(xcb[:, sl], wg_ref[g], preferred_element_type=F32)
        r = jax.nn.sigmoid(z[:, :gw] + ba_ref[:, sl])
        i = jax.nn.sigmoid(z[:, gw:] + bi_ref[:, sl])
        log_a = LRU_C * r * neg_sp[:, sl]
        a = jnp.exp(log_a)
        a_ref[:, sl] = a
        b_ref[:, sl] = jnp.sqrt(-jnp.tanh(log_a) * (a * a + 1.0)) * i * xc[:, sl]

    def step(s, h):
        row = pl.ds(s, 1)
        h = a_ref[row, :] * h + b_ref[row, :]
        b_ref[row, :] = h
        return h

    h = lax.fori_loop(0, tt, step, h_ref[...], unroll=8)
    h_ref[...] = h
    o_ref[...] = (b_ref[...] * jax.nn.gelu(yg_ref[...])).astype(o_ref.dtype)

    @pl.when(t == pl.num_programs(1) - 1)
    def _():
        convo_ref[0] = new_tail
        hlast_ref[0] = h


def _lru(xr, yg, conv0, h0, w_conv, b_conv, w_gate, b_a, b_i, lam, batch, seq, tt):
    width = xr.shape[1]
    nt = seq // tt
    gw = w_gate.shape[1]
    rows = lambda b, t: (b * nt + t, 0)
    const2 = lambda b, t: (0, 0)
    per_b = lambda b, t: (b, 0, 0)
    tail = CONV_WIDTH - 1
    return pl.pallas_call(
        _lru_kernel,
        out_shape=[
            jax.ShapeDtypeStruct((batch * seq, width), BF16),
            jax.ShapeDtypeStruct((batch, tail, width), F32),
            jax.ShapeDtypeStruct((batch, 1, width), F32),
        ],
        grid=(batch, nt),
        in_specs=[
            pl.BlockSpec((tt, width), rows),
            pl.BlockSpec((tt, width), rows),
            pl.BlockSpec((1, tail, width), per_b),
            pl.BlockSpec((1, 1, width), per_b),
            pl.BlockSpec((CONV_WIDTH, width), const2),
            pl.BlockSpec((1, width), const2),
            pl.BlockSpec((width // gw, gw, 2 * gw), lambda b, t: (0, 0, 0)),
            pl.BlockSpec((1, width), const2),
            pl.BlockSpec((1, width), const2),
            pl.BlockSpec((1, width), const2),
        ],
        out_specs=[
            pl.BlockSpec((tt, width), rows),
            pl.BlockSpec((1, tail, width), per_b),
            pl.BlockSpec((1, 1, width), per_b),
        ],
        scratch_shapes=[
            pltpu.VMEM((_HALO + tt, width), F32),
            pltpu.VMEM((tt, width), F32),
            pltpu.VMEM((tt, width), F32),
            pltpu.VMEM((1, width), F32),
        ],
        compiler_params=_cparams(("parallel", "arbitrary")),
        name="lru",
    )(xr, yg, conv0, h0, w_conv, b_conv, w_gate, b_a, b_i, lam)


def _mix_norm_kernel(fox_ref, lru_ref, x_ref, wt_ref, wb_ref, g_ref, b_ref, wrh_ref, wrl_ref, br_ref,
                     x1_ref, x1b_ref, idx_ref, gate_ref):
    mix = jnp.dot(fox_ref[...], wt_ref[...], preferred_element_type=F32)
    mix = mix + jnp.dot(lru_ref[...], wb_ref[...], preferred_element_type=F32)
    x1 = _layer_norm(DEEPNORM_ALPHA * x_ref[...] + mix, g_ref[...], b_ref[...])
    x1_ref[...] = x1
    x1_hi = x1.astype(BF16)
    x1b_ref[...] = x1_hi
    x1_lo = (x1 - x1_hi.astype(F32)).astype(BF16)
    lg = jnp.dot(x1_hi, wrh_ref[...], preferred_element_type=F32)
    lg = lg + jnp.dot(x1_lo, wrh_ref[...], preferred_element_type=F32)
    lg = lg + jnp.dot(x1_hi, wrl_ref[...], preferred_element_type=F32)
    lg = lg + br_ref[...]
    lanes = lax.broadcasted_iota(jnp.int32, lg.shape, 1)
    vals = []
    for k in range(TOP_K):
        m = jnp.max(lg, axis=-1, keepdims=True)
        ix = jnp.min(jnp.where(lg == m, lanes, N_EXPERTS), axis=-1, keepdims=True)
        idx_ref[:, k:k + 1] = ix
        vals.append(m)
        lg = jnp.where(lanes == ix, -jnp.inf, lg)
    es = [jnp.exp(v - vals[0]) for v in vals]
    denom = es[0] + es[1] + es[2] + es[3]
    for k in range(TOP_K):
        gate_ref[:, k:k + 1] = es[k] / denom


def _mix_norm(fox_o, lru_o, x2d, w_top, w_bot, ln_g, ln_b, wr_hi, wr_lo, b_router, tm):
    n, d = x2d.shape
    half = fox_o.shape[1]
    row = lambda i: (i, 0)
    const = lambda i: (0, 0)
    return pl.pallas_call(
        _mix_norm_kernel,
        out_shape=[
            jax.ShapeDtypeStruct((n, d), F32),
            jax.ShapeDtypeStruct((n, d), BF16),
            jax.ShapeDtypeStruct((n, TOP_K), jnp.int32),
            jax.ShapeDtypeStruct((n, TOP_K), F32),
        ],
        grid=(n // tm,),
        in_specs=[
            pl.BlockSpec((tm, half), row),
            pl.BlockSpec((tm, half), row),
            pl.BlockSpec((tm, d), row),
            pl.BlockSpec((half, d), const),
            pl.BlockSpec((half, d), const),
            pl.BlockSpec((1, d), const),
            pl.BlockSpec((1, d), const),
            pl.BlockSpec((d, N_EXPERTS), const),
            pl.BlockSpec((d, N_EXPERTS), const),
            pl.BlockSpec((1, N_EXPERTS), const),
        ],
        out_specs=[
            pl.BlockSpec((tm, d), row),
            pl.BlockSpec((tm, d), row),
            pl.BlockSpec((tm, TOP_K), row),
            pl.BlockSpec((tm, TOP_K), row),
        ],
        compiler_params=_cparams(("parallel",)),
        name="mix_norm",
    )(fox_o, lru_o, x2d, w_top, w_bot, ln_g, ln_b, wr_hi, wr_lo, b_router)


def _expert_kernel(be_ref, bv_ref, nu_ref, x_ref, wg_ref, wu_ref, bg_ref, bu_ref, wd_ref, bd_ref,
                   o_ref, acc_ref, *, sub):
    rb = pl.program_id(0)
    f = pl.program_id(1)
    last_f = pl.num_programs(1) - 1
    valid = bv_ref[rb]
    tm = x_ref.shape[0]

    @pl.when(valid > 0)
    def _():
        wg = wg_ref[0].astype(BF16)
        wu = wu_ref[0].astype(BF16)
        wd = wd_ref[0].astype(BF16)
        for s in range(tm // sub):
            rows = slice(s * sub, (s + 1) * sub)

            @pl.when(s * sub < valid)
            def _(rows=rows):
                x = x_ref[rows, :]
                hg = jnp.dot(x, wg, preferred_element_type=F32) + bg_ref[0]
                hu = jnp.dot(x, wu, preferred_element_type=F32) + bu_ref[0]
                gate = jnp.minimum(hg, SWIGLU_LIMIT)
                up = jnp.clip(hu, -SWIGLU_LIMIT, SWIGLU_LIMIT)
                act = (up + 1.0) * gate * jax.nn.sigmoid(SWIGLU_ALPHA * gate)
                part = jnp.dot(act.astype(BF16), wd, preferred_element_type=F32)

                @pl.when(f == 0)
                def _():
                    acc_ref[rows, :] = part + bd_ref[0]

                @pl.when(f > 0)
                def _():
                    acc_ref[rows, :] += part

                @pl.when(f == last_f)
                def _():
                    o_ref[rows, :] = acc_ref[rows, :].astype(o_ref.dtype)

            @pl.when(jnp.logical_and(s * sub >= valid, f == last_f))
            def _(rows=rows):
                o_ref[rows, :] = jnp.zeros((sub, o_ref.shape[1]), o_ref.dtype)


def _experts(x_pad, block_e, block_valid, n_used, w_gu, b_gu, w_down, b_down, tm, tf, sub):
    p, d = x_pad.shape
    d_ff = w_down.shape[1]
    nf = d_ff // tf
    nb = p // tm

    def blk(rb, nu):
        return jnp.minimum(rb, nu[0] - 1)

    grid_spec = pltpu.PrefetchScalarGridSpec(
        num_scalar_prefetch=3,
        grid=(nb, nf),
        in_specs=[
            pl.BlockSpec((tm, d), lambda rb, f, be, bv, nu: (blk(rb, nu), 0)),
            pl.BlockSpec((1, d, tf), lambda rb, f, be, bv, nu: (be[rb], 0, f)),
            pl.BlockSpec((1, d, tf), lambda rb, f, be, bv, nu: (be[rb], 0, nf + f)),
            pl.BlockSpec((1, 1, tf), lambda rb, f, be, bv, nu: (be[rb], 0, f)),
            pl.BlockSpec((1, 1, tf), lambda rb, f, be, bv, nu: (be[rb], 0, nf + f)),
            pl.BlockSpec((1, tf, d), lambda rb, f, be, bv, nu: (be[rb], f, 0)),
            pl.BlockSpec((1, 1, d), lambda rb, f, be, bv, nu: (be[rb], 0, 0)),
        ],
        out_specs=pl.BlockSpec((tm, d), lambda rb, f, be, bv, nu: (blk(rb, nu), 0)),
        scratch_shapes=[pltpu.VMEM((tm, d), F32)],
    )
    return pl.pallas_call(
        functools.partial(_expert_kernel, sub=sub),
        out_shape=jax.ShapeDtypeStruct((p, d), BF16),
        grid_spec=grid_spec,
        compiler_params=_cparams(("arbitrary", "arbitrary")),
        name="experts",
    )(block_e, block_valid, n_used, x_pad, w_gu, w_gu, b_gu, b_gu, w_down, b_down)


def _combine_kernel(y_ref, gate_ref, x1_ref, g_ref, b_ref, o_ref):
    y = y_ref[0].astype(F32) * gate_ref[:, 0:1]
    for k in range(1, TOP_K):
        y = y + y_ref[k].astype(F32) * gate_ref[:, k:k + 1]
    o_ref[...] = _layer_norm(DEEPNORM_ALPHA * x1_ref[...] + y, g_ref[...], b_ref[...])


def _combine(y_rows, gates, x1, ln_g, ln_b, row_block0, tn):
    n, d = x1.shape
    const = lambda i: (0, 0)
    return pl.pallas_call(
        _combine_kernel,
        out_shape=jax.ShapeDtypeStruct((n, d), F32),
        grid=(n // tn,),
        in_specs=[
            pl.BlockSpec((TOP_K, tn, d), lambda i: (0, row_block0 + i, 0)),
            pl.BlockSpec((tn, TOP_K), lambda i: (i, 0)),
            pl.BlockSpec((tn, d), lambda i: (i, 0)),
            pl.BlockSpec((1, d), const),
            pl.BlockSpec((1, d), const),
        ],
        out_specs=pl.BlockSpec((tn, d), lambda i: (i, 0)),
        compiler_params=_cparams(("parallel",)),
        name="combine",
    )(y_rows, gates, x1, ln_g, ln_b)


def _route(idx, tm):
    n = idx.shape[0]
    nk = n * TOP_K
    onehot = (idx[:, :, None] == jnp.arange(N_EXPERTS, dtype=jnp.int32)[None, None, :]).sum(axis=1).astype(jnp.int32)
    csum = jnp.cumsum(onehot, axis=0)
    rank = jnp.take_along_axis(csum - onehot, idx, axis=1)
    sizes = csum[-1]
    nblk = (sizes + tm - 1) // tm
    bends = jnp.cumsum(nblk)
    bstart = bends - nblk
    dest = bstart[idx] * tm + rank
    nb = -(-nk // tm) + N_EXPERTS
    blk = jnp.arange(nb, dtype=jnp.int32)
    block_e = jnp.minimum(jnp.searchsorted(bends, blk, side="right"), N_EXPERTS - 1).astype(jnp.int32)
    n_used = bends[-1].astype(jnp.int32)
    valid = jnp.clip(sizes[block_e] - (blk - bstart[block_e]) * tm, 0, tm)
    valid = jnp.where(blk < n_used, valid, 0).astype(jnp.int32)
    tok = jnp.broadcast_to(jnp.arange(n, dtype=jnp.int32)[:, None], (n, TOP_K))
    row_tok = jnp.zeros((nb * tm,), jnp.int32).at[dest.reshape(-1)].set(tok.reshape(-1))
    return dest, row_tok, block_e, valid, n_used.reshape(1)


def _pick(n, pref):
    t = min(n, pref)
    while n % t:
        t //= 2
    return t


def kernel(x_prompt, x_sample, cache_k, cache_v, cache_logf, state_conv, state_lru, w_in, b_f, w_conv, b_conv, w_a, b_a, w_i, b_i, lam, w_out, ln1_g, ln1_b, w_router, b_router, w_gu, b_gu, w_down, b_down, ln2_g, ln2_b):
    assert w_in.shape[0] == DEPTH
    bp, tp, d = x_prompt.shape
    bs, ts, _ = x_sample.shape
    past = cache_k.shape[2]
    lru_w = w_conv.shape[-1]
    np_, ns = bp * tp, bs * ts

    win = w_in[0]
    f0 = 3 * FOX_WIDTH
    w_main = jnp.concatenate([win[:, :f0], win[:, f0 + FOX_HEADS:]], axis=1).astype(BF16)
    w_f = jnp.pad(win[:, f0:f0 + FOX_HEADS], ((0, 0), (0, 128 - FOX_HEADS))).astype(BF16)
    bf2 = b_f[0].reshape(1, FOX_HEADS)
    gpb = 4
    bd = w_a.shape[-1]
    eye = jnp.eye(gpb, dtype=F32)

    def blockdiag(w):
        wg = w.reshape(LRU_BLOCKS // gpb, gpb, bd, bd)
        return jnp.einsum("gacd,ab->gacbd", wg, eye).reshape(LRU_BLOCKS // gpb, gpb * bd, gpb * bd)

    w_gate = jnp.concatenate([blockdiag(w_a[0]), blockdiag(w_i[0])], axis=-1).astype(BF16)
    w_top = w_out[0, :FOX_WIDTH].astype(BF16)
    w_bot = w_out[0, FOX_WIDTH:].astype(BF16)
    wr = w_router[0]
    wr_hi = wr.astype(BF16)
    wr_lo = (wr - wr_hi.astype(F32)).astype(BF16)
    row = lambda a: a.reshape(1, -1)

    def mixers(x, batch, seq, fox_fn, conv0, h0):
        n = batch * seq
        q, k, v, xr, yg, logf = _in_proj(x.reshape(n, d), w_main, w_f, bf2, _pick(n, 512))
        fox_o = fox_fn(q, k, v, logf.reshape(batch, seq, FOX_HEADS))
        lru_o, conv_new, h_last = _lru(xr, yg, conv0, h0, w_conv[0], row(b_conv[0]), w_gate,
                                       row(b_a[0]), row(b_i[0]), row(lam[0]), batch, seq, _pick(seq, 256))
        x1, x1b, idx, gates = _mix_norm(fox_o, lru_o, x.reshape(n, d), w_top, w_bot, row(ln1_g[0]), row(ln1_b[0]),
                                        wr_hi, wr_lo, row(b_router[0]), _pick(n, 512))
        state = (k.reshape(1, batch, seq, FOX_HEADS, FOX_HEAD_DIM), v.reshape(1, batch, seq, FOX_HEADS, FOX_HEAD_DIM),
                 logf.reshape(1, batch, seq, FOX_HEADS), conv_new[None], h_last.reshape(1, batch, lru_w))
        return x1, x1b, idx, gates, state

    def fox_p(q, k, v, logf):
        return _fox_prompt(q, k, v, jnp.cumsum(logf, axis=1), bp, tp, _pick(tp, 512))

    def fox_s(q, k, v, logf):
        c_all = jnp.cumsum(jnp.concatenate([cache_logf[0], logf], axis=1), axis=1)
        return _fox_sample(q, k, v, cache_k[0].reshape(bs, past, FOX_WIDTH), cache_v[0].reshape(bs, past, FOX_WIDTH),
                           c_all, bs, ts, _pick(past, 512))

    x1p, x1bp, idxp, gp, state_p = mixers(x_prompt, bp, tp, fox_p,
                                          jnp.zeros((bp, CONV_WIDTH - 1, lru_w), F32), jnp.zeros((bp, 1, lru_w), F32))
    x1s, x1bs, idxs, gs, state_s = mixers(x_sample, bs, ts, fox_s, state_conv[0], state_lru[0].reshape(bs, 1, lru_w))

    tm_e, tf_e, sub_e = 1024, 256, 512
    x1b = jnp.concatenate([x1bp, x1bs], axis=0)
    idx = jnp.concatenate([idxp, idxs], axis=0)
    dest, row_tok, block_e, block_valid, n_used = _route(idx, tm_e)
    x_pad = x1b[row_tok]
    y_pad = _experts(x_pad, block_e, block_valid, n_used, w_gu[0], b_gu[0].reshape(N_EXPERTS, 1, -1),
                     w_down[0], b_down[0].reshape(N_EXPERTS, 1, -1), tm_e, tf_e, sub_e)
    y_rows = y_pad[dest.T]

    tn = _pick(ns, 256)
    yp = _combine(y_rows, gp, x1p, row(ln2_g[0]), row(ln2_b[0]), 0, tn)
    ys = _combine(y_rows, gs, x1s, row(ln2_g[0]), row(ln2_b[0]), np_ // tn, tn)
    return (yp.reshape(bp, tp, d), ys.reshape(bs, ts, d)) + state_p + state_s
```

```python
import functools

import jax
import jax.numpy as jnp
from jax import lax
from jax.experimental import pallas as pl
from jax.experimental.pallas import tpu as pltpu

F32 = jnp.float32
BF16 = jnp.bfloat16

FOX_HEADS = 8
FOX_HEAD_DIM = 128
FOX_WIDTH = FOX_HEADS * FOX_HEAD_DIM
LRU_BLOCKS = 16
CONV_WIDTH = 4
LRU_C = 8.0
N_EXPERTS = 32
TOP_K = 4
SWIGLU_LIMIT = 7.0
SWIGLU_ALPHA = 1.702
LN_EPS = 1e-5
DEPTH = 1
DEEPNORM_ALPHA = (2.0 * DEPTH) ** 0.25

VMEM_LIMIT = 56 * 1024 * 1024


def _cparams(sem):
    return pltpu.CompilerParams(dimension_semantics=sem, vmem_limit_bytes=VMEM_LIMIT)


def _log_sigmoid(x):
    return jnp.minimum(x, 0.0) - jnp.log1p(jnp.exp(-jnp.abs(x)))


def _layer_norm(z, g, b):
    mu = jnp.mean(z, axis=-1, keepdims=True)
    zc = z - mu
    var = jnp.mean(zc * zc, axis=-1, keepdims=True)
    return zc * lax.rsqrt(var + LN_EPS) * g + b


def _in_proj_kernel(x_ref, w_ref, wf_ref, bf_ref, q_ref, k_ref, v_ref, xr_ref, yg_ref, lf_ref, xb_ref):
    j = pl.program_id(1)

    @pl.when(j == 0)
    def _():
        xb_ref[...] = x_ref[...].astype(BF16)
        zf = jnp.dot(xb_ref[...], wf_ref[...], preferred_element_type=F32)
        lf_ref[...] = _log_sigmoid(zf[:, :FOX_HEADS] + bf_ref[...])

    z = jnp.dot(xb_ref[...], w_ref[...], preferred_element_type=F32)
    for jj, ref in enumerate((q_ref, k_ref, v_ref, xr_ref, yg_ref)):
        @pl.when(j == jj)
        def _(ref=ref):
            ref[...] = z.astype(ref.dtype)


def _in_proj(x2d, w_main, w_f, b_f, tm):
    n, d = x2d.shape
    wcol = FOX_WIDTH
    row = lambda i, j: (i, 0)
    outs = [jax.ShapeDtypeStruct((n, wcol), BF16)] + [jax.ShapeDtypeStruct((n, wcol), F32)] * 4
    outs.append(jax.ShapeDtypeStruct((n, FOX_HEADS), F32))
    return pl.pallas_call(
        _in_proj_kernel,
        out_shape=outs,
        grid=(n // tm, 5),
        in_specs=[
            pl.BlockSpec((tm, d), row),
            pl.BlockSpec((d, wcol), lambda i, j: (0, j)),
            pl.BlockSpec((d, 128), lambda i, j: (0, 0)),
            pl.BlockSpec((1, FOX_HEADS), lambda i, j: (0, 0)),
        ],
        out_specs=[pl.BlockSpec((tm, wcol), row)] * 5 + [pl.BlockSpec((tm, FOX_HEADS), row)],
        scratch_shapes=[pltpu.VMEM((tm, d), BF16)],
        compiler_params=_cparams(("parallel", "arbitrary")),
        name="in_proj",
    )(x2d, w_main, w_f, b_f)


def _attn_heads_step(q_ref, k_ref, v_ref, cq_ref, ck_ref, m_ref, l_ref, acc_ref, masked):
    tq = q_ref.shape[0]
    tk = k_ref.shape[0]
    scale = FOX_HEAD_DIM ** -0.5

    def head(h, carry):
        sl = pl.ds(pl.multiple_of(h * FOX_HEAD_DIM, FOX_HEAD_DIM), FOX_HEAD_DIM)
        q = q_ref[:, sl]
        k = k_ref[:, sl].astype(BF16)
        v = v_ref[:, sl].astype(BF16)
        s = lax.dot_general(q, k, (((1,), (1,)), ((), ())), preferred_element_type=F32) * scale
        s = s + cq_ref[h] - ck_ref[h]
        if masked:
            rows = lax.broadcasted_iota(jnp.int32, (tq, tk), 0)
            cols = lax.broadcasted_iota(jnp.int32, (tq, tk), 1)
            s = jnp.where(cols <= rows, s, -jnp.inf)
        m_prev = m_ref[h]
        m_new = jnp.maximum(m_prev, jnp.max(s, axis=-1, keepdims=True))
        alpha = jnp.exp(m_prev - m_new)
        p = jnp.exp(s - m_new)
        l_ref[h] = alpha * l_ref[h] + jnp.sum(p, axis=-1, keepdims=True)
        acc_ref[:, sl] = alpha * acc_ref[:, sl] + jnp.dot(p.astype(BF16), v, preferred_element_type=F32)
        m_ref[h] = m_new
        return carry

    lax.fori_loop(0, FOX_HEADS, head, 0)


def _attn_init(m_ref, l_ref, acc_ref):
    m_ref[...] = jnp.full(m_ref.shape, -jnp.inf, F32)
    l_ref[...] = jnp.zeros(l_ref.shape, F32)
    acc_ref[...] = jnp.zeros(acc_ref.shape, F32)


def _attn_finish(o_ref, l_ref, acc_ref):
    for h in range(FOX_HEADS):
        sl = slice(h * FOX_HEAD_DIM, (h + 1) * FOX_HEAD_DIM)
        o_ref[:, sl] = (acc_ref[:, sl] / l_ref[h]).astype(o_ref.dtype)


def _fox_prompt_kernel(q_ref, k_ref, v_ref, cq_ref, ck_ref, o_ref, m_ref, l_ref, acc_ref):
    qi = pl.program_id(1)
    ki = pl.program_id(2)

    @pl.when(ki == 0)
    def _():
        _attn_init(m_ref, l_ref, acc_ref)

    @pl.when(ki < qi)
    def _():
        _attn_heads_step(q_ref, k_ref, v_ref, cq_ref.at[0], ck_ref.at[0], m_ref, l_ref, acc_ref, False)

    @pl.when(ki == qi)
    def _():
        _attn_heads_step(q_ref, k_ref, v_ref, cq_ref.at[0], ck_ref.at[0], m_ref, l_ref, acc_ref, True)
        _attn_finish(o_ref, l_ref, acc_ref)


def _fox_prompt(q, k, v, c, batch, seq, tq):
    nq = seq // tq
    width = q.shape[1]
    cq = jnp.transpose(c, (0, 2, 1))[..., None]
    ck = jnp.transpose(c, (0, 2, 1))[:, :, None, :]
    kv_map = lambda b, qi, ki: (b * nq + jnp.minimum(ki, qi), 0)
    return pl.pallas_call(
        _fox_prompt_kernel,
        out_shape=jax.ShapeDtypeStruct(q.shape, BF16),
        grid=(batch, nq, nq),
        in_specs=[
            pl.BlockSpec((tq, width), lambda b, qi, ki: (b * nq + qi, 0)),
            pl.BlockSpec((tq, width), kv_map),
            pl.BlockSpec((tq, width), kv_map),
            pl.BlockSpec((1, FOX_HEADS, tq, 1), lambda b, qi, ki: (b, 0, qi, 0)),
            pl.BlockSpec((1, FOX_HEADS, 1, tq), lambda b, qi, ki: (b, 0, 0, jnp.minimum(ki, qi))),
        ],
        out_specs=pl.BlockSpec((tq, width), lambda b, qi, ki: (b * nq + qi, 0)),
        scratch_shapes=[
            pltpu.VMEM((FOX_HEADS, tq, 1), F32),
            pltpu.VMEM((FOX_HEADS, tq, 1), F32),
            pltpu.VMEM((tq, width), F32),
        ],
        compiler_params=_cparams(("parallel", "parallel", "arbitrary")),
        name="fox_prompt",
    )(q, k, v, cq, ck)


def _fox_sample_kernel(q_ref, kc_ref, vc_ref, kn_ref, vn_ref, cq_ref, ckc_ref, ckn_ref, o_ref,
                       m_ref, l_ref, acc_ref):
    j = pl.program_id(1)

    @pl.when(j == 0)
    def _():
        _attn_init(m_ref, l_ref, acc_ref)

    _attn_heads_step(q_ref, kc_ref.at[0], vc_ref.at[0], cq_ref.at[0], ckc_ref.at[0], m_ref, l_ref, acc_ref, False)

    @pl.when(j == pl.num_programs(1) - 1)
    def _():
        _attn_heads_step(q_ref, kn_ref, vn_ref, cq_ref.at[0], ckn_ref.at[0], m_ref, l_ref, acc_ref, True)
        _attn_finish(o_ref, l_ref, acc_ref)


def _fox_sample(q, k, v, cache_k, cache_v, c_all, batch, seq, tk):
    past = cache_k.shape[1]
    width = q.shape[1]
    ct = jnp.transpose(c_all, (0, 2, 1))
    cq = ct[:, :, past:, None]
    ckc = ct[:, :, None, :past]
    ckn = ct[:, :, None, past:]
    new_map = lambda b, j: (b, 0)
    return pl.pallas_call(
        _fox_sample_kernel,
        out_shape=jax.ShapeDtypeStruct(q.shape, BF16),
        grid=(batch, past // tk),
        in_specs=[
            pl.BlockSpec((seq, width), new_map),
            pl.BlockSpec((1, tk, width), lambda b, j: (b, j, 0)),
            pl.BlockSpec((1, tk, width), lambda b, j: (b, j, 0)),
            pl.BlockSpec((seq, width), new_map),
            pl.BlockSpec((seq, width), new_map),
            pl.BlockSpec((1, FOX_HEADS, seq, 1), lambda b, j: (b, 0, 0, 0)),
            pl.BlockSpec((1, FOX_HEADS, 1, tk), lambda b, j: (b, 0, 0, j)),
            pl.BlockSpec((1, FOX_HEADS, 1, seq), lambda b, j: (b, 0, 0, 0)),
        ],
        out_specs=pl.BlockSpec((seq, width), new_map),
        scratch_shapes=[
            pltpu.VMEM((FOX_HEADS, seq, 1), F32),
            pltpu.VMEM((FOX_HEADS, seq, 1), F32),
            pltpu.VMEM((seq, width), F32),
        ],
        compiler_params=_cparams(("parallel", "arbitrary")),
        name="fox_sample",
    )(q, cache_k, cache_v, k, v, cq, ckc, ckn)


_HALO = 8


def _lru_kernel(xr_ref, yg_ref, conv0_ref, h0_ref, wc_ref, bc_ref, wg_ref, ba_ref, bi_ref, lam_ref,
                o_ref, convo_ref, hlast_ref, xp_ref, a_ref, b_ref, h_ref):
    t = pl.program_id(1)
    tt, width = xr_ref.shape
    tail = CONV_WIDTH - 1
    lo = _HALO - tail

    @pl.when(t == 0)
    def _():
        xp_ref[lo:_HALO, :] = conv0_ref[0]
        h_ref[...] = h0_ref[0]

    xp_ref[_HALO:_HALO + tt, :] = xr_ref[...]
    xc = bc_ref[...] + xp_ref[lo:lo + tt, :] * wc_ref[0:1, :]
    for j in range(1, CONV_WIDTH):
        xc = xc + xp_ref[lo + j:lo + j + tt, :] * wc_ref[j:j + 1, :]
    new_tail = xp_ref[lo + tt:_HALO + tt, :]
    xp_ref[lo:_HALO, :] = new_tail

    xcb = xc.astype(BF16)
    gw = wg_ref.shape[1]
    lam = lam_ref[...]
    neg_sp = -(jnp.maximum(-lam, 0.0) + jnp.log1p(jnp.exp(-jnp.abs(lam))))
    for g in range(width // gw):
        sl = slice(g * gw, (g + 1) * gw)
        z = jnp.dot(xcb[:, sl], wg_ref[g], preferred_element_type=F32)
        r = jax.nn.sigmoid(z[:, :gw] + ba_ref[:, sl])
        i = jax.nn.sigmoid(z[:, gw:] + bi_ref[:, sl])
        log_a = LRU_C * r * neg_sp[:, sl]
        a = jnp.exp(log_a)
        a_ref[:, sl] = a
        b_ref[:, sl] = jnp.sqrt(-jnp.tanh(log_a) * (a * a + 1.0)) * i * xc[:, sl]

    def step(s, h):
        row = pl.ds(s, 1)
        h = a_ref[row, :] * h + b_ref[row, :]
        b_ref[row, :] = h
        return h

    h = lax.fori_loop(0, tt, step, h_ref[...], unroll=8)
    h_ref[...] = h
    o_ref[...] = (b_ref[...] * jax.nn.gelu(yg_ref[...])).astype(o_ref.dtype)

    @pl.when(t == pl.num_programs(1) - 1)
    def _():
        convo_ref[0] = new_tail
        hlast_ref[0] = h


def _lru(xr, yg, conv0, h0, w_conv, b_conv, w_gate, b_a, b_i, lam, batch, seq, tt):
    width = xr.shape[1]
    nt = seq // tt
    gw = w_gate.shape[1]
    rows = lambda b, t: (b * nt + t, 0)
    const2 = lambda b, t: (0, 0)
    per_b = lambda b, t: (b, 0, 0)
    tail = CONV_WIDTH - 1
    return pl.pallas_call(
        _lru_kernel,
        out_shape=[
            jax.ShapeDtypeStruct((batch * seq, width), BF16),
            jax.ShapeDtypeStruct((batch, tail, width), F32),
            jax.ShapeDtypeStruct((batch, 1, width), F32),
        ],
        grid=(batch, nt),
        in_specs=[
            pl.BlockSpec((tt, width), rows),
            pl.BlockSpec((tt, width), rows),
            pl.BlockSpec((1, tail, width), per_b),
            pl.BlockSpec((1, 1, width), per_b),
            pl.BlockSpec((CONV_WIDTH, width), const2),
            pl.BlockSpec((1, width), const2),
            pl.BlockSpec((width // gw, gw, 2 * gw), lambda b, t: (0, 0, 0)),
            pl.BlockSpec((1, width), const2),
            pl.BlockSpec((1, width), const2),
            pl.BlockSpec((1, width), const2),
        ],
        out_specs=[
            pl.BlockSpec((tt, width), rows),
            pl.BlockSpec((1, tail, width), per_b),
            pl.BlockSpec((1, 1, width), per_b),
        ],
        scratch_shapes=[
            pltpu.VMEM((_HALO + tt, width), F32),
            pltpu.VMEM((tt, width), F32),
            pltpu.VMEM((tt, width), F32),
            pltpu.VMEM((1, width), F32),
        ],
        compiler_params=_cparams(("parallel", "arbitrary")),
        name="lru",
    )(xr, yg, conv0, h0, w_conv, b_conv, w_gate, b_a, b_i, lam)


def _mix_norm_kernel(fox_ref, lru_ref, x_ref, wt_ref, wb_ref, g_ref, b_ref, wrh_ref, wrl_ref, br_ref,
                     x1_ref, x1b_ref, idx_ref, gate_ref):
    mix = jnp.dot(fox_ref[...], wt_ref[...], preferred_element_type=F32)
    mix = mix + jnp.dot(lru_ref[...], wb_ref[...], preferred_element_type=F32)
    x1 = _layer_norm(DEEPNORM_ALPHA * x_ref[...] + mix, g_ref[...], b_ref[...])
    x1_ref[...] = x1
    x1_hi = x1.astype(BF16)
    x1b_ref[...] = x1_hi
    x1_lo = (x1 - x1_hi.astype(F32)).astype(BF16)
    lg = jnp.dot(x1_hi, wrh_ref[...], preferred_element_type=F32)
    lg = lg + jnp.dot(x1_lo, wrh_ref[...], preferred_element_type=F32)
    lg = lg + jnp.dot(x1_hi, wrl_ref[...], preferred_element_type=F32)
    lg = lg + br_ref[...]
    lanes = lax.broadcasted_iota(jnp.int32, lg.shape, 1)
    vals = []
    for k in range(TOP_K):
        m = jnp.max(lg, axis=-1, keepdims=True)
        ix = jnp.min(jnp.where(lg == m, lanes, N_EXPERTS), axis=-1, keepdims=True)
        idx_ref[:, k:k + 1] = ix
        vals.append(m)
        lg = jnp.where(lanes == ix, -jnp.inf, lg)
    es = [jnp.exp(v - vals[0]) for v in vals]
    denom = es[0] + es[1] + es[2] + es[3]
    for k in range(TOP_K):
        gate_ref[:, k:k + 1] = es[k] / denom


def _mix_norm(fox_o, lru_o, x2d, w_top, w_bot, ln_g, ln_b, wr_hi, wr_lo, b_router, tm):
    n, d = x2d.shape
    half = fox_o.shape[1]
    row = lambda i: (i, 0)
    const = lambda i: (0, 0)
    return pl.pallas_call(
        _mix_norm_kernel,
        out_shape=[
            jax.ShapeDtypeStruct((n, d), F32),
            jax.ShapeDtypeStruct((n, d), BF16),
            jax.ShapeDtypeStruct((n, TOP_K), jnp.int32),
            jax.ShapeDtypeStruct((n, TOP_K), F32),
        ],
        grid=(n // tm,),
        in_specs=[
            pl.BlockSpec((tm, half), row),
            pl.BlockSpec((tm, half), row),
            pl.BlockSpec((tm, d), row),
            pl.BlockSpec((half, d), const),
            pl.BlockSpec((half, d), const),
            pl.BlockSpec((1, d), const),
            pl.BlockSpec((1, d), const),
            pl.BlockSpec((d, N_EXPERTS), const),
            pl.BlockSpec((d, N_EXPERTS), const),
            pl.BlockSpec((1, N_EXPERTS), const),
        ],
        out_specs=[
            pl.BlockSpec((tm, d), row),
            pl.BlockSpec((tm, d), row),
            pl.BlockSpec((tm, TOP_K), row),
            pl.BlockSpec((tm, TOP_K), row),
        ],
        compiler_params=_cparams(("parallel",)),
        name="mix_norm",
    )(fox_o, lru_o, x2d, w_top, w_bot, ln_g, ln_b, wr_hi, wr_lo, b_router)


def _expert_kernel(be_ref, bv_ref, nu_ref, x_ref, wg_ref, wu_ref, bg_ref, bu_ref, wd_ref, bd_ref,
                   o_ref, acc_ref, *, sub):
    rb = pl.program_id(0)
    f = pl.program_id(1)
    last_f = pl.num_programs(1) - 1
    valid = bv_ref[rb]
    tm = x_ref.shape[0]

    def compute(nrows):
        x = x_ref[0:nrows, :]
        hg = jnp.dot(x, wg_ref[0].astype(BF16), preferred_element_type=F32) + bg_ref[0]
        hu = jnp.dot(x, wu_ref[0].astype(BF16), preferred_element_type=F32) + bu_ref[0]
        gate = jnp.minimum(hg, SWIGLU_LIMIT)
        up = jnp.clip(hu, -SWIGLU_LIMIT, SWIGLU_LIMIT)
        act = (up + 1.0) * gate * jax.nn.sigmoid(SWIGLU_ALPHA * gate)
        acc_ref[0:nrows, :] += jnp.dot(act.astype(BF16), wd_ref[0].astype(BF16), preferred_element_type=F32)

    @pl.when(jnp.logical_and(valid > 0, f == 0))
    def _():
        acc_ref[...] = jnp.broadcast_to(bd_ref[0], acc_ref.shape)

    @pl.when(valid > sub)
    def _():
        compute(tm)

    @pl.when(jnp.logical_and(valid > 0, valid <= sub))
    def _():
        compute(sub)

    @pl.when(jnp.logical_and(valid > 0, f == last_f))
    def _():
        o_ref[...] = acc_ref[...].astype(o_ref.dtype)


def _experts(x_pad, block_e, block_valid, n_used, w_gu, b_gu, w_down, b_down, tm, tf, sub):
    p, d = x_pad.shape
    d_ff = w_down.shape[1]
    nf = d_ff // tf
    nb = p // tm

    def blk(rb, nu):
        return jnp.minimum(rb, nu[0] - 1)

    grid_spec = pltpu.PrefetchScalarGridSpec(
        num_scalar_prefetch=3,
        grid=(nb, nf),
        in_specs=[
            pl.BlockSpec((tm, d), lambda rb, f, be, bv, nu: (blk(rb, nu), 0)),
            pl.BlockSpec((1, d, tf), lambda rb, f, be, bv, nu: (be[rb], 0, f)),
            pl.BlockSpec((1, d, tf), lambda rb, f, be, bv, nu: (be[rb], 0, nf + f)),
            pl.BlockSpec((1, 1, tf), lambda rb, f, be, bv, nu: (be[rb], 0, f)),
            pl.BlockSpec((1, 1, tf), lambda rb, f, be, bv, nu: (be[rb], 0, nf + f)),
            pl.BlockSpec((1, tf, d), lambda rb, f, be, bv, nu: (be[rb], f, 0)),
            pl.BlockSpec((1, 1, d), lambda rb, f, be, bv, nu: (be[rb], 0, 0)),
        ],
        out_specs=pl.BlockSpec((tm, d), lambda rb, f, be, bv, nu: (blk(rb, nu), 0)),
        scratch_shapes=[pltpu.VMEM((tm, d), F32)],
    )
    return pl.pallas_call(
        functools.partial(_expert_kernel, sub=sub),
        out_shape=jax.ShapeDtypeStruct((p, d), BF16),
        grid_spec=grid_spec,
        compiler_params=_cparams(("arbitrary", "arbitrary")),
        name="experts",
    )(block_e, block_valid, n_used, x_pad, w_gu, w_gu, b_gu, b_gu, w_down, b_down)


def _combine_kernel(y_ref, gate_ref, x1_ref, g_ref, b_ref, o_ref):
    y = y_ref[0].astype(F32) * gate_ref[:, 0:1]
    for k in range(1, TOP_K):
        y = y + y_ref[k].astype(F32) * gate_ref[:, k:k + 1]
    o_ref[...] = _layer_norm(DEEPNORM_ALPHA * x1_ref[...] + y, g_ref[...], b_ref[...])


def _combine(y_rows, gates, x1, ln_g, ln_b, row_block0, tn):
    n, d = x1.shape
    const = lambda i: (0, 0)
    return pl.pallas_call(
        _combine_kernel,
        out_shape=jax.ShapeDtypeStruct((n, d), F32),
        grid=(n // tn,),
        in_specs=[
            pl.BlockSpec((TOP_K, tn, d), lambda i: (0, row_block0 + i, 0)),
            pl.BlockSpec((tn, TOP_K), lambda i: (i, 0)),
            pl.BlockSpec((tn, d), lambda i: (i, 0)),
            pl.BlockSpec((1, d), const),
            pl.BlockSpec((1, d), const),
        ],
        out_specs=pl.BlockSpec((tn, d), lambda i: (i, 0)),
        compiler_params=_cparams(("parallel",)),
        name="combine",
    )(y_rows, gates, x1, ln_g, ln_b)


def _route(idx, tm):
    n = idx.shape[0]
    nk = n * TOP_K
    onehot = (idx[:, :, None] == jnp.arange(N_EXPERTS, dtype=jnp.int32)[None, None, :]).sum(axis=1).astype(jnp.int32)
    csum = jnp.cumsum(onehot, axis=0)
    rank = jnp.take_along_axis(csum - onehot, idx, axis=1)
    sizes = csum[-1]
    nblk = (sizes + tm - 1) // tm
    bends = jnp.cumsum(nblk)
    bstart = bends - nblk
    dest = bstart[idx] * tm + rank
    nb = -(-nk // tm) + N_EXPERTS
    blk = jnp.arange(nb, dtype=jnp.int32)
    block_e = jnp.minimum(jnp.sum(bends[None, :] <= blk[:, None], axis=1), N_EXPERTS - 1).astype(jnp.int32)
    n_used = bends[-1].astype(jnp.int32)
    valid = jnp.clip(sizes[block_e] - (blk - bstart[block_e]) * tm, 0, tm)
    valid = jnp.where(blk < n_used, valid, 0).astype(jnp.int32)
    tok = jnp.broadcast_to(jnp.arange(n, dtype=jnp.int32)[:, None], (n, TOP_K))
    row_tok = (jnp.arange(nb * tm, dtype=jnp.int32) % n).at[dest.reshape(-1)].set(tok.reshape(-1))
    return dest, row_tok, block_e, valid, n_used.reshape(1)


def _pick(n, pref):
    t = min(n, pref)
    while n % t:
        t //= 2
    return t


def kernel(x_prompt, x_sample, cache_k, cache_v, cache_logf, state_conv, state_lru, w_in, b_f, w_conv, b_conv, w_a, b_a, w_i, b_i, lam, w_out, ln1_g, ln1_b, w_router, b_router, w_gu, b_gu, w_down, b_down, ln2_g, ln2_b):
    assert w_in.shape[0] == DEPTH
    bp, tp, d = x_prompt.shape
    bs, ts, _ = x_sample.shape
    past = cache_k.shape[2]
    lru_w = w_conv.shape[-1]
    np_, ns = bp * tp, bs * ts

    win = w_in[0]
    f0 = 3 * FOX_WIDTH
    w_main = jnp.concatenate([win[:, :f0], win[:, f0 + FOX_HEADS:]], axis=1).astype(BF16)
    w_f = jnp.pad(win[:, f0:f0 + FOX_HEADS], ((0, 0), (0, 128 - FOX_HEADS))).astype(BF16)
    bf2 = b_f[0].reshape(1, FOX_HEADS)
    gpb = 4
    bd = w_a.shape[-1]
    eye = jnp.eye(gpb, dtype=F32)

    def blockdiag(w):
        wg = w.reshape(LRU_BLOCKS // gpb, gpb, bd, bd)
        return jnp.einsum("gacd,ab->gacbd", wg, eye).reshape(LRU_BLOCKS // gpb, gpb * bd, gpb * bd)

    w_gate = jnp.concatenate([blockdiag(w_a[0]), blockdiag(w_i[0])], axis=-1).astype(BF16)
    w_top = w_out[0, :FOX_WIDTH].astype(BF16)
    w_bot = w_out[0, FOX_WIDTH:].astype(BF16)
    wr = w_router[0]
    wr_hi = wr.astype(BF16)
    wr_lo = (wr - wr_hi.astype(F32)).astype(BF16)
    row = lambda a: a.reshape(1, -1)

    def mixers(x, batch, seq, fox_fn, conv0, h0):
        n = batch * seq
        q, k, v, xr, yg, logf = _in_proj(x.reshape(n, d), w_main, w_f, bf2, _pick(n, 512))
        fox_o = fox_fn(q, k, v, logf.reshape(batch, seq, FOX_HEADS))
        lru_o, conv_new, h_last = _lru(xr, yg, conv0, h0, w_conv[0], row(b_conv[0]), w_gate,
                                       row(b_a[0]), row(b_i[0]), row(lam[0]), batch, seq, _pick(seq, 256))
        x1, x1b, idx, gates = _mix_norm(fox_o, lru_o, x.reshape(n, d), w_top, w_bot, row(ln1_g[0]), row(ln1_b[0]),
                                        wr_hi, wr_lo, row(b_router[0]), _pick(n, 512))
        state = (k.reshape(1, batch, seq, FOX_HEADS, FOX_HEAD_DIM), v.reshape(1, batch, seq, FOX_HEADS, FOX_HEAD_DIM),
                 logf.reshape(1, batch, seq, FOX_HEADS), conv_new[None], h_last.reshape(1, batch, lru_w))
        return x1, x1b, idx, gates, state

    def fox_p(q, k, v, logf):
        return _fox_prompt(q, k, v, jnp.cumsum(logf, axis=1), bp, tp, _pick(tp, 512))

    def fox_s(q, k, v, logf):
        c_all = jnp.cumsum(jnp.concatenate([cache_logf[0], logf], axis=1), axis=1)
        return _fox_sample(q, k, v, cache_k[0].reshape(bs, past, FOX_WIDTH), cache_v[0].reshape(bs, past, FOX_WIDTH),
                           c_all, bs, ts, _pick(past, 512))

    x1p, x1bp, idxp, gp, state_p = mixers(x_prompt, bp, tp, fox_p,
                                          jnp.zeros((bp, CONV_WIDTH - 1, lru_w), F32), jnp.zeros((bp, 1, lru_w), F32))
    x1s, x1bs, idxs, gs, state_s = mixers(x_sample, bs, ts, fox_s, state_conv[0], state_lru[0].reshape(bs, 1, lru_w))

    tm_e, tf_e, sub_e = 1024, 256, 512
    x1b = jnp.concatenate([x1bp, x1bs], axis=0)
    idx = jnp.concatenate([idxp, idxs], axis=0)
    dest, row_tok, block_e, block_valid, n_used = _route(idx, tm_e)
    x_pad = x1b[row_tok]
    y_pad = _experts(x_pad, block_e, block_valid, n_used, w_gu[0], b_gu[0].reshape(N_EXPERTS, 1, -1),
                     w_down[0], b_down[0].reshape(N_EXPERTS, 1, -1), tm_e, tf_e, sub_e)
    y_rows = y_pad[dest.T]

    tn = _pick(ns, 256)
    yp = _combine(y_rows, gp, x1p, row(ln2_g[0]), row(ln2_b[0]), 0, tn)
    ys = _combine(y_rows, gs, x1s, row(ln2_g[0]), row(ln2_b[0]), np_ // tn, tn)
    return (yp.reshape(bp, tp, d), ys.reshape(bs, ts, d)) + state_p + state_s
```

```python
import functools

import jax
import jax.numpy as jnp
from jax import lax
from jax.experimental import pallas as pl
from jax.experimental.pallas import tpu as pltpu

F32 = jnp.float32
BF16 = jnp.bfloat16

FOX_HEADS = 8
FOX_HEAD_DIM = 128
FOX_WIDTH = FOX_HEADS * FOX_HEAD_DIM
LRU_BLOCKS = 16
CONV_WIDTH = 4
LRU_C = 8.0
N_EXPERTS = 32
TOP_K = 4
SWIGLU_LIMIT = 7.0
SWIGLU_ALPHA = 1.702
LN_EPS = 1e-5
DEPTH = 1
DEEPNORM_ALPHA = (2.0 * DEPTH) ** 0.25
LOG2E = 1.4426950408889634
Q_SCALE = FOX_HEAD_DIM ** -0.5 * LOG2E

VMEM_LIMIT = 56 * 1024 * 1024


def _cparams(sem):
    return pltpu.CompilerParams(dimension_semantics=sem, vmem_limit_bytes=VMEM_LIMIT)


def _log_sigmoid(x):
    return jnp.minimum(x, 0.0) - jnp.log1p(jnp.exp(-jnp.abs(x)))


def _layer_norm(z, g, b):
    mu = jnp.mean(z, axis=-1, keepdims=True)
    zc = z - mu
    var = jnp.mean(zc * zc, axis=-1, keepdims=True)
    return zc * lax.rsqrt(var + LN_EPS) * g + b


def _in_proj_kernel(x_ref, w_ref, wf_ref, bf_ref, q_ref, k_ref, v_ref, kb_ref, vb_ref, xr_ref, yg_ref, lf_ref,
                    xb_ref):
    j = pl.program_id(1)

    @pl.when(j == 0)
    def _():
        xb_ref[...] = x_ref[...].astype(BF16)
        zf = jnp.dot(xb_ref[...], wf_ref[...], preferred_element_type=F32)
        lf_ref[...] = _log_sigmoid(zf[:, :FOX_HEADS] + bf_ref[...])

    z = jnp.dot(xb_ref[...], w_ref[...], preferred_element_type=F32)

    @pl.when(j == 0)
    def _():
        q_ref[...] = (z * Q_SCALE).astype(BF16)

    @pl.when(j == 1)
    def _():
        k_ref[...] = z
        kb_ref[...] = z.astype(BF16)

    @pl.when(j == 2)
    def _():
        v_ref[...] = z
        vb_ref[...] = z.astype(BF16)

    @pl.when(j == 3)
    def _():
        xr_ref[...] = z

    @pl.when(j == 4)
    def _():
        yg_ref[...] = z


def _in_proj(x2d, w_main, w_f, b_f, tm):
    n, d = x2d.shape
    wcol = FOX_WIDTH
    row = lambda i, j: (i, 0)
    dts = (BF16, F32, F32, BF16, BF16, F32, F32)
    outs = [jax.ShapeDtypeStruct((n, wcol), dt) for dt in dts]
    outs.append(jax.ShapeDtypeStruct((n, FOX_HEADS), F32))
    return pl.pallas_call(
        _in_proj_kernel,
        out_shape=outs,
        grid=(n // tm, 5),
        in_specs=[
            pl.BlockSpec((tm, d), row),
            pl.BlockSpec((d, wcol), lambda i, j: (0, j)),
            pl.BlockSpec((d, 128), lambda i, j: (0, 0)),
            pl.BlockSpec((1, FOX_HEADS), lambda i, j: (0, 0)),
        ],
        out_specs=[pl.BlockSpec((tm, wcol), row)] * len(dts) + [pl.BlockSpec((tm, FOX_HEADS), row)],
        scratch_shapes=[pltpu.VMEM((tm, d), BF16)],
        compiler_params=_cparams(("parallel", "arbitrary")),
        name="in_proj",
    )(x2d, w_main, w_f, b_f)


_NT = (((1,), (1,)), ((), ()))


def _attn_init(m_ref, l_ref, acc_ref):
    m_ref[...] = jnp.full(m_ref.shape, -jnp.inf, F32)
    l_ref[...] = jnp.zeros(l_ref.shape, F32)
    acc_ref[...] = jnp.zeros(acc_ref.shape, F32)


def _fox_prompt_kernel(qi_ref, ki_ref, q_ref, k_ref, v_ref, ck_ref, o_ref,
                       m_ref, l_ref, acc_ref, s_ref, p_ref, a_ref, *, rs):
    pair = pl.program_id(1)
    qi = qi_ref[pair]
    ki = ki_ref[pair]
    tq = q_ref.shape[0]
    tk = k_ref.shape[0]
    lanes = FOX_HEAD_DIM

    @pl.when(ki == 0)
    def _():
        _attn_init(m_ref, l_ref, acc_ref)

    def all_heads(masked):
        def head(h, carry):
            sl = pl.ds(pl.multiple_of(h * lanes, lanes), lanes)
            s_ref[...] = lax.dot_general(q_ref[:, sl], k_ref[:, sl], _NT, preferred_element_type=F32)
            ck = ck_ref[0, h]

            def rows(r, c2):
                r0 = pl.multiple_of(r * rs, rs)
                rsl = pl.ds(r0, rs)
                chunks = []
                for c in range(tk // lanes):
                    s = s_ref[rsl, c * lanes:(c + 1) * lanes] - ck[:, c * lanes:(c + 1) * lanes]
                    if masked:
                        row = r0 + lax.broadcasted_iota(jnp.int32, (rs, lanes), 0)
                        col = c * lanes + lax.broadcasted_iota(jnp.int32, (rs, lanes), 1)
                        s = jnp.where(col <= row, s, -jnp.inf)
                    chunks.append(s)
                mc = chunks[0]
                for s in chunks[1:]:
                    mc = jnp.maximum(mc, s)
                m_prev = m_ref[h, rsl, :]
                m_new = jnp.maximum(m_prev, jnp.max(mc, axis=-1, keepdims=True))
                alpha = jnp.exp2(m_prev - m_new)
                psum = None
                for c, s in enumerate(chunks):
                    p = jnp.exp2(s - m_new)
                    p_ref[rsl, c * lanes:(c + 1) * lanes] = p.astype(BF16)
                    psum = p if psum is None else psum + p
                l_ref[h, rsl, :] = alpha * l_ref[h, rsl, :] + psum
                m_ref[h, rsl, :] = m_new
                a_ref[rsl, :] = alpha
                return c2

            lax.fori_loop(0, tq // rs, rows, 0, unroll=2)
            pv = jnp.dot(p_ref[...], v_ref[:, sl], preferred_element_type=F32)
            acc_ref[:, sl] = a_ref[...] * acc_ref[:, sl] + pv
            return carry

        lax.fori_loop(0, FOX_HEADS, head, 0)

    @pl.when(ki < qi)
    def _():
        all_heads(False)

    @pl.when(ki == qi)
    def _():
        all_heads(True)
        for h in range(FOX_HEADS):
            sl = slice(h * lanes, (h + 1) * lanes)
            l_tot = jnp.sum(l_ref[h], axis=-1, keepdims=True)
            o_ref[:, sl] = (acc_ref[:, sl] / l_tot).astype(o_ref.dtype)


def _fox_prompt(q, kb, vb, c2, batch, seq, tq):
    nq = seq // tq
    width = q.shape[1]
    ck = jnp.transpose(c2, (0, 2, 1))[:, :, None, :]
    pairs = [(i, j) for i in range(nq) for j in range(i + 1)]
    qi_tab = jnp.asarray([p[0] for p in pairs], jnp.int32)
    ki_tab = jnp.asarray([p[1] for p in pairs], jnp.int32)
    q_map = lambda b, p, qt, kt: (b * nq + qt[p], 0)
    k_map = lambda b, p, qt, kt: (b * nq + kt[p], 0)
    grid_spec = pltpu.PrefetchScalarGridSpec(
        num_scalar_prefetch=2,
        grid=(batch, len(pairs)),
        in_specs=[
            pl.BlockSpec((tq, width), q_map),
            pl.BlockSpec((tq, width), k_map),
            pl.BlockSpec((tq, width), k_map),
            pl.BlockSpec((1, FOX_HEADS, 1, tq), lambda b, p, qt, kt: (b, 0, 0, kt[p])),
        ],
        out_specs=pl.BlockSpec((tq, width), q_map),
        scratch_shapes=[
            pltpu.VMEM((FOX_HEADS, tq, FOX_HEAD_DIM), F32),
            pltpu.VMEM((FOX_HEADS, tq, FOX_HEAD_DIM), F32),
            pltpu.VMEM((tq, width), F32),
            pltpu.VMEM((tq, tq), F32),
            pltpu.VMEM((tq, tq), BF16),
            pltpu.VMEM((tq, FOX_HEAD_DIM), F32),
        ],
    )
    return pl.pallas_call(
        functools.partial(_fox_prompt_kernel, rs=min(64, tq)),
        out_shape=jax.ShapeDtypeStruct(q.shape, BF16),
        grid_spec=grid_spec,
        compiler_params=_cparams(("parallel", "arbitrary")),
        name="fox_prompt",
    )(qi_tab, ki_tab, q, kb, vb, ck)


def _fox_sample_kernel(q_ref, kc_ref, vc_ref, kn_ref, vn_ref, ckc_ref, ckn_ref, o_ref, m_ref, l_ref, acc_ref):
    j = pl.program_id(1)
    tq = q_ref.shape[0]
    tk = kc_ref.shape[1] // FOX_HEADS
    lanes = FOX_HEAD_DIM

    @pl.when(j == 0)
    def _():
        _attn_init(m_ref, l_ref, acc_ref)

    def update(h, k, v, ck, masked):
        sl = slice(h * lanes, (h + 1) * lanes)
        s = lax.dot_general(q_ref[:, sl], k, _NT, preferred_element_type=F32) - ck
        if masked:
            row = lax.broadcasted_iota(jnp.int32, s.shape, 0)
            col = lax.broadcasted_iota(jnp.int32, s.shape, 1)
            s = jnp.where(col <= row, s, -jnp.inf)
        m_prev = m_ref[h]
        m_new = jnp.maximum(m_prev, jnp.max(s, axis=-1, keepdims=True))
        alpha = jnp.exp2(m_prev - m_new)
        p = jnp.exp2(s - m_new[:, 0:1])
        l_ref[h] = alpha * l_ref[h] + jnp.sum(p, axis=-1, keepdims=True)
        acc_ref[:, sl] = alpha * acc_ref[:, sl] + jnp.dot(p.astype(BF16), v, preferred_element_type=F32)
        m_ref[h] = m_new

    for h in range(FOX_HEADS):
        k = kc_ref[0, pl.ds(h, tk, stride=FOX_HEADS), :].astype(BF16)
        v = vc_ref[0, pl.ds(h, tk, stride=FOX_HEADS), :].astype(BF16)
        update(h, k, v, ckc_ref[0, h], False)

    @pl.when(j == pl.num_programs(1) - 1)
    def _():
        for h in range(FOX_HEADS):
            sl = slice(h * lanes, (h + 1) * lanes)
            update(h, kn_ref[:, sl], vn_ref[:, sl], ckn_ref[0, h], True)
            o_ref[:, sl] = (acc_ref[:, sl] / l_ref[h]).astype(o_ref.dtype)


def _fox_sample(q, kb, vb, cache_k, cache_v, c2_all, batch, seq, tk):
    past = cache_k.shape[1] // FOX_HEADS
    width = q.shape[1]
    ct = jnp.transpose(c2_all, (0, 2, 1))
    ckc = ct[:, :, None, :past]
    ckn = ct[:, :, None, past:]
    new_map = lambda b, j: (b, 0)
    cache_spec = pl.BlockSpec((1, tk * FOX_HEADS, FOX_HEAD_DIM), lambda b, j: (b, j, 0))
    return pl.pallas_call(
        _fox_sample_kernel,
        out_shape=jax.ShapeDtypeStruct(q.shape, BF16),
        grid=(batch, past // tk),
        in_specs=[
            pl.BlockSpec((seq, width), new_map),
            cache_spec,
            cache_spec,
            pl.BlockSpec((seq, width), new_map),
            pl.BlockSpec((seq, width), new_map),
            pl.BlockSpec((1, FOX_HEADS, 1, tk), lambda b, j: (b, 0, 0, j)),
            pl.BlockSpec((1, FOX_HEADS, 1, seq), lambda b, j: (b, 0, 0, 0)),
        ],
        out_specs=pl.BlockSpec((seq, width), new_map),
        scratch_shapes=[
            pltpu.VMEM((FOX_HEADS, seq, FOX_HEAD_DIM), F32),
            pltpu.VMEM((FOX_HEADS, seq, FOX_HEAD_DIM), F32),
            pltpu.VMEM((seq, width), F32),
        ],
        compiler_params=_cparams(("parallel", "arbitrary")),
        name="fox_sample",
    )(q, cache_k, cache_v, kb, vb, ckc, ckn)


_HALO = 8


def _lru_kernel(xr_ref, yg_ref, conv0_ref, h0_ref, wc_ref, bc_ref, wg_ref, ba_ref, bi_ref, lam_ref,
                o_ref, convo_ref, hlast_ref, xp_ref, a_ref, b_ref, h_ref):
    t = pl.program_id(1)
    tt, width = xr_ref.shape
    tail = CONV_WIDTH - 1
    lo = _HALO - tail

    @pl.when(t == 0)
    def _():
        xp_ref[lo:_HALO, :] = conv0_ref[0]
        h_ref[...] = h0_ref[0]

    xp_ref[_HALO:_HALO + tt, :] = xr_ref[...]
    xc = bc_ref[...] + xp_ref[lo:lo + tt, :] * wc_ref[0:1, :]
    for j in range(1, CONV_WIDTH):
        xc = xc + xp_ref[lo + j:lo + j + tt, :] * wc_ref[j:j + 1, :]
    new_tail = xp_ref[lo + tt:_HALO + tt, :]
    xp_ref[lo:_HALO, :] = new_tail

    xcb = xc.astype(BF16)
    gw = wg_ref.shape[1]
    lam = lam_ref[...]
    neg_sp = -(jnp.maximum(-lam, 0.0) + jnp.log1p(jnp.exp(-jnp.abs(lam))))
    for g in range(width // gw):
        sl = slice(g * gw, (g + 1) * gw)
        z = jnp.dot(xcb[:, sl], wg_ref[g], preferred_element_type=F32)
        r = jax.nn.sigmoid(z[:, :gw] + ba_ref[:, sl])
        i = jax.nn.sigmoid(z[:, gw:] + bi_ref[:, sl])
        log_a = LRU_C * r * neg_sp[:, sl]
        a = jnp.exp(log_a)
        a_ref[:, sl] = a
        b_ref[:, sl] = jnp.sqrt(-jnp.tanh(log_a) * (a * a + 1.0)) * i * xc[:, sl]

    def step(s, h):
        row = pl.ds(s, 1)
        h = a_ref[row, :] * h + b_ref[row, :]
        b_ref[row, :] = h
        return h

    h = lax.fori_loop(0, tt, step, h_ref[...], unroll=8)
    h_ref[...] = h
    o_ref[...] = (b_ref[...] * jax.nn.gelu(yg_ref[...])).astype(o_ref.dtype)

    @pl.when(t == pl.num_programs(1) - 1)
    def _():
        convo_ref[0] = new_tail
        hlast_ref[0] = h


def _lru(xr, yg, conv0, h0, w_conv, b_conv, w_gate, b_a, b_i, lam, batch, seq, tt):
    width = xr.shape[1]
    nt = seq // tt
    gw = w_gate.shape[1]
    rows = lambda b, t: (b * nt + t, 0)
    const2 = lambda b, t: (0, 0)
    per_b = lambda b, t: (b, 0, 0)
    tail = CONV_WIDTH - 1
    return pl.pallas_call(
        _lru_kernel,
        out_shape=[
            jax.ShapeDtypeStruct((batch * seq, width), BF16),
            jax.ShapeDtypeStruct((batch, tail, width), F32),
            jax.ShapeDtypeStruct((batch, 1, width), F32),
        ],
        grid=(batch, nt),
        in_specs=[
            pl.BlockSpec((tt, width), rows),
            pl.BlockSpec((tt, width), rows),
            pl.BlockSpec((1, tail, width), per_b),
            pl.BlockSpec((1, 1, width), per_b),
            pl.BlockSpec((CONV_WIDTH, width), const2),
            pl.BlockSpec((1, width), const2),
            pl.BlockSpec((width // gw, gw, 2 * gw), lambda b, t: (0, 0, 0)),
            pl.BlockSpec((1, width), const2),
            pl.BlockSpec((1, width), const2),
            pl.BlockSpec((1, width), const2),
        ],
        out_specs=[
            pl.BlockSpec((tt, width), rows),
            pl.BlockSpec((1, tail, width), per_b),
            pl.BlockSpec((1, 1, width), per_b),
        ],
        scratch_shapes=[
            pltpu.VMEM((_HALO + tt, width), F32),
            pltpu.VMEM((tt, width), F32),
            pltpu.VMEM((tt, width), F32),
            pltpu.VMEM((1, width), F32),
        ],
        compiler_params=_cparams(("parallel", "arbitrary")),
        name="lru",
    )(xr, yg, conv0, h0, w_conv, b_conv, w_gate, b_a, b_i, lam)


def _mix_norm_kernel(fox_ref, lru_ref, x_ref, wt_ref, wb_ref, g_ref, b_ref, wrh_ref, wrl_ref, br_ref,
                     x1_ref, x1b_ref, idx_ref, gate_ref):
    mix = jnp.dot(fox_ref[...], wt_ref[...], preferred_element_type=F32)
    mix = mix + jnp.dot(lru_ref[...], wb_ref[...], preferred_element_type=F32)
    x1 = _layer_norm(DEEPNORM_ALPHA * x_ref[...] + mix, g_ref[...], b_ref[...])
    x1_ref[...] = x1
    x1_hi = x1.astype(BF16)
    x1b_ref[...] = x1_hi
    x1_lo = (x1 - x1_hi.astype(F32)).astype(BF16)
    lg = jnp.dot(x1_hi, wrh_ref[...], preferred_element_type=F32)
    lg = lg + jnp.dot(x1_lo, wrh_ref[...], preferred_element_type=F32)
    lg = lg + jnp.dot(x1_hi, wrl_ref[...], preferred_element_type=F32)
    lg = lg + br_ref[...]
    lanes = lax.broadcasted_iota(jnp.int32, lg.shape, 1)
    vals = []
    for k in range(TOP_K):
        m = jnp.max(lg, axis=-1, keepdims=True)
        ix = jnp.min(jnp.where(lg == m, lanes, N_EXPERTS), axis=-1, keepdims=True)
        idx_ref[:, k:k + 1] = ix
        vals.append(m)
        lg = jnp.where(lanes == ix, -jnp.inf, lg)
    es = [jnp.exp(v - vals[0]) for v in vals]
    denom = es[0] + es[1] + es[2] + es[3]
    for k in range(TOP_K):
        gate_ref[:, k:k + 1] = es[k] / denom


def _mix_norm(fox_o, lru_o, x2d, w_top, w_bot, ln_g, ln_b, wr_hi, wr_lo, b_router, tm):
    n, d = x2d.shape
    half = fox_o.shape[1]
    row = lambda i: (i, 0)
    const = lambda i: (0, 0)
    return pl.pallas_call(
        _mix_norm_kernel,
        out_shape=[
            jax.ShapeDtypeStruct((n, d), F32),
            jax.ShapeDtypeStruct((n, d), BF16),
            jax.ShapeDtypeStruct((n, TOP_K), jnp.int32),
            jax.ShapeDtypeStruct((n, TOP_K), F32),
        ],
        grid=(n // tm,),
        in_specs=[
            pl.BlockSpec((tm, half), row),
            pl.BlockSpec((tm, half), row),
            pl.BlockSpec((tm, d), row),
            pl.BlockSpec((half, d), const),
            pl.BlockSpec((half, d), const),
            pl.BlockSpec((1, d), const),
            pl.BlockSpec((1, d), const),
            pl.BlockSpec((d, N_EXPERTS), const),
            pl.BlockSpec((d, N_EXPERTS), const),
            pl.BlockSpec((1, N_EXPERTS), const),
        ],
        out_specs=[
            pl.BlockSpec((tm, d), row),
            pl.BlockSpec((tm, d), row),
            pl.BlockSpec((tm, TOP_K), row),
            pl.BlockSpec((tm, TOP_K), row),
        ],
        compiler_params=_cparams(("parallel",)),
        name="mix_norm",
    )(fox_o, lru_o, x2d, w_top, w_bot, ln_g, ln_b, wr_hi, wr_lo, b_router)


def _expert_kernel(be_ref, bv_ref, nu_ref, x_ref, wg_ref, wu_ref, bg_ref, bu_ref, wd_ref, bd_ref,
                   o_ref, acc_ref, *, sub):
    rb = pl.program_id(0)
    f = pl.program_id(1)
    last_f = pl.num_programs(1) - 1
    valid = bv_ref[rb]
    tm = x_ref.shape[0]

    def compute(nrows):
        x = x_ref[0:nrows, :]
        hg = jnp.dot(x, wg_ref[0].astype(BF16), preferred_element_type=F32) + bg_ref[0]
        hu = jnp.dot(x, wu_ref[0].astype(BF16), preferred_element_type=F32) + bu_ref[0]
        gate = jnp.minimum(hg, SWIGLU_LIMIT)
        up = jnp.clip(hu, -SWIGLU_LIMIT, SWIGLU_LIMIT)
        act = (up + 1.0) * gate * jax.nn.sigmoid(SWIGLU_ALPHA * gate)
        acc_ref[0:nrows, :] += jnp.dot(act.astype(BF16), wd_ref[0].astype(BF16), preferred_element_type=F32)

    @pl.when(jnp.logical_and(valid > 0, f == 0))
    def _():
        acc_ref[...] = jnp.broadcast_to(bd_ref[0], acc_ref.shape)

    @pl.when(valid > sub)
    def _():
        compute(tm)

    @pl.when(jnp.logical_and(valid > 0, valid <= sub))
    def _():
        compute(sub)

    @pl.when(jnp.logical_and(valid > 0, f == last_f))
    def _():
        o_ref[...] = acc_ref[...].astype(o_ref.dtype)


def _experts(x_pad, block_e, block_valid, n_used, w_gu, b_gu, w_down, b_down, tm, tf, sub):
    p, d = x_pad.shape
    d_ff = w_down.shape[1]
    nf = d_ff // tf
    nb = p // tm

    def blk(rb, nu):
        return jnp.minimum(rb, nu[0] - 1)

    grid_spec = pltpu.PrefetchScalarGridSpec(
        num_scalar_prefetch=3,
        grid=(nb, nf),
        in_specs=[
            pl.BlockSpec((tm, d), lambda rb, f, be, bv, nu: (blk(rb, nu), 0)),
            pl.BlockSpec((1, d, tf), lambda rb, f, be, bv, nu: (be[rb], 0, f)),
            pl.BlockSpec((1, d, tf), lambda rb, f, be, bv, nu: (be[rb], 0, nf + f)),
            pl.BlockSpec((1, 1, tf), lambda rb, f, be, bv, nu: (be[rb], 0, f)),
            pl.BlockSpec((1, 1, tf), lambda rb, f, be, bv, nu: (be[rb], 0, nf + f)),
            pl.BlockSpec((1, tf, d), lambda rb, f, be, bv, nu: (be[rb], f, 0)),
            pl.BlockSpec((1, 1, d), lambda rb, f, be, bv, nu: (be[rb], 0, 0)),
        ],
        out_specs=pl.BlockSpec((tm, d), lambda rb, f, be, bv, nu: (blk(rb, nu), 0)),
        scratch_shapes=[pltpu.VMEM((tm, d), F32)],
    )
    return pl.pallas_call(
        functools.partial(_expert_kernel, sub=sub),
        out_shape=jax.ShapeDtypeStruct((p, d), BF16),
        grid_spec=grid_spec,
        compiler_params=_cparams(("arbitrary", "arbitrary")),
        name="experts",
    )(block_e, block_valid, n_used, x_pad, w_gu, w_gu, b_gu, b_gu, w_down, b_down)


def _combine_kernel(y_ref, gate_ref, x1_ref, g_ref, b_ref, o_ref):
    y = y_ref[0].astype(F32) * gate_ref[:, 0:1]
    for k in range(1, TOP_K):
        y = y + y_ref[k].astype(F32) * gate_ref[:, k:k + 1]
    o_ref[...] = _layer_norm(DEEPNORM_ALPHA * x1_ref[...] + y, g_ref[...], b_ref[...])


def _combine(y_rows, gates, x1, ln_g, ln_b, row_block0, tn):
    n, d = x1.shape
    const = lambda i: (0, 0)
    return pl.pallas_call(
        _combine_kernel,
        out_shape=jax.ShapeDtypeStruct((n, d), F32),
        grid=(n // tn,),
        in_specs=[
            pl.BlockSpec((TOP_K, tn, d), lambda i: (0, row_block0 + i, 0)),
            pl.BlockSpec((tn, TOP_K), lambda i: (i, 0)),
            pl.BlockSpec((tn, d), lambda i: (i, 0)),
            pl.BlockSpec((1, d), const),
            pl.BlockSpec((1, d), const),
        ],
        out_specs=pl.BlockSpec((tn, d), lambda i: (i, 0)),
        compiler_params=_cparams(("parallel",)),
        name="combine",
    )(y_rows, gates, x1, ln_g, ln_b)


def _route(idx, tm):
    n = idx.shape[0]
    nk = n * TOP_K
    onehot = (idx[:, :, None] == jnp.arange(N_EXPERTS, dtype=jnp.int32)[None, None, :]).sum(axis=1).astype(jnp.int32)
    csum = jnp.cumsum(onehot, axis=0)
    rank = jnp.take_along_axis(csum - onehot, idx, axis=1)
    sizes = csum[-1]
    nblk = (sizes + tm - 1) // tm
    bends = jnp.cumsum(nblk)
    bstart = bends - nblk
    dest = bstart[idx] * tm + rank
    nb = -(-nk // tm) + N_EXPERTS
    blk = jnp.arange(nb, dtype=jnp.int32)
    block_e = jnp.minimum(jnp.sum(bends[None, :] <= blk[:, None], axis=1), N_EXPERTS - 1).astype(jnp.int32)
    n_used = bends[-1].astype(jnp.int32)
    valid = jnp.clip(sizes[block_e] - (blk - bstart[block_e]) * tm, 0, tm)
    valid = jnp.where(blk < n_used, valid, 0).astype(jnp.int32)
    tok = jnp.broadcast_to(jnp.arange(n, dtype=jnp.int32)[:, None], (n, TOP_K))
    row_tok = (jnp.arange(nb * tm, dtype=jnp.int32) % n).at[dest.reshape(-1)].set(tok.reshape(-1))
    return dest, row_tok, block_e, valid, n_used.reshape(1)


def _pick(n, pref):
    t = min(n, pref)
    while n % t:
        t //= 2
    return t


def kernel(x_prompt, x_sample, cache_k, cache_v, cache_logf, state_conv, state_lru, w_in, b_f, w_conv, b_conv, w_a, b_a, w_i, b_i, lam, w_out, ln1_g, ln1_b, w_router, b_router, w_gu, b_gu, w_down, b_down, ln2_g, ln2_b):
    assert w_in.shape[0] == DEPTH
    bp, tp, d = x_prompt.shape
    bs, ts, _ = x_sample.shape
    past = cache_k.shape[2]
    lru_w = w_conv.shape[-1]
    np_, ns = bp * tp, bs * ts

    win = w_in[0]
    f0 = 3 * FOX_WIDTH
    w_main = jnp.concatenate([win[:, :f0], win[:, f0 + FOX_HEADS:]], axis=1).astype(BF16)
    w_f = jnp.pad(win[:, f0:f0 + FOX_HEADS], ((0, 0), (0, 128 - FOX_HEADS))).astype(BF16)
    bf2 = b_f[0].reshape(1, FOX_HEADS)
    gpb = 4
    bd = w_a.shape[-1]
    eye = jnp.eye(gpb, dtype=F32)

    def blockdiag(w):
        wg = w.reshape(LRU_BLOCKS // gpb, gpb, bd, bd)
        return jnp.einsum("gacd,ab->gacbd", wg, eye).reshape(LRU_BLOCKS // gpb, gpb * bd, gpb * bd)

    w_gate = jnp.concatenate([blockdiag(w_a[0]), blockdiag(w_i[0])], axis=-1).astype(BF16)
    w_top = w_out[0, :FOX_WIDTH].astype(BF16)
    w_bot = w_out[0, FOX_WIDTH:].astype(BF16)
    wr = w_router[0]
    wr_hi = wr.astype(BF16)
    wr_lo = (wr - wr_hi.astype(F32)).astype(BF16)
    row = lambda a: a.reshape(1, -1)

    def mixers(x, batch, seq, fox_fn, conv0, h0):
        n = batch * seq
        q, k, v, kb, vb, xr, yg, logf = _in_proj(x.reshape(n, d), w_main, w_f, bf2, _pick(n, 512))
        fox_o = fox_fn(q, kb, vb, logf.reshape(batch, seq, FOX_HEADS))
        lru_o, conv_new, h_last = _lru(xr, yg, conv0, h0, w_conv[0], row(b_conv[0]), w_gate,
                                       row(b_a[0]), row(b_i[0]), row(lam[0]), batch, seq, _pick(seq, 256))
        x1, x1b, idx, gates = _mix_norm(fox_o, lru_o, x.reshape(n, d), w_top, w_bot, row(ln1_g[0]), row(ln1_b[0]),
                                        wr_hi, wr_lo, row(b_router[0]), _pick(n, 512))
        state = (k.reshape(1, batch, seq, FOX_HEADS, FOX_HEAD_DIM), v.reshape(1, batch, seq, FOX_HEADS, FOX_HEAD_DIM),
                 logf.reshape(1, batch, seq, FOX_HEADS), conv_new[None], h_last.reshape(1, batch, lru_w))
        return x1, x1b, idx, gates, state

    def fox_p(q, kb, vb, logf):
        return _fox_prompt(q, kb, vb, jnp.cumsum(logf, axis=1) * LOG2E, bp, tp, _pick(tp, 512))

    def fox_s(q, kb, vb, logf):
        c_all = jnp.cumsum(jnp.concatenate([cache_logf[0], logf], axis=1), axis=1) * LOG2E
        cache_rows = lambda c: c[0].reshape(bs, past * FOX_HEADS, FOX_HEAD_DIM)
        return _fox_sample(q, kb, vb, cache_rows(cache_k), cache_rows(cache_v), c_all, bs, ts, _pick(past, 512))

    x1p, x1bp, idxp, gp, state_p = mixers(x_prompt, bp, tp, fox_p,
                                          jnp.zeros((bp, CONV_WIDTH - 1, lru_w), F32), jnp.zeros((bp, 1, lru_w), F32))
    x1s, x1bs, idxs, gs, state_s = mixers(x_sample, bs, ts, fox_s, state_conv[0], state_lru[0].reshape(bs, 1, lru_w))

    tm_e, tf_e, sub_e = 1024, 256, 512
    x1b = jnp.concatenate([x1bp, x1bs], axis=0)
    idx = jnp.concatenate([idxp, idxs], axis=0)
    dest, row_tok, block_e, block_valid, n_used = _route(idx, tm_e)
    x_pad = x1b[row_tok]
    y_pad = _experts(x_pad, block_e, block_valid, n_used, w_gu[0], b_gu[0].reshape(N_EXPERTS, 1, -1),
                     w_down[0], b_down[0].reshape(N_EXPERTS, 1, -1), tm_e, tf_e, sub_e)
    y_rows = y_pad[dest.T]

    tn = _pick(ns, 256)
    yp = _combine(y_rows, gp, x1p, row(ln2_g[0]), row(ln2_b[0]), 0, tn)
    ys = _combine(y_rows, gs, x1s, row(ln2_g[0]), row(ln2_b[0]), np_ // tn, tn)
    return (yp.reshape(bp, tp, d), ys.reshape(bs, ts, d)) + state_p + state_s
```

```python
import functools

import jax
import jax.numpy as jnp
from jax import lax
from jax.experimental import pallas as pl
from jax.experimental.pallas import tpu as pltpu

F32 = jnp.float32
BF16 = jnp.bfloat16

FOX_HEADS = 8
FOX_HEAD_DIM = 128
FOX_WIDTH = FOX_HEADS * FOX_HEAD_DIM
LRU_BLOCKS = 16
CONV_WIDTH = 4
LRU_C = 8.0
N_EXPERTS = 32
TOP_K = 4
SWIGLU_LIMIT = 7.0
SWIGLU_ALPHA = 1.702
LN_EPS = 1e-5
DEPTH = 1
DEEPNORM_ALPHA = (2.0 * DEPTH) ** 0.25
LOG2E = 1.4426950408889634
Q_SCALE = FOX_HEAD_DIM ** -0.5 * LOG2E

VMEM_LIMIT = 56 * 1024 * 1024


def _cparams(sem):
    return pltpu.CompilerParams(dimension_semantics=sem, vmem_limit_bytes=VMEM_LIMIT)


def _log_sigmoid(x):
    return jnp.minimum(x, 0.0) - jnp.log1p(jnp.exp(-jnp.abs(x)))


def _layer_norm(z, g, b):
    mu = jnp.mean(z, axis=-1, keepdims=True)
    zc = z - mu
    var = jnp.mean(zc * zc, axis=-1, keepdims=True)
    return zc * lax.rsqrt(var + LN_EPS) * g + b


def _in_proj_kernel(x_ref, w_ref, wf_ref, bf_ref, q_ref, k_ref, v_ref, kb_ref, vb_ref, xr_ref, yg_ref, lf_ref,
                    xb_ref):
    j = pl.program_id(1)

    @pl.when(j == 0)
    def _():
        xb_ref[...] = x_ref[...].astype(BF16)
        zf = jnp.dot(xb_ref[...], wf_ref[...], preferred_element_type=F32)
        lf_ref[...] = _log_sigmoid(zf[:, :FOX_HEADS] + bf_ref[...])

    z = jnp.dot(xb_ref[...], w_ref[...], preferred_element_type=F32)

    @pl.when(j == 0)
    def _():
        q_ref[...] = (z * Q_SCALE).astype(BF16)

    @pl.when(j == 1)
    def _():
        k_ref[...] = z
        kb_ref[...] = z.astype(BF16)

    @pl.when(j == 2)
    def _():
        v_ref[...] = z
        vb_ref[...] = z.astype(BF16)

    @pl.when(j == 3)
    def _():
        xr_ref[...] = z

    @pl.when(j == 4)
    def _():
        yg_ref[...] = z


def _in_proj(x2d, w_main, w_f, b_f, tm):
    n, d = x2d.shape
    wcol = FOX_WIDTH
    row = lambda i, j: (i, 0)
    dts = (BF16, F32, F32, BF16, BF16, F32, F32)
    outs = [jax.ShapeDtypeStruct((n, wcol), dt) for dt in dts]
    outs.append(jax.ShapeDtypeStruct((n, FOX_HEADS), F32))
    return pl.pallas_call(
        _in_proj_kernel,
        out_shape=outs,
        grid=(n // tm, 5),
        in_specs=[
            pl.BlockSpec((tm, d), row),
            pl.BlockSpec((d, wcol), lambda i, j: (0, j)),
            pl.BlockSpec((d, 128), lambda i, j: (0, 0)),
            pl.BlockSpec((1, FOX_HEADS), lambda i, j: (0, 0)),
        ],
        out_specs=[pl.BlockSpec((tm, wcol), row)] * len(dts) + [pl.BlockSpec((tm, FOX_HEADS), row)],
        scratch_shapes=[pltpu.VMEM((tm, d), BF16)],
        compiler_params=_cparams(("parallel", "arbitrary")),
        name="in_proj",
    )(x2d, w_main, w_f, b_f)


_NT = (((1,), (1,)), ((), ()))


def _attn_init(m_ref, l_ref, acc_ref):
    m_ref[...] = jnp.full(m_ref.shape, -jnp.inf, F32)
    l_ref[...] = jnp.zeros(l_ref.shape, F32)
    acc_ref[...] = jnp.zeros(acc_ref.shape, F32)


def _fox_prompt_kernel(qi_ref, ki_ref, q_ref, k_ref, v_ref, ck_ref, o_ref,
                       m_ref, l_ref, acc_ref, s0_ref, s1_ref, p0_ref, p1_ref, a0_ref, a1_ref, *, rs):
    pair = pl.program_id(1)
    qi = qi_ref[pair]
    ki = ki_ref[pair]
    tq = q_ref.shape[0]
    tk = k_ref.shape[0]
    lanes = FOX_HEAD_DIM

    @pl.when(ki == 0)
    def _():
        _attn_init(m_ref, l_ref, acc_ref)

    def head_lanes(h):
        return pl.ds(pl.multiple_of(h * lanes, lanes), lanes)

    def scores(h, s_ref):
        sl = head_lanes(h)
        s_ref[...] = lax.dot_general(q_ref[:, sl], k_ref[:, sl], _NT, preferred_element_type=F32)

    def weighted_values(h, p_ref, a_ref):
        sl = head_lanes(h)
        pv = jnp.dot(p_ref[...], v_ref[:, sl], preferred_element_type=F32)
        acc_ref[:, sl] = a_ref[...] * acc_ref[:, sl] + pv

    def softmax(h, s_ref, p_ref, a_ref, masked):
        ck = ck_ref[0, h]
        for r in range(tq // rs):
            r0 = r * rs
            rsl = slice(r0, r0 + rs)
            chunks = []
            for c in range(tk // lanes):
                if masked and c * lanes > r0 + rs - 1:
                    continue
                s = s_ref[rsl, c * lanes:(c + 1) * lanes] - ck[:, c * lanes:(c + 1) * lanes]
                if masked and (c + 1) * lanes - 1 > r0:
                    row = r0 + lax.broadcasted_iota(jnp.int32, (rs, lanes), 0)
                    col = c * lanes + lax.broadcasted_iota(jnp.int32, (rs, lanes), 1)
                    s = jnp.where(col <= row, s, -jnp.inf)
                chunks.append((c, s))
            mc = chunks[0][1]
            for _, s in chunks[1:]:
                mc = jnp.maximum(mc, s)
            m_prev = m_ref[h, rsl, :]
            m_new = jnp.maximum(m_prev, jnp.max(mc, axis=-1, keepdims=True))
            alpha = jnp.exp2(m_prev - m_new)
            psum = None
            for c, s in chunks:
                p = jnp.exp2(s - m_new)
                p_ref[rsl, c * lanes:(c + 1) * lanes] = p.astype(BF16)
                psum = p if psum is None else psum + p
            for c in range(chunks[-1][0] + 1, tk // lanes):
                p_ref[rsl, c * lanes:(c + 1) * lanes] = jnp.zeros((rs, lanes), BF16)
            l_ref[h, rsl, :] = alpha * l_ref[h, rsl, :] + psum
            m_ref[h, rsl, :] = m_new
            a_ref[rsl, :] = alpha

    def all_heads(masked):
        scores(0, s0_ref)
        p1_ref[...] = jnp.zeros(p1_ref.shape, BF16)
        a1_ref[...] = jnp.ones(a1_ref.shape, F32)

        def two_heads(i, carry):
            h0 = 2 * i
            h1 = h0 + 1
            scores(h1, s1_ref)
            softmax(h0, s0_ref, p0_ref, a0_ref, masked)
            weighted_values(lax.rem(h0 + FOX_HEADS - 1, FOX_HEADS), p1_ref, a1_ref)
            scores(lax.rem(h0 + 2, FOX_HEADS), s0_ref)
            softmax(h1, s1_ref, p1_ref, a1_ref, masked)
            weighted_values(h0, p0_ref, a0_ref)
            return carry

        lax.fori_loop(0, FOX_HEADS // 2, two_heads, 0)
        weighted_values(FOX_HEADS - 1, p1_ref, a1_ref)

    @pl.when(ki < qi)
    def _():
        all_heads(False)

    @pl.when(ki == qi)
    def _():
        all_heads(True)
        for h in range(FOX_HEADS):
            sl = slice(h * lanes, (h + 1) * lanes)
            l_tot = jnp.sum(l_ref[h], axis=-1, keepdims=True)
            o_ref[:, sl] = (acc_ref[:, sl] / l_tot).astype(o_ref.dtype)


def _fox_prompt(q, kb, vb, c2, batch, seq, tq):
    nq = seq // tq
    width = q.shape[1]
    ck = jnp.transpose(c2, (0, 2, 1))[:, :, None, :]
    pairs = [(i, j) for i in range(nq) for j in range(i + 1)]
    qi_tab = jnp.asarray([p[0] for p in pairs], jnp.int32)
    ki_tab = jnp.asarray([p[1] for p in pairs], jnp.int32)
    q_map = lambda b, p, qt, kt: (b * nq + qt[p], 0)
    k_map = lambda b, p, qt, kt: (b * nq + kt[p], 0)
    grid_spec = pltpu.PrefetchScalarGridSpec(
        num_scalar_prefetch=2,
        grid=(batch, len(pairs)),
        in_specs=[
            pl.BlockSpec((tq, width), q_map),
            pl.BlockSpec((tq, width), k_map),
            pl.BlockSpec((tq, width), k_map),
            pl.BlockSpec((1, FOX_HEADS, 1, tq), lambda b, p, qt, kt: (b, 0, 0, kt[p])),
        ],
        out_specs=pl.BlockSpec((tq, width), q_map),
        scratch_shapes=[
            pltpu.VMEM((FOX_HEADS, tq, FOX_HEAD_DIM), F32),
            pltpu.VMEM((FOX_HEADS, tq, FOX_HEAD_DIM), F32),
            pltpu.VMEM((tq, width), F32),
            pltpu.VMEM((tq, tq), F32),
            pltpu.VMEM((tq, tq), F32),
            pltpu.VMEM((tq, tq), BF16),
            pltpu.VMEM((tq, tq), BF16),
            pltpu.VMEM((tq, FOX_HEAD_DIM), F32),
            pltpu.VMEM((tq, FOX_HEAD_DIM), F32),
        ],
    )
    return pl.pallas_call(
        functools.partial(_fox_prompt_kernel, rs=min(64, tq)),
        out_shape=jax.ShapeDtypeStruct(q.shape, BF16),
        grid_spec=grid_spec,
        compiler_params=_cparams(("parallel", "arbitrary")),
        name="fox_prompt",
    )(qi_tab, ki_tab, q, kb, vb, ck)


def _fox_sample_kernel(q_ref, kc_ref, vc_ref, kn_ref, vn_ref, ckc_ref, ckn_ref, o_ref, m_ref, l_ref, acc_ref):
    j = pl.program_id(1)
    tq = q_ref.shape[0]
    tk = kc_ref.shape[1] // FOX_HEADS
    lanes = FOX_HEAD_DIM

    @pl.when(j == 0)
    def _():
        _attn_init(m_ref, l_ref, acc_ref)

    def update(h, k, v, ck, masked):
        sl = slice(h * lanes, (h + 1) * lanes)
        s = lax.dot_general(q_ref[:, sl], k, _NT, preferred_element_type=F32) - ck
        if masked:
            row = lax.broadcasted_iota(jnp.int32, s.shape, 0)
            col = lax.broadcasted_iota(jnp.int32, s.shape, 1)
            s = jnp.where(col <= row, s, -jnp.inf)
        m_prev = m_ref[h]
        m_new = jnp.maximum(m_prev, jnp.max(s, axis=-1, keepdims=True))
        alpha = jnp.exp2(m_prev - m_new)
        p = jnp.exp2(s - m_new[:, 0:1])
        l_ref[h] = alpha * l_ref[h] + jnp.sum(p, axis=-1, keepdims=True)
        acc_ref[:, sl] = alpha * acc_ref[:, sl] + jnp.dot(p.astype(BF16), v, preferred_element_type=F32)
        m_ref[h] = m_new

    for h in range(FOX_HEADS):
        k = kc_ref[0, pl.ds(h, tk, stride=FOX_HEADS), :].astype(BF16)
        v = vc_ref[0, pl.ds(h, tk, stride=FOX_HEADS), :].astype(BF16)
        update(h, k, v, ckc_ref[0, h], False)

    @pl.when(j == pl.num_programs(1) - 1)
    def _():
        for h in range(FOX_HEADS):
            sl = slice(h * lanes, (h + 1) * lanes)
            update(h, kn_ref[:, sl], vn_ref[:, sl], ckn_ref[0, h], True)
            o_ref[:, sl] = (acc_ref[:, sl] / l_ref[h]).astype(o_ref.dtype)


def _fox_sample(q, kb, vb, cache_k, cache_v, c2_all, batch, seq, tk):
    past = cache_k.shape[1] // FOX_HEADS
    width = q.shape[1]
    ct = jnp.transpose(c2_all, (0, 2, 1))
    ckc = ct[:, :, None, :past]
    ckn = ct[:, :, None, past:]
    new_map = lambda b, j: (b, 0)
    cache_spec = pl.BlockSpec((1, tk * FOX_HEADS, FOX_HEAD_DIM), lambda b, j: (b, j, 0))
    return pl.pallas_call(
        _fox_sample_kernel,
        out_shape=jax.ShapeDtypeStruct(q.shape, BF16),
        grid=(batch, past // tk),
        in_specs=[
            pl.BlockSpec((seq, width), new_map),
            cache_spec,
            cache_spec,
            pl.BlockSpec((seq, width), new_map),
            pl.BlockSpec((seq, width), new_map),
            pl.BlockSpec((1, FOX_HEADS, 1, tk), lambda b, j: (b, 0, 0, j)),
            pl.BlockSpec((1, FOX_HEADS, 1, seq), lambda b, j: (b, 0, 0, 0)),
        ],
        out_specs=pl.BlockSpec((seq, width), new_map),
        scratch_shapes=[
            pltpu.VMEM((FOX_HEADS, seq, FOX_HEAD_DIM), F32),
            pltpu.VMEM((FOX_HEADS, seq, FOX_HEAD_DIM), F32),
            pltpu.VMEM((seq, width), F32),
        ],
        compiler_params=_cparams(("parallel", "arbitrary")),
        name="fox_sample",
    )(q, cache_k, cache_v, kb, vb, ckc, ckn)


_HALO = 8


def _lru_kernel(xr_ref, yg_ref, conv0_ref, h0_ref, wc_ref, bc_ref, wg_ref, ba_ref, bi_ref, lam_ref,
                o_ref, convo_ref, hlast_ref, xp_ref, a_ref, b_ref, h_ref):
    t = pl.program_id(1)
    tt, width = xr_ref.shape
    tail = CONV_WIDTH - 1
    lo = _HALO - tail

    @pl.when(t == 0)
    def _():
        xp_ref[lo:_HALO, :] = conv0_ref[0]
        h_ref[...] = h0_ref[0]

    xp_ref[_HALO:_HALO + tt, :] = xr_ref[...]
    xc = bc_ref[...] + xp_ref[lo:lo + tt, :] * wc_ref[0:1, :]
    for j in range(1, CONV_WIDTH):
        xc = xc + xp_ref[lo + j:lo + j + tt, :] * wc_ref[j:j + 1, :]
    new_tail = xp_ref[lo + tt:_HALO + tt, :]
    xp_ref[lo:_HALO, :] = new_tail

    xcb = xc.astype(BF16)
    gw = wg_ref.shape[1]
    lam = lam_ref[...]
    neg_sp = -(jnp.maximum(-lam, 0.0) + jnp.log1p(jnp.exp(-jnp.abs(lam))))
    for g in range(width // gw):
        sl = slice(g * gw, (g + 1) * gw)
        z = jnp.dot(xcb[:, sl], wg_ref[g], preferred_element_type=F32)
        r = jax.nn.sigmoid(z[:, :gw] + ba_ref[:, sl])
        i = jax.nn.sigmoid(z[:, gw:] + bi_ref[:, sl])
        log_a = LRU_C * r * neg_sp[:, sl]
        a = jnp.exp(log_a)
        a_ref[:, sl] = a
        b_ref[:, sl] = jnp.sqrt(-jnp.tanh(log_a) * (a * a + 1.0)) * i * xc[:, sl]

    def step(s, h):
        row = pl.ds(s, 1)
        h = a_ref[row, :] * h + b_ref[row, :]
        b_ref[row, :] = h
        return h

    h = lax.fori_loop(0, tt, step, h_ref[...], unroll=8)
    h_ref[...] = h
    o_ref[...] = (b_ref[...] * jax.nn.gelu(yg_ref[...])).astype(o_ref.dtype)

    @pl.when(t == pl.num_programs(1) - 1)
    def _():
        convo_ref[0] = new_tail
        hlast_ref[0] = h


def _lru(xr, yg, conv0, h0, w_conv, b_conv, w_gate, b_a, b_i, lam, batch, seq, tt):
    width = xr.shape[1]
    nt = seq // tt
    gw = w_gate.shape[1]
    rows = lambda b, t: (b * nt + t, 0)
    const2 = lambda b, t: (0, 0)
    per_b = lambda b, t: (b, 0, 0)
    tail = CONV_WIDTH - 1
    return pl.pallas_call(
        _lru_kernel,
        out_shape=[
            jax.ShapeDtypeStruct((batch * seq, width), BF16),
            jax.ShapeDtypeStruct((batch, tail, width), F32),
            jax.ShapeDtypeStruct((batch, 1, width), F32),
        ],
        grid=(batch, nt),
        in_specs=[
            pl.BlockSpec((tt, width), rows),
            pl.BlockSpec((tt, width), rows),
            pl.BlockSpec((1, tail, width), per_b),
            pl.BlockSpec((1, 1, width), per_b),
            pl.BlockSpec((CONV_WIDTH, width), const2),
            pl.BlockSpec((1, width), const2),
            pl.BlockSpec((width // gw, gw, 2 * gw), lambda b, t: (0, 0, 0)),
            pl.BlockSpec((1, width), const2),
            pl.BlockSpec((1, width), const2),
            pl.BlockSpec((1, width), const2),
        ],
        out_specs=[
            pl.BlockSpec((tt, width), rows),
            pl.BlockSpec((1, tail, width), per_b),
            pl.BlockSpec((1, 1, width), per_b),
        ],
        scratch_shapes=[
            pltpu.VMEM((_HALO + tt, width), F32),
            pltpu.VMEM((tt, width), F32),
            pltpu.VMEM((tt, width), F32),
            pltpu.VMEM((1, width), F32),
        ],
        compiler_params=_cparams(("parallel", "arbitrary")),
        name="lru",
    )(xr, yg, conv0, h0, w_conv, b_conv, w_gate, b_a, b_i, lam)


def _mix_norm_kernel(fox_ref, lru_ref, x_ref, wt_ref, wb_ref, g_ref, b_ref, wrh_ref, wrl_ref, br_ref,
                     x1_ref, x1b_ref, idx_ref, gate_ref):
    mix = jnp.dot(fox_ref[...], wt_ref[...], preferred_element_type=F32)
    mix = mix + jnp.dot(lru_ref[...], wb_ref[...], preferred_element_type=F32)
    x1 = _layer_norm(DEEPNORM_ALPHA * x_ref[...] + mix, g_ref[...], b_ref[...])
    x1_ref[...] = x1
    x1_hi = x1.astype(BF16)
    x1b_ref[...] = x1_hi
    x1_lo = (x1 - x1_hi.astype(F32)).astype(BF16)
    lg = jnp.dot(x1_hi, wrh_ref[...], preferred_element_type=F32)
    lg = lg + jnp.dot(x1_lo, wrh_ref[...], preferred_element_type=F32)
    lg = lg + jnp.dot(x1_hi, wrl_ref[...], preferred_element_type=F32)
    lg = lg + br_ref[...]
    lanes = lax.broadcasted_iota(jnp.int32, lg.shape, 1)
    vals = []
    for k in range(TOP_K):
        m = jnp.max(lg, axis=-1, keepdims=True)
        ix = jnp.min(jnp.where(lg == m, lanes, N_EXPERTS), axis=-1, keepdims=True)
        idx_ref[:, k:k + 1] = ix
        vals.append(m)
        lg = jnp.where(lanes == ix, -jnp.inf, lg)
    es = [jnp.exp(v - vals[0]) for v in vals]
    denom = es[0] + es[1] + es[2] + es[3]
    for k in range(TOP_K):
        gate_ref[:, k:k + 1] = es[k] / denom


def _mix_norm(fox_o, lru_o, x2d, w_top, w_bot, ln_g, ln_b, wr_hi, wr_lo, b_router, tm):
    n, d = x2d.shape
    half = fox_o.shape[1]
    row = lambda i: (i, 0)
    const = lambda i: (0, 0)
    return pl.pallas_call(
        _mix_norm_kernel,
        out_shape=[
            jax.ShapeDtypeStruct((n, d), F32),
            jax.ShapeDtypeStruct((n, d), BF16),
            jax.ShapeDtypeStruct((n, TOP_K), jnp.int32),
            jax.ShapeDtypeStruct((n, TOP_K), F32),
        ],
        grid=(n // tm,),
        in_specs=[
            pl.BlockSpec((tm, half), row),
            pl.BlockSpec((tm, half), row),
            pl.BlockSpec((tm, d), row),
            pl.BlockSpec((half, d), const),
            pl.BlockSpec((half, d), const),
            pl.BlockSpec((1, d), const),
            pl.BlockSpec((1, d), const),
            pl.BlockSpec((d, N_EXPERTS), const),
            pl.BlockSpec((d, N_EXPERTS), const),
            pl.BlockSpec((1, N_EXPERTS), const),
        ],
        out_specs=[
            pl.BlockSpec((tm, d), row),
            pl.BlockSpec((tm, d), row),
            pl.BlockSpec((tm, TOP_K), row),
            pl.BlockSpec((tm, TOP_K), row),
        ],
        compiler_params=_cparams(("parallel",)),
        name="mix_norm",
    )(fox_o, lru_o, x2d, w_top, w_bot, ln_g, ln_b, wr_hi, wr_lo, b_router)


def _expert_kernel(be_ref, bv_ref, nu_ref, x_ref, wg_ref, wu_ref, bg_ref, bu_ref, wd_ref, bd_ref,
                   o_ref, act_ref, *, sub, nf):
    rb = pl.program_id(0)
    s = pl.program_id(1)
    valid = bv_ref[rb]
    tm = x_ref.shape[0]
    tf = wg_ref.shape[2]

    def up_phase(nrows):
        x = x_ref[0:nrows, :]
        hg = jnp.dot(x, wg_ref[0].astype(BF16), preferred_element_type=F32) + bg_ref[0]
        hu = jnp.dot(x, wu_ref[0].astype(BF16), preferred_element_type=F32) + bu_ref[0]
        gate = jnp.minimum(hg, SWIGLU_LIMIT)
        up = jnp.clip(hu, -SWIGLU_LIMIT, SWIGLU_LIMIT)
        act = (up + 1.0) * gate * jax.nn.sigmoid(SWIGLU_ALPHA * gate)
        act_ref[0:nrows, pl.ds(pl.multiple_of(s * tf, tf), tf)] = act.astype(BF16)

    def down_phase(nrows):
        y = jnp.dot(act_ref[0:nrows, :], wd_ref[0].astype(BF16), preferred_element_type=F32) + bd_ref[0]
        o_ref[0:nrows, :] = y.astype(o_ref.dtype)
        if nrows < tm:
            o_ref[nrows:tm, :] = jnp.zeros((tm - nrows, o_ref.shape[1]), o_ref.dtype)

    for nrows, cond in ((tm, valid > sub), (sub, jnp.logical_and(valid > 0, valid <= sub))):
        @pl.when(jnp.logical_and(cond, s < nf))
        def _(nrows=nrows):
            up_phase(nrows)

        @pl.when(jnp.logical_and(cond, s >= nf))
        def _(nrows=nrows):
            down_phase(nrows)


def _experts(x_pad, block_e, block_valid, n_used, w_gu, b_gu, w_down, b_down, tm, tf, tn, sub):
    p, d = x_pad.shape
    d_ff = w_down.shape[1]
    nf = d_ff // tf
    nn = d // tn
    nb = p // tm

    def rbc(rb, nu):
        return jnp.minimum(rb, nu[0] - 1)

    def fcol(rb, s, nu):
        return jnp.where(rb < nu[0], jnp.minimum(s, nf - 1), nf - 1)

    def ncol(rb, s, nu):
        return jnp.where(rb < nu[0], jnp.maximum(s - nf, 0), nn - 1)

    grid_spec = pltpu.PrefetchScalarGridSpec(
        num_scalar_prefetch=3,
        grid=(nb, nf + nn),
        in_specs=[
            pl.BlockSpec((tm, d), lambda rb, s, be, bv, nu: (rbc(rb, nu), 0)),
            pl.BlockSpec((1, d, tf), lambda rb, s, be, bv, nu: (be[rbc(rb, nu)], 0, fcol(rb, s, nu))),
            pl.BlockSpec((1, d, tf), lambda rb, s, be, bv, nu: (be[rbc(rb, nu)], 0, nf + fcol(rb, s, nu))),
            pl.BlockSpec((1, 1, tf), lambda rb, s, be, bv, nu: (be[rbc(rb, nu)], 0, fcol(rb, s, nu))),
            pl.BlockSpec((1, 1, tf), lambda rb, s, be, bv, nu: (be[rbc(rb, nu)], 0, nf + fcol(rb, s, nu))),
            pl.BlockSpec((1, d_ff, tn), lambda rb, s, be, bv, nu: (be[rbc(rb, nu)], 0, ncol(rb, s, nu))),
            pl.BlockSpec((1, 1, tn), lambda rb, s, be, bv, nu: (be[rbc(rb, nu)], 0, ncol(rb, s, nu))),
        ],
        out_specs=pl.BlockSpec((tm, tn), lambda rb, s, be, bv, nu: (rbc(rb, nu), ncol(rb, s, nu))),
        scratch_shapes=[pltpu.VMEM((tm, d_ff), BF16)],
    )
    return pl.pallas_call(
        functools.partial(_expert_kernel, sub=sub, nf=nf),
        out_shape=jax.ShapeDtypeStruct((p, d), BF16),
        grid_spec=grid_spec,
        compiler_params=_cparams(("arbitrary", "arbitrary")),
        name="experts",
    )(block_e, block_valid, n_used, x_pad, w_gu, w_gu, b_gu, b_gu, w_down, b_down)


def _combine_kernel(y_ref, gate_ref, x1_ref, g_ref, b_ref, o_ref):
    y = y_ref[0].astype(F32) * gate_ref[:, 0:1]
    for k in range(1, TOP_K):
        y = y + y_ref[k].astype(F32) * gate_ref[:, k:k + 1]
    o_ref[...] = _layer_norm(DEEPNORM_ALPHA * x1_ref[...] + y, g_ref[...], b_ref[...])


def _combine(y_rows, gates, x1, ln_g, ln_b, row_block0, tn):
    n, d = x1.shape
    const = lambda i: (0, 0)
    return pl.pallas_call(
        _combine_kernel,
        out_shape=jax.ShapeDtypeStruct((n, d), F32),
        grid=(n // tn,),
        in_specs=[
            pl.BlockSpec((TOP_K, tn, d), lambda i: (0, row_block0 + i, 0)),
            pl.BlockSpec((tn, TOP_K), lambda i: (i, 0)),
            pl.BlockSpec((tn, d), lambda i: (i, 0)),
            pl.BlockSpec((1, d), const),
            pl.BlockSpec((1, d), const),
        ],
        out_specs=pl.BlockSpec((tn, d), lambda i: (i, 0)),
        compiler_params=_cparams(("parallel",)),
        name="combine",
    )(y_rows, gates, x1, ln_g, ln_b)


def _route(idx, tm):
    n = idx.shape[0]
    nk = n * TOP_K
    onehot = (idx[:, :, None] == jnp.arange(N_EXPERTS, dtype=jnp.int32)[None, None, :]).sum(axis=1).astype(jnp.int32)
    csum = jnp.cumsum(onehot, axis=0)
    rank = jnp.take_along_axis(csum - onehot, idx, axis=1)
    sizes = csum[-1]
    nblk = (sizes + tm - 1) // tm
    bends = jnp.cumsum(nblk)
    bstart = bends - nblk
    dest = bstart[idx] * tm + rank
    nb = -(-nk // tm) + N_EXPERTS
    blk = jnp.arange(nb, dtype=jnp.int32)
    block_e = jnp.minimum(jnp.sum(bends[None, :] <= blk[:, None], axis=1), N_EXPERTS - 1).astype(jnp.int32)
    n_used = bends[-1].astype(jnp.int32)
    valid = jnp.clip(sizes[block_e] - (blk - bstart[block_e]) * tm, 0, tm)
    valid = jnp.where(blk < n_used, valid, 0).astype(jnp.int32)
    tok = jnp.broadcast_to(jnp.arange(n, dtype=jnp.int32)[:, None], (n, TOP_K))
    row_tok = (jnp.arange(nb * tm, dtype=jnp.int32) % n).at[dest.reshape(-1)].set(tok.reshape(-1))
    return dest, row_tok, block_e, valid, n_used.reshape(1)


def _pick(n, pref):
    t = min(n, pref)
    while n % t:
        t //= 2
    return t


def kernel(x_prompt, x_sample, cache_k, cache_v, cache_logf, state_conv, state_lru, w_in, b_f, w_conv, b_conv, w_a, b_a, w_i, b_i, lam, w_out, ln1_g, ln1_b, w_router, b_router, w_gu, b_gu, w_down, b_down, ln2_g, ln2_b):
    assert w_in.shape[0] == DEPTH
    bp, tp, d = x_prompt.shape
    bs, ts, _ = x_sample.shape
    past = cache_k.shape[2]
    lru_w = w_conv.shape[-1]
    np_, ns = bp * tp, bs * ts

    win = w_in[0]
    f0 = 3 * FOX_WIDTH
    w_main = jnp.concatenate([win[:, :f0], win[:, f0 + FOX_HEADS:]], axis=1).astype(BF16)
    w_f = jnp.pad(win[:, f0:f0 + FOX_HEADS], ((0, 0), (0, 128 - FOX_HEADS))).astype(BF16)
    bf2 = b_f[0].reshape(1, FOX_HEADS)
    gpb = 4
    bd = w_a.shape[-1]
    eye = jnp.eye(gpb, dtype=F32)

    def blockdiag(w):
        wg = w.reshape(LRU_BLOCKS // gpb, gpb, bd, bd)
        return jnp.einsum("gacd,ab->gacbd", wg, eye).reshape(LRU_BLOCKS // gpb, gpb * bd, gpb * bd)

    w_gate = jnp.concatenate([blockdiag(w_a[0]), blockdiag(w_i[0])], axis=-1).astype(BF16)
    w_top = w_out[0, :FOX_WIDTH].astype(BF16)
    w_bot = w_out[0, FOX_WIDTH:].astype(BF16)
    wr = w_router[0]
    wr_hi = wr.astype(BF16)
    wr_lo = (wr - wr_hi.astype(F32)).astype(BF16)
    row = lambda a: a.reshape(1, -1)

    def mixers(x, batch, seq, fox_fn, conv0, h0):
        n = batch * seq
        q, k, v, kb, vb, xr, yg, logf = _in_proj(x.reshape(n, d), w_main, w_f, bf2, _pick(n, 512))
        fox_o = fox_fn(q, kb, vb, logf.reshape(batch, seq, FOX_HEADS))
        lru_o, conv_new, h_last = _lru(xr, yg, conv0, h0, w_conv[0], row(b_conv[0]), w_gate,
                                       row(b_a[0]), row(b_i[0]), row(lam[0]), batch, seq, _pick(seq, 256))
        x1, x1b, idx, gates = _mix_norm(fox_o, lru_o, x.reshape(n, d), w_top, w_bot, row(ln1_g[0]), row(ln1_b[0]),
                                        wr_hi, wr_lo, row(b_router[0]), _pick(n, 512))
        state = (k.reshape(1, batch, seq, FOX_HEADS, FOX_HEAD_DIM), v.reshape(1, batch, seq, FOX_HEADS, FOX_HEAD_DIM),
                 logf.reshape(1, batch, seq, FOX_HEADS), conv_new[None], h_last.reshape(1, batch, lru_w))
        return x1, x1b, idx, gates, state

    def fox_p(q, kb, vb, logf):
        return _fox_prompt(q, kb, vb, jnp.cumsum(logf, axis=1) * LOG2E, bp, tp, _pick(tp, 512))

    def fox_s(q, kb, vb, logf):
        c_all = jnp.cumsum(jnp.concatenate([cache_logf[0], logf], axis=1), axis=1) * LOG2E
        cache_rows = lambda c: c[0].reshape(bs, past * FOX_HEADS, FOX_HEAD_DIM)
        return _fox_sample(q, kb, vb, cache_rows(cache_k), cache_rows(cache_v), c_all, bs, ts, _pick(past, 512))

    x1p, x1bp, idxp, gp, state_p = mixers(x_prompt, bp, tp, fox_p,
                                          jnp.zeros((bp, CONV_WIDTH - 1, lru_w), F32), jnp.zeros((bp, 1, lru_w), F32))
    x1s, x1bs, idxs, gs, state_s = mixers(x_sample, bs, ts, fox_s, state_conv[0], state_lru[0].reshape(bs, 1, lru_w))

    tm_e, tf_e, tn_e, sub_e = 1024, 512, 512, 512
    x1b = jnp.concatenate([x1bp, x1bs], axis=0)
    idx = jnp.concatenate([idxp, idxs], axis=0)
    dest, row_tok, block_e, block_valid, n_used = _route(idx, tm_e)
    x_pad = x1b[row_tok]
    y_pad = _experts(x_pad, block_e, block_valid, n_used, w_gu[0], b_gu[0].reshape(N_EXPERTS, 1, -1),
                     w_down[0], b_down[0].reshape(N_EXPERTS, 1, -1), tm_e, tf_e, tn_e, sub_e)
    y_rows = y_pad[dest.T]

    tn = _pick(ns, 256)
    yp = _combine(y_rows, gp, x1p, row(ln2_g[0]), row(ln2_b[0]), 0, tn)
    ys = _combine(y_rows, gs, x1s, row(ln2_g[0]), row(ln2_b[0]), np_ // tn, tn)
    return (yp.reshape(bp, tp, d), ys.reshape(bs, ts, d)) + state_p + state_s
```

```python
import functools

import jax
import jax.numpy as jnp
from jax import lax
from jax.experimental import pallas as pl
from jax.experimental.pallas import tpu as pltpu

F32 = jnp.float32
BF16 = jnp.bfloat16

FOX_HEADS = 8
FOX_HEAD_DIM = 128
FOX_WIDTH = FOX_HEADS * FOX_HEAD_DIM
LRU_BLOCKS = 16
CONV_WIDTH = 4
LRU_C = 8.0
N_EXPERTS = 32
TOP_K = 4
SWIGLU_LIMIT = 7.0
SWIGLU_ALPHA = 1.702
LN_EPS = 1e-5
DEPTH = 1
DEEPNORM_ALPHA = (2.0 * DEPTH) ** 0.25
LOG2E = 1.4426950408889634
Q_SCALE = FOX_HEAD_DIM ** -0.5 * LOG2E

VMEM_LIMIT = 56 * 1024 * 1024


def _cparams(sem):
    return pltpu.CompilerParams(dimension_semantics=sem, vmem_limit_bytes=VMEM_LIMIT)


def _log_sigmoid(x):
    return jnp.minimum(x, 0.0) - jnp.log1p(jnp.exp(-jnp.abs(x)))


def _layer_norm(z, g, b):
    mu = jnp.mean(z, axis=-1, keepdims=True)
    zc = z - mu
    var = jnp.mean(zc * zc, axis=-1, keepdims=True)
    return zc * lax.rsqrt(var + LN_EPS) * g + b


def _in_proj_kernel(x_ref, w_ref, wf_ref, bf_ref, q_ref, k_ref, v_ref, kb_ref, vb_ref, xr_ref, yg_ref, lf_ref,
                    xb_ref):
    j = pl.program_id(1)

    @pl.when(j == 0)
    def _():
        xb_ref[...] = x_ref[...].astype(BF16)
        zf = jnp.dot(xb_ref[...], wf_ref[...], preferred_element_type=F32)
        lf_ref[...] = _log_sigmoid(zf[:, :FOX_HEADS] + bf_ref[...])

    z = jnp.dot(xb_ref[...], w_ref[...], preferred_element_type=F32)

    @pl.when(j == 0)
    def _():
        q_ref[...] = (z * Q_SCALE).astype(BF16)

    @pl.when(j == 1)
    def _():
        k_ref[...] = z
        kb_ref[...] = z.astype(BF16)

    @pl.when(j == 2)
    def _():
        v_ref[...] = z
        vb_ref[...] = z.astype(BF16)

    @pl.when(j == 3)
    def _():
        xr_ref[...] = z

    @pl.when(j == 4)
    def _():
        yg_ref[...] = z


def _in_proj(x2d, w_main, w_f, b_f, tm):
    n, d = x2d.shape
    wcol = FOX_WIDTH
    row = lambda i, j: (i, 0)
    dts = (BF16, F32, F32, BF16, BF16, F32, F32)
    outs = [jax.ShapeDtypeStruct((n, wcol), dt) for dt in dts]
    outs.append(jax.ShapeDtypeStruct((n, FOX_HEADS), F32))
    return pl.pallas_call(
        _in_proj_kernel,
        out_shape=outs,
        grid=(n // tm, 5),
        in_specs=[
            pl.BlockSpec((tm, d), row),
            pl.BlockSpec((d, wcol), lambda i, j: (0, j)),
            pl.BlockSpec((d, 128), lambda i, j: (0, 0)),
            pl.BlockSpec((1, FOX_HEADS), lambda i, j: (0, 0)),
        ],
        out_specs=[pl.BlockSpec((tm, wcol), row)] * len(dts) + [pl.BlockSpec((tm, FOX_HEADS), row)],
        scratch_shapes=[pltpu.VMEM((tm, d), BF16)],
        compiler_params=_cparams(("parallel", "arbitrary")),
        name="in_proj",
    )(x2d, w_main, w_f, b_f)


_NT = (((1,), (1,)), ((), ()))


def _attn_init(m_ref, l_ref, acc_ref):
    m_ref[...] = jnp.full(m_ref.shape, -jnp.inf, F32)
    l_ref[...] = jnp.zeros(l_ref.shape, F32)
    acc_ref[...] = jnp.zeros(acc_ref.shape, F32)


def _fox_prompt_kernel(qi_ref, ki_ref, q_ref, k_ref, v_ref, ck_ref, o_ref,
                       m_ref, l_ref, acc_ref, s0_ref, s1_ref, p0_ref, p1_ref, a0_ref, a1_ref, *, rs):
    pair = pl.program_id(1)
    qi = qi_ref[pair]
    ki = ki_ref[pair]
    tq = q_ref.shape[0]
    tk = k_ref.shape[0]
    lanes = FOX_HEAD_DIM

    @pl.when(ki == 0)
    def _():
        _attn_init(m_ref, l_ref, acc_ref)

    def head_lanes(h):
        return pl.ds(pl.multiple_of(h * lanes, lanes), lanes)

    def scores(h, s_ref):
        sl = head_lanes(h)
        s_ref[...] = lax.dot_general(q_ref[:, sl], k_ref[:, sl], _NT, preferred_element_type=F32)

    def weighted_values(h, p_ref, a_ref):
        sl = head_lanes(h)
        pv = jnp.dot(p_ref[...], v_ref[:, sl], preferred_element_type=F32)
        acc_ref[:, sl] = a_ref[...] * acc_ref[:, sl] + pv

    def softmax(h, s_ref, p_ref, a_ref, masked):
        ck = ck_ref[0, h]
        for r in range(tq // rs):
            r0 = r * rs
            rsl = slice(r0, r0 + rs)
            chunks = []
            for c in range(tk // lanes):
                if masked and c * lanes > r0 + rs - 1:
                    continue
                s = s_ref[rsl, c * lanes:(c + 1) * lanes] - ck[:, c * lanes:(c + 1) * lanes]
                if masked and (c + 1) * lanes - 1 > r0:
                    row = r0 + lax.broadcasted_iota(jnp.int32, (rs, lanes), 0)
                    col = c * lanes + lax.broadcasted_iota(jnp.int32, (rs, lanes), 1)
                    s = jnp.where(col <= row, s, -jnp.inf)
                chunks.append((c, s))
            mc = chunks[0][1]
            for _, s in chunks[1:]:
                mc = jnp.maximum(mc, s)
            m_prev = m_ref[h, rsl, :]
            m_new = jnp.maximum(m_prev, jnp.max(mc, axis=-1, keepdims=True))
            alpha = jnp.exp2(m_prev - m_new)
            psum = None
            for c, s in chunks:
                p = jnp.exp2(s - m_new)
                p_ref[rsl, c * lanes:(c + 1) * lanes] = p.astype(BF16)
                psum = p if psum is None else psum + p
            for c in range(chunks[-1][0] + 1, tk // lanes):
                p_ref[rsl, c * lanes:(c + 1) * lanes] = jnp.zeros((rs, lanes), BF16)
            l_ref[h, rsl, :] = alpha * l_ref[h, rsl, :] + psum
            m_ref[h, rsl, :] = m_new
            a_ref[rsl, :] = alpha

    def all_heads(masked):
        scores(0, s0_ref)
        p1_ref[...] = jnp.zeros(p1_ref.shape, BF16)
        a1_ref[...] = jnp.ones(a1_ref.shape, F32)

        def two_heads(i, carry):
            h0 = 2 * i
            h1 = h0 + 1
            scores(h1, s1_ref)
            softmax(h0, s0_ref, p0_ref, a0_ref, masked)
            weighted_values(lax.rem(h0 + FOX_HEADS - 1, FOX_HEADS), p1_ref, a1_ref)
            scores(lax.rem(h0 + 2, FOX_HEADS), s0_ref)
            softmax(h1, s1_ref, p1_ref, a1_ref, masked)
            weighted_values(h0, p0_ref, a0_ref)
            return carry

        lax.fori_loop(0, FOX_HEADS // 2, two_heads, 0)
        weighted_values(FOX_HEADS - 1, p1_ref, a1_ref)

    @pl.when(ki < qi)
    def _():
        all_heads(False)

    @pl.when(ki == qi)
    def _():
        all_heads(True)
        for h in range(FOX_HEADS):
            sl = slice(h * lanes, (h + 1) * lanes)
            l_tot = jnp.sum(l_ref[h], axis=-1, keepdims=True)
            o_ref[:, sl] = (acc_ref[:, sl] / l_tot).astype(o_ref.dtype)


def _fox_prompt(q, kb, vb, c2, batch, seq, tq):
    nq = seq // tq
    width = q.shape[1]
    ck = jnp.transpose(c2, (0, 2, 1))[:, :, None, :]
    pairs = [(i, j) for i in range(nq) for j in range(i + 1)]
    qi_tab = jnp.asarray([p[0] for p in pairs], jnp.int32)
    ki_tab = jnp.asarray([p[1] for p in pairs], jnp.int32)
    q_map = lambda b, p, qt, kt: (b * nq + qt[p], 0)
    k_map = lambda b, p, qt, kt: (b * nq + kt[p], 0)
    grid_spec = pltpu.PrefetchScalarGridSpec(
        num_scalar_prefetch=2,
        grid=(batch, len(pairs)),
        in_specs=[
            pl.BlockSpec((tq, width), q_map),
            pl.BlockSpec((tq, width), k_map),
            pl.BlockSpec((tq, width), k_map),
            pl.BlockSpec((1, FOX_HEADS, 1, tq), lambda b, p, qt, kt: (b, 0, 0, kt[p])),
        ],
        out_specs=pl.BlockSpec((tq, width), q_map),
        scratch_shapes=[
            pltpu.VMEM((FOX_HEADS, tq, FOX_HEAD_DIM), F32),
            pltpu.VMEM((FOX_HEADS, tq, FOX_HEAD_DIM), F32),
            pltpu.VMEM((tq, width), F32),
            pltpu.VMEM((tq, tq), F32),
            pltpu.VMEM((tq, tq), F32),
            pltpu.VMEM((tq, tq), BF16),
            pltpu.VMEM((tq, tq), BF16),
            pltpu.VMEM((tq, FOX_HEAD_DIM), F32),
            pltpu.VMEM((tq, FOX_HEAD_DIM), F32),
        ],
    )
    return pl.pallas_call(
        functools.partial(_fox_prompt_kernel, rs=min(64, tq)),
        out_shape=jax.ShapeDtypeStruct(q.shape, BF16),
        grid_spec=grid_spec,
        compiler_params=_cparams(("parallel", "arbitrary")),
        name="fox_prompt",
    )(qi_tab, ki_tab, q, kb, vb, ck)


def _fox_sample_kernel(q_ref, kc_ref, vc_ref, kn_ref, vn_ref, ckc_ref, ckn_ref, o_ref, m_ref, l_ref, acc_ref):
    j = pl.program_id(1)
    tq = q_ref.shape[0]
    tk = kc_ref.shape[1] // FOX_HEADS
    lanes = FOX_HEAD_DIM

    @pl.when(j == 0)
    def _():
        _attn_init(m_ref, l_ref, acc_ref)

    def update(h, k, v, ck, masked):
        sl = slice(h * lanes, (h + 1) * lanes)
        s = lax.dot_general(q_ref[:, sl], k, _NT, preferred_element_type=F32) - ck
        if masked:
            row = lax.broadcasted_iota(jnp.int32, s.shape, 0)
            col = lax.broadcasted_iota(jnp.int32, s.shape, 1)
            s = jnp.where(col <= row, s, -jnp.inf)
        m_prev = m_ref[h]
        m_new = jnp.maximum(m_prev, jnp.max(s, axis=-1, keepdims=True))
        alpha = jnp.exp2(m_prev - m_new)
        p = jnp.exp2(s - m_new[:, 0:1])
        l_ref[h] = alpha * l_ref[h] + jnp.sum(p, axis=-1, keepdims=True)
        acc_ref[:, sl] = alpha * acc_ref[:, sl] + jnp.dot(p.astype(BF16), v, preferred_element_type=F32)
        m_ref[h] = m_new

    for h in range(FOX_HEADS):
        k = kc_ref[0, pl.ds(h, tk, stride=FOX_HEADS), :].astype(BF16)
        v = vc_ref[0, pl.ds(h, tk, stride=FOX_HEADS), :].astype(BF16)
        update(h, k, v, ckc_ref[0, h], False)

    @pl.when(j == pl.num_programs(1) - 1)
    def _():
        for h in range(FOX_HEADS):
            sl = slice(h * lanes, (h + 1) * lanes)
            update(h, kn_ref[:, sl], vn_ref[:, sl], ckn_ref[0, h], True)
            o_ref[:, sl] = (acc_ref[:, sl] / l_ref[h]).astype(o_ref.dtype)


def _fox_sample(q, kb, vb, cache_k, cache_v, c2_all, batch, seq, tk):
    past = cache_k.shape[1] // FOX_HEADS
    width = q.shape[1]
    ct = jnp.transpose(c2_all, (0, 2, 1))
    ckc = ct[:, :, None, :past]
    ckn = ct[:, :, None, past:]
    new_map = lambda b, j: (b, 0)
    cache_spec = pl.BlockSpec((1, tk * FOX_HEADS, FOX_HEAD_DIM), lambda b, j: (b, j, 0))
    return pl.pallas_call(
        _fox_sample_kernel,
        out_shape=jax.ShapeDtypeStruct(q.shape, BF16),
        grid=(batch, past // tk),
        in_specs=[
            pl.BlockSpec((seq, width), new_map),
            cache_spec,
            cache_spec,
            pl.BlockSpec((seq, width), new_map),
            pl.BlockSpec((seq, width), new_map),
            pl.BlockSpec((1, FOX_HEADS, 1, tk), lambda b, j: (b, 0, 0, j)),
            pl.BlockSpec((1, FOX_HEADS, 1, seq), lambda b, j: (b, 0, 0, 0)),
        ],
        out_specs=pl.BlockSpec((seq, width), new_map),
        scratch_shapes=[
            pltpu.VMEM((FOX_HEADS, seq, FOX_HEAD_DIM), F32),
            pltpu.VMEM((FOX_HEADS, seq, FOX_HEAD_DIM), F32),
            pltpu.VMEM((seq, width), F32),
        ],
        compiler_params=_cparams(("parallel", "arbitrary")),
        name="fox_sample",
    )(q, cache_k, cache_v, kb, vb, ckc, ckn)


_HALO = 8


def _lru_kernel(xr_ref, yg_ref, conv0_ref, h0_ref, wc_ref, bc_ref, wg_ref, ba_ref, bi_ref, lam_ref,
                o_ref, convo_ref, hlast_ref, xp_ref, a_ref, b_ref, h_ref):
    t = pl.program_id(1)
    tt, width = xr_ref.shape
    tail = CONV_WIDTH - 1
    lo = _HALO - tail

    @pl.when(t == 0)
    def _():
        xp_ref[lo:_HALO, :] = conv0_ref[0]
        h_ref[...] = h0_ref[0]

    xp_ref[_HALO:_HALO + tt, :] = xr_ref[...]
    xc = bc_ref[...] + xp_ref[lo:lo + tt, :] * wc_ref[0:1, :]
    for j in range(1, CONV_WIDTH):
        xc = xc + xp_ref[lo + j:lo + j + tt, :] * wc_ref[j:j + 1, :]
    new_tail = xp_ref[lo + tt:_HALO + tt, :]
    xp_ref[lo:_HALO, :] = new_tail

    xcb = xc.astype(BF16)
    gw = wg_ref.shape[1]
    lam = lam_ref[...]
    neg_sp = -(jnp.maximum(-lam, 0.0) + jnp.log1p(jnp.exp(-jnp.abs(lam))))
    for g in range(width // gw):
        sl = slice(g * gw, (g + 1) * gw)
        z = jnp.dot(xcb[:, sl], wg_ref[g], preferred_element_type=F32)
        r = jax.nn.sigmoid(z[:, :gw] + ba_ref[:, sl])
        i = jax.nn.sigmoid(z[:, gw:] + bi_ref[:, sl])
        log_a = LRU_C * r * neg_sp[:, sl]
        a = jnp.exp(log_a)
        a_ref[:, sl] = a
        b_ref[:, sl] = jnp.sqrt(-jnp.tanh(log_a) * (a * a + 1.0)) * i * xc[:, sl]

    def step(s, h):
        row = pl.ds(s, 1)
        h = a_ref[row, :] * h + b_ref[row, :]
        b_ref[row, :] = h
        return h

    h = lax.fori_loop(0, tt, step, h_ref[...], unroll=8)
    h_ref[...] = h
    o_ref[...] = (b_ref[...] * jax.nn.gelu(yg_ref[...])).astype(o_ref.dtype)

    @pl.when(t == pl.num_programs(1) - 1)
    def _():
        convo_ref[0] = new_tail
        hlast_ref[0] = h


def _lru(xr, yg, conv0, h0, w_conv, b_conv, w_gate, b_a, b_i, lam, batch, seq, tt):
    width = xr.shape[1]
    nt = seq // tt
    gw = w_gate.shape[1]
    rows = lambda b, t: (b * nt + t, 0)
    const2 = lambda b, t: (0, 0)
    per_b = lambda b, t: (b, 0, 0)
    tail = CONV_WIDTH - 1
    return pl.pallas_call(
        _lru_kernel,
        out_shape=[
            jax.ShapeDtypeStruct((batch * seq, width), BF16),
            jax.ShapeDtypeStruct((batch, tail, width), F32),
            jax.ShapeDtypeStruct((batch, 1, width), F32),
        ],
        grid=(batch, nt),
        in_specs=[
            pl.BlockSpec((tt, width), rows),
            pl.BlockSpec((tt, width), rows),
            pl.BlockSpec((1, tail, width), per_b),
            pl.BlockSpec((1, 1, width), per_b),
            pl.BlockSpec((CONV_WIDTH, width), const2),
            pl.BlockSpec((1, width), const2),
            pl.BlockSpec((width // gw, gw, 2 * gw), lambda b, t: (0, 0, 0)),
            pl.BlockSpec((1, width), const2),
            pl.BlockSpec((1, width), const2),
            pl.BlockSpec((1, width), const2),
        ],
        out_specs=[
            pl.BlockSpec((tt, width), rows),
            pl.BlockSpec((1, tail, width), per_b),
            pl.BlockSpec((1, 1, width), per_b),
        ],
        scratch_shapes=[
            pltpu.VMEM((_HALO + tt, width), F32),
            pltpu.VMEM((tt, width), F32),
            pltpu.VMEM((tt, width), F32),
            pltpu.VMEM((1, width), F32),
        ],
        compiler_params=_cparams(("parallel", "arbitrary")),
        name="lru",
    )(xr, yg, conv0, h0, w_conv, b_conv, w_gate, b_a, b_i, lam)


def _mix_norm_kernel(*refs, aliased):
    (fox_ref, lru_ref, x_ref, wt_ref, wb_ref, g_ref, b_ref, wrc_ref, wrh_ref, br_ref, cnt0_ref) = refs[:11]
    x1_ref, x1b_ref, idx_ref, gate_ref, rank_ref, cnt_ref = refs[12:] if aliased else refs[11:]
    tm = x_ref.shape[0]

    mix = jnp.dot(fox_ref[...], wt_ref[...], preferred_element_type=F32)
    mix = mix + jnp.dot(lru_ref[...], wb_ref[...], preferred_element_type=F32)
    x1 = _layer_norm(DEEPNORM_ALPHA * x_ref[...] + mix, g_ref[...], b_ref[...])
    x1_ref[...] = x1
    x1_hi = x1.astype(BF16)
    x1b_ref[...] = x1_hi
    x1_lo = (x1 - x1_hi.astype(F32)).astype(BF16)
    lg2 = jnp.dot(x1_hi, wrc_ref[...], preferred_element_type=F32)
    lg = lg2[:, :N_EXPERTS] + lg2[:, N_EXPERTS:] + jnp.dot(x1_lo, wrh_ref[...], preferred_element_type=F32)
    lg = lg + br_ref[...]
    lanes = lax.broadcasted_iota(jnp.int32, lg.shape, 1)
    vals, picks = [], []
    for k in range(TOP_K):
        m = jnp.max(lg, axis=-1, keepdims=True)
        ix = jnp.min(jnp.where(lg == m, lanes, N_EXPERTS), axis=-1, keepdims=True)
        idx_ref[:, k:k + 1] = ix
        vals.append(m)
        picks.append(lanes == ix)
        lg = jnp.where(picks[-1], -jnp.inf, lg)
    es = [jnp.exp(v - vals[0]) for v in vals]
    denom = es[0] + es[1] + es[2] + es[3]
    for k in range(TOP_K):
        gate_ref[:, k:k + 1] = es[k] / denom

    @pl.when(pl.program_id(0) == 0)
    def _():
        cnt_ref[...] = cnt0_ref[...]

    sel = jnp.where(picks[0], 1.0, 0.0)
    for k in range(1, TOP_K):
        sel = sel + jnp.where(picks[k], 1.0, 0.0)
    earlier = lax.broadcasted_iota(jnp.int32, (tm, tm), 1) < lax.broadcasted_iota(jnp.int32, (tm, tm), 0)
    tri = jnp.where(earlier, 1.0, 0.0).astype(BF16)
    before = jnp.dot(tri, sel.astype(BF16), preferred_element_type=F32) + cnt_ref[...]
    for k in range(TOP_K):
        rank_ref[:, k:k + 1] = jnp.sum(jnp.where(picks[k], before, 0.0), axis=-1, keepdims=True).astype(jnp.int32)
    cnt_ref[...] += jnp.sum(sel, axis=0, keepdims=True)


def _mix_norm(fox_o, lru_o, x2d, w_top, w_bot, ln_g, ln_b, wr_cat, wr_hi, b_router, cnt0, tm, n_all, row_block0,
              x1b_buf=None):
    n, d = x2d.shape
    half = fox_o.shape[1]
    row = lambda i: (i, 0)
    const = lambda i: (0, 0)
    aliased = x1b_buf is not None
    in_specs = [
        pl.BlockSpec((tm, half), row),
        pl.BlockSpec((tm, half), row),
        pl.BlockSpec((tm, d), row),
        pl.BlockSpec((half, d), const),
        pl.BlockSpec((half, d), const),
        pl.BlockSpec((1, d), const),
        pl.BlockSpec((1, d), const),
        pl.BlockSpec((d, 2 * N_EXPERTS), const),
        pl.BlockSpec((d, N_EXPERTS), const),
        pl.BlockSpec((1, N_EXPERTS), const),
        pl.BlockSpec((1, N_EXPERTS), const),
    ]
    args = [fox_o, lru_o, x2d, w_top, w_bot, ln_g, ln_b, wr_cat, wr_hi, b_router, cnt0]
    if aliased:
        in_specs.append(pl.BlockSpec(memory_space=pl.ANY))
        args.append(x1b_buf)
    return pl.pallas_call(
        functools.partial(_mix_norm_kernel, aliased=aliased),
        out_shape=[
            jax.ShapeDtypeStruct((n, d), F32),
            jax.ShapeDtypeStruct((n_all, d), BF16),
            jax.ShapeDtypeStruct((n, TOP_K), jnp.int32),
            jax.ShapeDtypeStruct((n, TOP_K), F32),
            jax.ShapeDtypeStruct((n, TOP_K), jnp.int32),
            jax.ShapeDtypeStruct((1, N_EXPERTS), F32),
        ],
        grid=(n // tm,),
        in_specs=in_specs,
        out_specs=[
            pl.BlockSpec((tm, d), row),
            pl.BlockSpec((tm, d), lambda i: (row_block0 + i, 0)),
            pl.BlockSpec((tm, TOP_K), row),
            pl.BlockSpec((tm, TOP_K), row),
            pl.BlockSpec((tm, TOP_K), row),
            pl.BlockSpec((1, N_EXPERTS), const),
        ],
        input_output_aliases={11: 1} if aliased else {},
        compiler_params=_cparams(("arbitrary",)),
        name="mix_norm",
    )(*args)


def _expert_kernel(be_ref, bv_ref, nu_ref, x_ref, wg_ref, wu_ref, bg_ref, bu_ref, wd_ref, bd_ref,
                   o_ref, act_ref, *, sub, nf):
    rb = pl.program_id(0)
    s = pl.program_id(1)
    valid = bv_ref[rb]
    tm = x_ref.shape[0]
    tf = wg_ref.shape[2]

    def up_phase(nrows):
        x = x_ref[0:nrows, :]
        hg = jnp.dot(x, wg_ref[0].astype(BF16), preferred_element_type=F32) + bg_ref[0]
        hu = jnp.dot(x, wu_ref[0].astype(BF16), preferred_element_type=F32) + bu_ref[0]
        gate = jnp.minimum(hg, SWIGLU_LIMIT)
        up = jnp.clip(hu, -SWIGLU_LIMIT, SWIGLU_LIMIT)
        act = (up + 1.0) * gate * jax.nn.sigmoid(SWIGLU_ALPHA * gate)
        act_ref[0:nrows, pl.ds(pl.multiple_of(s * tf, tf), tf)] = act.astype(BF16)

    def down_phase(nrows):
        y = jnp.dot(act_ref[0:nrows, :], wd_ref[0].astype(BF16), preferred_element_type=F32) + bd_ref[0]
        o_ref[0:nrows, :] = y.astype(o_ref.dtype)
        if nrows < tm:
            o_ref[nrows:tm, :] = jnp.zeros((tm - nrows, o_ref.shape[1]), o_ref.dtype)

    for nrows in range(sub, tm + 1, sub):
        cond = jnp.logical_and(valid > nrows - sub, valid <= nrows)

        @pl.when(jnp.logical_and(cond, s < nf))
        def _(nrows=nrows):
            up_phase(nrows)

        @pl.when(jnp.logical_and(cond, s >= nf))
        def _(nrows=nrows):
            down_phase(nrows)


def _experts(x_pad, block_e, block_valid, n_used, w_gu, b_gu, w_down, b_down, tm, tf, tn, sub):
    p, d = x_pad.shape
    d_ff = w_down.shape[1]
    nf = d_ff // tf
    nn = d // tn
    nb = p // tm

    def rbc(rb, nu):
        return jnp.minimum(rb, nu[0] - 1)

    def fcol(rb, s, nu):
        return jnp.where(rb < nu[0], jnp.minimum(s, nf - 1), nf - 1)

    def ncol(rb, s, nu):
        return jnp.where(rb < nu[0], jnp.maximum(s - nf, 0), nn - 1)

    grid_spec = pltpu.PrefetchScalarGridSpec(
        num_scalar_prefetch=3,
        grid=(nb, nf + nn),
        in_specs=[
            pl.BlockSpec((tm, d), lambda rb, s, be, bv, nu: (rbc(rb, nu), 0)),
            pl.BlockSpec((1, d, tf), lambda rb, s, be, bv, nu: (be[rbc(rb, nu)], 0, fcol(rb, s, nu))),
            pl.BlockSpec((1, d, tf), lambda rb, s, be, bv, nu: (be[rbc(rb, nu)], 0, nf + fcol(rb, s, nu))),
            pl.BlockSpec((1, 1, tf), lambda rb, s, be, bv, nu: (be[rbc(rb, nu)], 0, fcol(rb, s, nu))),
            pl.BlockSpec((1, 1, tf), lambda rb, s, be, bv, nu: (be[rbc(rb, nu)], 0, nf + fcol(rb, s, nu))),
            pl.BlockSpec((1, d_ff, tn), lambda rb, s, be, bv, nu: (be[rbc(rb, nu)], 0, ncol(rb, s, nu))),
            pl.BlockSpec((1, 1, tn), lambda rb, s, be, bv, nu: (be[rbc(rb, nu)], 0, ncol(rb, s, nu))),
        ],
        out_specs=pl.BlockSpec((tm, tn), lambda rb, s, be, bv, nu: (rbc(rb, nu), ncol(rb, s, nu))),
        scratch_shapes=[pltpu.VMEM((tm, d_ff), BF16)],
    )
    return pl.pallas_call(
        functools.partial(_expert_kernel, sub=sub, nf=nf),
        out_shape=jax.ShapeDtypeStruct((p, d), BF16),
        grid_spec=grid_spec,
        compiler_params=_cparams(("arbitrary", "arbitrary")),
        name="experts",
    )(block_e, block_valid, n_used, x_pad, w_gu, w_gu, b_gu, b_gu, w_down, b_down)


def _combine_kernel(y_ref, gate_ref, x1_ref, g_ref, b_ref, o_ref):
    y = y_ref[0].astype(F32) * gate_ref[:, 0:1]
    for k in range(1, TOP_K):
        y = y + y_ref[k].astype(F32) * gate_ref[:, k:k + 1]
    o_ref[...] = _layer_norm(DEEPNORM_ALPHA * x1_ref[...] + y, g_ref[...], b_ref[...])


def _combine(y_rows, gates, x1, ln_g, ln_b, row_block0, tn):
    n, d = x1.shape
    const = lambda i: (0, 0)
    return pl.pallas_call(
        _combine_kernel,
        out_shape=jax.ShapeDtypeStruct((n, d), F32),
        grid=(n // tn,),
        in_specs=[
            pl.BlockSpec((TOP_K, tn, d), lambda i: (0, row_block0 + i, 0)),
            pl.BlockSpec((tn, TOP_K), lambda i: (i, 0)),
            pl.BlockSpec((tn, d), lambda i: (i, 0)),
            pl.BlockSpec((1, d), const),
            pl.BlockSpec((1, d), const),
        ],
        out_specs=pl.BlockSpec((tn, d), lambda i: (i, 0)),
        compiler_params=_cparams(("parallel",)),
        name="combine",
    )(y_rows, gates, x1, ln_g, ln_b)


def _route(idx, rank, sizes, tm):
    n = idx.shape[0]
    nk = n * TOP_K
    nblk = (sizes + tm - 1) // tm
    bends = jnp.cumsum(nblk)
    bstart = bends - nblk
    dest = bstart[idx] * tm + rank
    nb = -(-nk // tm) + N_EXPERTS
    blk = jnp.arange(nb, dtype=jnp.int32)
    block_e = jnp.minimum(jnp.sum(bends[None, :] <= blk[:, None], axis=1), N_EXPERTS - 1).astype(jnp.int32)
    n_used = bends[-1].astype(jnp.int32)
    valid = jnp.clip(sizes[block_e] - (blk - bstart[block_e]) * tm, 0, tm)
    valid = jnp.where(blk < n_used, valid, 0).astype(jnp.int32)
    tok = jnp.broadcast_to(jnp.arange(n, dtype=jnp.int32)[:, None], (n, TOP_K))
    row_tok = (jnp.arange(nb * tm, dtype=jnp.int32) % n).at[dest.reshape(-1)].set(tok.reshape(-1))
    return dest, row_tok, block_e, valid, n_used.reshape(1)


def _pick(n, pref):
    t = min(n, pref)
    while n % t:
        t //= 2
    return t


def kernel(x_prompt, x_sample, cache_k, cache_v, cache_logf, state_conv, state_lru, w_in, b_f, w_conv, b_conv, w_a, b_a, w_i, b_i, lam, w_out, ln1_g, ln1_b, w_router, b_router, w_gu, b_gu, w_down, b_down, ln2_g, ln2_b):
    assert w_in.shape[0] == DEPTH
    bp, tp, d = x_prompt.shape
    bs, ts, _ = x_sample.shape
    past = cache_k.shape[2]
    lru_w = w_conv.shape[-1]
    np_, ns = bp * tp, bs * ts

    win = w_in[0]
    f0 = 3 * FOX_WIDTH
    w_main = jnp.concatenate([win[:, :f0], win[:, f0 + FOX_HEADS:]], axis=1).astype(BF16)
    w_f = jnp.pad(win[:, f0:f0 + FOX_HEADS], ((0, 0), (0, 128 - FOX_HEADS))).astype(BF16)
    bf2 = b_f[0].reshape(1, FOX_HEADS)
    gpb = 4
    bd = w_a.shape[-1]
    eye = jnp.eye(gpb, dtype=F32)

    def blockdiag(w):
        wg = w.reshape(LRU_BLOCKS // gpb, gpb, bd, bd)
        return jnp.einsum("gacd,ab->gacbd", wg, eye).reshape(LRU_BLOCKS // gpb, gpb * bd, gpb * bd)

    w_gate = jnp.concatenate([blockdiag(w_a[0]), blockdiag(w_i[0])], axis=-1).astype(BF16)
    w_top = w_out[0, :FOX_WIDTH].astype(BF16)
    w_bot = w_out[0, FOX_WIDTH:].astype(BF16)
    wr = w_router[0]
    wr_hi = wr.astype(BF16)
    wr_cat = jnp.concatenate([wr_hi, (wr - wr_hi.astype(F32)).astype(BF16)], axis=1)
    row = lambda a: a.reshape(1, -1)

    n_all = np_ + ns
    tm_mix = _pick(ns, 512)

    def mixers(x, batch, seq, fox_fn, conv0, h0, cnt0, x1b_buf, row0):
        n = batch * seq
        q, k, v, kb, vb, xr, yg, logf = _in_proj(x.reshape(n, d), w_main, w_f, bf2, _pick(n, 512))
        fox_o = fox_fn(q, kb, vb, logf.reshape(batch, seq, FOX_HEADS))
        lru_o, conv_new, h_last = _lru(xr, yg, conv0, h0, w_conv[0], row(b_conv[0]), w_gate,
                                       row(b_a[0]), row(b_i[0]), row(lam[0]), batch, seq, _pick(seq, 256))
        routed = _mix_norm(fox_o, lru_o, x.reshape(n, d), w_top, w_bot, row(ln1_g[0]), row(ln1_b[0]),
                           wr_cat, wr_hi, row(b_router[0]), cnt0, tm_mix, n_all, row0 // tm_mix, x1b_buf)
        state = (k.reshape(1, batch, seq, FOX_HEADS, FOX_HEAD_DIM), v.reshape(1, batch, seq, FOX_HEADS, FOX_HEAD_DIM),
                 logf.reshape(1, batch, seq, FOX_HEADS), conv_new[None], h_last.reshape(1, batch, lru_w))
        return routed, state

    def fox_p(q, kb, vb, logf):
        return _fox_prompt(q, kb, vb, jnp.cumsum(logf, axis=1) * LOG2E, bp, tp, _pick(tp, 512))

    def fox_s(q, kb, vb, logf):
        c_all = jnp.cumsum(jnp.concatenate([cache_logf[0], logf], axis=1), axis=1) * LOG2E
        cache_rows = lambda c: c[0].reshape(bs, past * FOX_HEADS, FOX_HEAD_DIM)
        return _fox_sample(q, kb, vb, cache_rows(cache_k), cache_rows(cache_v), c_all, bs, ts, _pick(past, 512))

    (x1p, x1b, idxp, gp, rankp, cnt_p), state_p = mixers(
        x_prompt, bp, tp, fox_p, jnp.zeros((bp, CONV_WIDTH - 1, lru_w), F32), jnp.zeros((bp, 1, lru_w), F32),
        jnp.zeros((1, N_EXPERTS), F32), None, 0)
    (x1s, x1b, idxs, gs, ranks, cnt_all), state_s = mixers(
        x_sample, bs, ts, fox_s, state_conv[0], state_lru[0].reshape(bs, 1, lru_w), cnt_p, x1b, np_)

    tm_e, tf_e, tn_e, sub_e = 1024, 512, 512, 256
    idx = jnp.concatenate([idxp, idxs], axis=0)
    rank = jnp.concatenate([rankp, ranks], axis=0)
    dest, row_tok, block_e, block_valid, n_used = _route(idx, rank, cnt_all[0].astype(jnp.int32), tm_e)
    x_pad = x1b[row_tok]
    y_pad = _experts(x_pad, block_e, block_valid, n_used, w_gu[0], b_gu[0].reshape(N_EXPERTS, 1, -1),
                     w_down[0], b_down[0].reshape(N_EXPERTS, 1, -1), tm_e, tf_e, tn_e, sub_e)
    y_rows = y_pad[dest.T]

    tn = _pick(ns, 512)
    yp = _combine(y_rows, gp, x1p, row(ln2_g[0]), row(ln2_b[0]), 0, tn)
    ys = _combine(y_rows, gs, x1s, row(ln2_g[0]), row(ln2_b[0]), np_ // tn, tn)
    return (yp.reshape(bp, tp, d), ys.reshape(bs, ts, d)) + state_p + state_s
```

```python
import functools

import jax
import jax.numpy as jnp
from jax import lax
from jax.experimental import pallas as pl
from jax.experimental.pallas import tpu as pltpu

F32 = jnp.float32
BF16 = jnp.bfloat16

FOX_HEADS = 8
FOX_HEAD_DIM = 128
FOX_WIDTH = FOX_HEADS * FOX_HEAD_DIM
LRU_BLOCKS = 16
CONV_WIDTH = 4
LRU_C = 8.0
N_EXPERTS = 32
TOP_K = 4
SWIGLU_LIMIT = 7.0
SWIGLU_ALPHA = 1.702
LN_EPS = 1e-5
DEPTH = 1
DEEPNORM_ALPHA = (2.0 * DEPTH) ** 0.25
LOG2E = 1.4426950408889634
Q_SCALE = FOX_HEAD_DIM ** -0.5 * LOG2E

VMEM_LIMIT = 56 * 1024 * 1024


def _cparams(sem):
    return pltpu.CompilerParams(dimension_semantics=sem, vmem_limit_bytes=VMEM_LIMIT)


def _log_sigmoid(x):
    return jnp.minimum(x, 0.0) - jnp.log1p(jnp.exp(-jnp.abs(x)))


def _layer_norm(z, g, b):
    mu = jnp.mean(z, axis=-1, keepdims=True)
    zc = z - mu
    var = jnp.mean(zc * zc, axis=-1, keepdims=True)
    return zc * lax.rsqrt(var + LN_EPS) * g + b


def _in_proj_kernel(x_ref, w_ref, wf_ref, bf_ref, q_ref, k_ref, v_ref, kb_ref, vb_ref, xr_ref, yg_ref, lf_ref,
                    xb_ref):
    j = pl.program_id(1)

    @pl.when(j == 0)
    def _():
        xb_ref[...] = x_ref[...].astype(BF16)
        zf = jnp.dot(xb_ref[...], wf_ref[...], preferred_element_type=F32)
        lf_ref[...] = _log_sigmoid(zf[:, :FOX_HEADS] + bf_ref[...])

    z = jnp.dot(xb_ref[...], w_ref[...], preferred_element_type=F32)

    @pl.when(j == 0)
    def _():
        q_ref[...] = (z * Q_SCALE).astype(BF16)

    @pl.when(j == 1)
    def _():
        k_ref[...] = z
        kb_ref[...] = z.astype(BF16)

    @pl.when(j == 2)
    def _():
        v_ref[...] = z
        vb_ref[...] = z.astype(BF16)

    @pl.when(j == 3)
    def _():
        xr_ref[...] = z

    @pl.when(j == 4)
    def _():
        yg_ref[...] = z


def _in_proj(x2d, w_main, w_f, b_f, tm):
    n, d = x2d.shape
    wcol = FOX_WIDTH
    row = lambda i, j: (i, 0)
    dts = (BF16, F32, F32, BF16, BF16, F32, F32)
    outs = [jax.ShapeDtypeStruct((n, wcol), dt) for dt in dts]
    outs.append(jax.ShapeDtypeStruct((n, FOX_HEADS), F32))
    return pl.pallas_call(
        _in_proj_kernel,
        out_shape=outs,
        grid=(n // tm, 5),
        in_specs=[
            pl.BlockSpec((tm, d), row),
            pl.BlockSpec((d, wcol), lambda i, j: (0, j)),
            pl.BlockSpec((d, 128), lambda i, j: (0, 0)),
            pl.BlockSpec((1, FOX_HEADS), lambda i, j: (0, 0)),
        ],
        out_specs=[pl.BlockSpec((tm, wcol), row)] * len(dts) + [pl.BlockSpec((tm, FOX_HEADS), row)],
        scratch_shapes=[pltpu.VMEM((tm, d), BF16)],
        compiler_params=_cparams(("parallel", "arbitrary")),
        name="in_proj",
    )(x2d, w_main, w_f, b_f)


_NT = (((1,), (1,)), ((), ()))


def _attn_init(m_ref, l_ref, acc_ref):
    m_ref[...] = jnp.full(m_ref.shape, -jnp.inf, F32)
    l_ref[...] = jnp.zeros(l_ref.shape, F32)
    acc_ref[...] = jnp.zeros(acc_ref.shape, F32)


def _fox_prompt_kernel(qi_ref, ki_ref, q_ref, k_ref, v_ref, ck_ref, o_ref,
                       m_ref, l_ref, acc_ref, s0_ref, s1_ref, p0_ref, p1_ref, a0_ref, a1_ref, *, rs):
    pair = pl.program_id(1)
    qi = qi_ref[pair]
    ki = ki_ref[pair]
    tq = q_ref.shape[0]
    tk = k_ref.shape[0]
    lanes = FOX_HEAD_DIM

    @pl.when(ki == 0)
    def _():
        _attn_init(m_ref, l_ref, acc_ref)

    def head_lanes(h):
        return pl.ds(pl.multiple_of(h * lanes, lanes), lanes)

    def scores(h, s_ref):
        sl = head_lanes(h)
        s_ref[...] = lax.dot_general(q_ref[:, sl], k_ref[:, sl], _NT, preferred_element_type=F32)

    def weighted_values(h, p_ref, a_ref):
        sl = head_lanes(h)
        pv = jnp.dot(p_ref[...], v_ref[:, sl], preferred_element_type=F32)
        acc_ref[:, sl] = a_ref[...] * acc_ref[:, sl] + pv

    def softmax(h, s_ref, p_ref, a_ref, masked):
        ck = ck_ref[0, h]
        for r in range(tq // rs):
            r0 = r * rs
            rsl = slice(r0, r0 + rs)
            chunks = []
            for c in range(tk // lanes):
                if masked and c * lanes > r0 + rs - 1:
                    continue
                s = s_ref[rsl, c * lanes:(c + 1) * lanes] - ck[:, c * lanes:(c + 1) * lanes]
                if masked and (c + 1) * lanes - 1 > r0:
                    row = r0 + lax.broadcasted_iota(jnp.int32, (rs, lanes), 0)
                    col = c * lanes + lax.broadcasted_iota(jnp.int32, (rs, lanes), 1)
                    s = jnp.where(col <= row, s, -jnp.inf)
                chunks.append((c, s))
            mc = chunks[0][1]
            for _, s in chunks[1:]:
                mc = jnp.maximum(mc, s)
            m_prev = m_ref[h, rsl, :]
            m_new = jnp.maximum(m_prev, jnp.max(mc, axis=-1, keepdims=True))
            alpha = jnp.exp2(m_prev - m_new)
            psum = None
            for c, s in chunks:
                p = jnp.exp2(s - m_new)
                p_ref[rsl, c * lanes:(c + 1) * lanes] = p.astype(BF16)
                psum = p if psum is None else psum + p
            for c in range(chunks[-1][0] + 1, tk // lanes):
                p_ref[rsl, c * lanes:(c + 1) * lanes] = jnp.zeros((rs, lanes), BF16)
            l_ref[h, rsl, :] = alpha * l_ref[h, rsl, :] + psum
            m_ref[h, rsl, :] = m_new
            a_ref[rsl, :] = alpha

    def all_heads(masked):
        scores(0, s0_ref)
        p1_ref[...] = jnp.zeros(p1_ref.shape, BF16)
        a1_ref[...] = jnp.ones(a1_ref.shape, F32)

        def two_heads(i, carry):
            h0 = 2 * i
            h1 = h0 + 1
            scores(h1, s1_ref)
            softmax(h0, s0_ref, p0_ref, a0_ref, masked)
            weighted_values(lax.rem(h0 + FOX_HEADS - 1, FOX_HEADS), p1_ref, a1_ref)
            scores(lax.rem(h0 + 2, FOX_HEADS), s0_ref)
            softmax(h1, s1_ref, p1_ref, a1_ref, masked)
            weighted_values(h0, p0_ref, a0_ref)
            return carry

        lax.fori_loop(0, FOX_HEADS // 2, two_heads, 0)
        weighted_values(FOX_HEADS - 1, p1_ref, a1_ref)

    @pl.when(ki < qi)
    def _():
        all_heads(False)

    @pl.when(ki == qi)
    def _():
        all_heads(True)
        for h in range(FOX_HEADS):
            sl = slice(h * lanes, (h + 1) * lanes)
            l_tot = jnp.sum(l_ref[h], axis=-1, keepdims=True)
            o_ref[:, sl] = (acc_ref[:, sl] / l_tot).astype(o_ref.dtype)


def _fox_prompt(q, kb, vb, c2, batch, seq, tq):
    nq = seq // tq
    width = q.shape[1]
    ck = jnp.transpose(c2, (0, 2, 1))[:, :, None, :]
    pairs = [(i, j) for i in range(nq) for j in range(i + 1)]
    qi_tab = jnp.asarray([p[0] for p in pairs], jnp.int32)
    ki_tab = jnp.asarray([p[1] for p in pairs], jnp.int32)
    q_map = lambda b, p, qt, kt: (b * nq + qt[p], 0)
    k_map = lambda b, p, qt, kt: (b * nq + kt[p], 0)
    grid_spec = pltpu.PrefetchScalarGridSpec(
        num_scalar_prefetch=2,
        grid=(batch, len(pairs)),
        in_specs=[
            pl.BlockSpec((tq, width), q_map),
            pl.BlockSpec((tq, width), k_map),
            pl.BlockSpec((tq, width), k_map),
            pl.BlockSpec((1, FOX_HEADS, 1, tq), lambda b, p, qt, kt: (b, 0, 0, kt[p])),
        ],
        out_specs=pl.BlockSpec((tq, width), q_map),
        scratch_shapes=[
            pltpu.VMEM((FOX_HEADS, tq, FOX_HEAD_DIM), F32),
            pltpu.VMEM((FOX_HEADS, tq, FOX_HEAD_DIM), F32),
            pltpu.VMEM((tq, width), F32),
            pltpu.VMEM((tq, tq), F32),
            pltpu.VMEM((tq, tq), F32),
            pltpu.VMEM((tq, tq), BF16),
            pltpu.VMEM((tq, tq), BF16),
            pltpu.VMEM((tq, FOX_HEAD_DIM), F32),
            pltpu.VMEM((tq, FOX_HEAD_DIM), F32),
        ],
    )
    return pl.pallas_call(
        functools.partial(_fox_prompt_kernel, rs=min(64, tq)),
        out_shape=jax.ShapeDtypeStruct(q.shape, BF16),
        grid_spec=grid_spec,
        compiler_params=_cparams(("parallel", "arbitrary")),
        name="fox_prompt",
    )(qi_tab, ki_tab, q, kb, vb, ck)


def _fox_sample_kernel(q_ref, kc_ref, vc_ref, kn_ref, vn_ref, ckc_ref, ckn_ref, o_ref, m_ref, l_ref, acc_ref):
    j = pl.program_id(1)
    tq = q_ref.shape[0]
    tk = kc_ref.shape[1] // FOX_HEADS
    lanes = FOX_HEAD_DIM

    @pl.when(j == 0)
    def _():
        _attn_init(m_ref, l_ref, acc_ref)

    def update(h, k, v, ck, masked):
        sl = slice(h * lanes, (h + 1) * lanes)
        s = lax.dot_general(q_ref[:, sl], k, _NT, preferred_element_type=F32) - ck
        if masked:
            row = lax.broadcasted_iota(jnp.int32, s.shape, 0)
            col = lax.broadcasted_iota(jnp.int32, s.shape, 1)
            s = jnp.where(col <= row, s, -jnp.inf)
        m_prev = m_ref[h]
        m_new = jnp.maximum(m_prev, jnp.max(s, axis=-1, keepdims=True))
        alpha = jnp.exp2(m_prev - m_new)
        p = jnp.exp2(s - m_new[:, 0:1])
        l_ref[h] = alpha * l_ref[h] + jnp.sum(p, axis=-1, keepdims=True)
        acc_ref[:, sl] = alpha * acc_ref[:, sl] + jnp.dot(p.astype(BF16), v, preferred_element_type=F32)
        m_ref[h] = m_new

    for h in range(FOX_HEADS):
        k = kc_ref[0, pl.ds(h, tk, stride=FOX_HEADS), :].astype(BF16)
        v = vc_ref[0, pl.ds(h, tk, stride=FOX_HEADS), :].astype(BF16)
        update(h, k, v, ckc_ref[0, h], False)

    @pl.when(j == pl.num_programs(1) - 1)
    def _():
        for h in range(FOX_HEADS):
            sl = slice(h * lanes, (h + 1) * lanes)
            update(h, kn_ref[:, sl], vn_ref[:, sl], ckn_ref[0, h], True)
            o_ref[:, sl] = (acc_ref[:, sl] / l_ref[h]).astype(o_ref.dtype)


def _fox_sample(q, kb, vb, cache_k, cache_v, c2_all, batch, seq, tk):
    past = cache_k.shape[1] // FOX_HEADS
    width = q.shape[1]
    ct = jnp.transpose(c2_all, (0, 2, 1))
    ckc = ct[:, :, None, :past]
    ckn = ct[:, :, None, past:]
    new_map = lambda b, j: (b, 0)
    cache_spec = pl.BlockSpec((1, tk * FOX_HEADS, FOX_HEAD_DIM), lambda b, j: (b, j, 0))
    return pl.pallas_call(
        _fox_sample_kernel,
        out_shape=jax.ShapeDtypeStruct(q.shape, BF16),
        grid=(batch, past // tk),
        in_specs=[
            pl.BlockSpec((seq, width), new_map),
            cache_spec,
            cache_spec,
            pl.BlockSpec((seq, width), new_map),
            pl.BlockSpec((seq, width), new_map),
            pl.BlockSpec((1, FOX_HEADS, 1, tk), lambda b, j: (b, 0, 0, j)),
            pl.BlockSpec((1, FOX_HEADS, 1, seq), lambda b, j: (b, 0, 0, 0)),
        ],
        out_specs=pl.BlockSpec((seq, width), new_map),
        scratch_shapes=[
            pltpu.VMEM((FOX_HEADS, seq, FOX_HEAD_DIM), F32),
            pltpu.VMEM((FOX_HEADS, seq, FOX_HEAD_DIM), F32),
            pltpu.VMEM((seq, width), F32),
        ],
        compiler_params=_cparams(("parallel", "arbitrary")),
        name="fox_sample",
    )(q, cache_k, cache_v, kb, vb, ckc, ckn)


_HALO = 8


def _lru_kernel(xr_ref, yg_ref, conv0_ref, h0_ref, wc_ref, bc_ref, wg_ref, ba_ref, bi_ref, lam_ref,
                o_ref, convo_ref, hlast_ref, xp_ref, a_ref, b_ref, h_ref):
    t = pl.program_id(1)
    tt, width = xr_ref.shape
    tail = CONV_WIDTH - 1
    lo = _HALO - tail

    @pl.when(t == 0)
    def _():
        xp_ref[lo:_HALO, :] = conv0_ref[0]
        h_ref[...] = h0_ref[0]

    xp_ref[_HALO:_HALO + tt, :] = xr_ref[...]
    xc = bc_ref[...] + xp_ref[lo:lo + tt, :] * wc_ref[0:1, :]
    for j in range(1, CONV_WIDTH):
        xc = xc + xp_ref[lo + j:lo + j + tt, :] * wc_ref[j:j + 1, :]
    new_tail = xp_ref[lo + tt:_HALO + tt, :]
    xp_ref[lo:_HALO, :] = new_tail

    xcb = xc.astype(BF16)
    gw = wg_ref.shape[1]
    lam = lam_ref[...]
    neg_sp = -(jnp.maximum(-lam, 0.0) + jnp.log1p(jnp.exp(-jnp.abs(lam))))
    for g in range(width // gw):
        sl = slice(g * gw, (g + 1) * gw)
        z = jnp.dot(xcb[:, sl], wg_ref[g], preferred_element_type=F32)
        r = jax.nn.sigmoid(z[:, :gw] + ba_ref[:, sl])
        i = jax.nn.sigmoid(z[:, gw:] + bi_ref[:, sl])
        log_a = LRU_C * r * neg_sp[:, sl]
        a = jnp.exp(log_a)
        a_ref[:, sl] = a
        b_ref[:, sl] = jnp.sqrt(-jnp.tanh(log_a) * (a * a + 1.0)) * i * xc[:, sl]

    def step(s, h):
        row = pl.ds(s, 1)
        h = a_ref[row, :] * h + b_ref[row, :]
        b_ref[row, :] = h
        return h

    h = lax.fori_loop(0, tt, step, h_ref[...], unroll=8)
    h_ref[...] = h
    o_ref[...] = (b_ref[...] * jax.nn.gelu(yg_ref[...])).astype(o_ref.dtype)

    @pl.when(t == pl.num_programs(1) - 1)
    def _():
        convo_ref[0] = new_tail
        hlast_ref[0] = h


def _lru(xr, yg, conv0, h0, w_conv, b_conv, w_gate, b_a, b_i, lam, batch, seq, tt):
    width = xr.shape[1]
    nt = seq // tt
    gw = w_gate.shape[1]
    rows = lambda b, t: (b * nt + t, 0)
    const2 = lambda b, t: (0, 0)
    per_b = lambda b, t: (b, 0, 0)
    tail = CONV_WIDTH - 1
    return pl.pallas_call(
        _lru_kernel,
        out_shape=[
            jax.ShapeDtypeStruct((batch * seq, width), BF16),
            jax.ShapeDtypeStruct((batch, tail, width), F32),
            jax.ShapeDtypeStruct((batch, 1, width), F32),
        ],
        grid=(batch, nt),
        in_specs=[
            pl.BlockSpec((tt, width), rows),
            pl.BlockSpec((tt, width), rows),
            pl.BlockSpec((1, tail, width), per_b),
            pl.BlockSpec((1, 1, width), per_b),
            pl.BlockSpec((CONV_WIDTH, width), const2),
            pl.BlockSpec((1, width), const2),
            pl.BlockSpec((width // gw, gw, 2 * gw), lambda b, t: (0, 0, 0)),
            pl.BlockSpec((1, width), const2),
            pl.BlockSpec((1, width), const2),
            pl.BlockSpec((1, width), const2),
        ],
        out_specs=[
            pl.BlockSpec((tt, width), rows),
            pl.BlockSpec((1, tail, width), per_b),
            pl.BlockSpec((1, 1, width), per_b),
        ],
        scratch_shapes=[
            pltpu.VMEM((_HALO + tt, width), F32),
            pltpu.VMEM((tt, width), F32),
            pltpu.VMEM((tt, width), F32),
            pltpu.VMEM((1, width), F32),
        ],
        compiler_params=_cparams(("parallel", "arbitrary")),
        name="lru",
    )(xr, yg, conv0, h0, w_conv, b_conv, w_gate, b_a, b_i, lam)


def _mix_norm_kernel(*refs, aliased, n_tiles):
    ins = refs[:11]
    outs = refs[12:] if aliased else refs[11:]
    live = pl.program_id(0) < n_tiles

    @pl.when(live)
    def _():
        _mix_norm_body(*ins, *outs)

    @pl.when(jnp.logical_not(live))
    def _():
        outs[1][...] = jnp.zeros(outs[1].shape, outs[1].dtype)


def _mix_norm_body(fox_ref, lru_ref, x_ref, wt_ref, wb_ref, g_ref, b_ref, wrc_ref, wrh_ref, br_ref, cnt0_ref,
                   x1_ref, x1b_ref, idx_ref, gate_ref, rank_ref, cnt_ref):
    tm = x_ref.shape[0]

    mix = jnp.dot(fox_ref[...], wt_ref[...], preferred_element_type=F32)
    mix = mix + jnp.dot(lru_ref[...], wb_ref[...], preferred_element_type=F32)
    x1 = _layer_norm(DEEPNORM_ALPHA * x_ref[...] + mix, g_ref[...], b_ref[...])
    x1_ref[...] = x1
    x1_hi = x1.astype(BF16)
    x1b_ref[...] = x1_hi
    x1_lo = (x1 - x1_hi.astype(F32)).astype(BF16)
    lg2 = jnp.dot(x1_hi, wrc_ref[...], preferred_element_type=F32)
    lg = lg2[:, :N_EXPERTS] + lg2[:, N_EXPERTS:] + jnp.dot(x1_lo, wrh_ref[...], preferred_element_type=F32)
    lg = lg + br_ref[...]
    lanes = lax.broadcasted_iota(jnp.int32, lg.shape, 1)
    vals, picks = [], []
    for k in range(TOP_K):
        m = jnp.max(lg, axis=-1, keepdims=True)
        ix = jnp.min(jnp.where(lg == m, lanes, N_EXPERTS), axis=-1, keepdims=True)
        idx_ref[:, k:k + 1] = ix
        vals.append(m)
        picks.append(lanes == ix)
        lg = jnp.where(picks[-1], -jnp.inf, lg)
    es = [jnp.exp(v - vals[0]) for v in vals]
    denom = es[0] + es[1] + es[2] + es[3]
    for k in range(TOP_K):
        gate_ref[:, k:k + 1] = es[k] / denom

    @pl.when(pl.program_id(0) == 0)
    def _():
        cnt_ref[...] = cnt0_ref[...]

    sel = jnp.where(picks[0], 1.0, 0.0)
    for k in range(1, TOP_K):
        sel = sel + jnp.where(picks[k], 1.0, 0.0)
    earlier = lax.broadcasted_iota(jnp.int32, (tm, tm), 1) < lax.broadcasted_iota(jnp.int32, (tm, tm), 0)
    tri = jnp.where(earlier, 1.0, 0.0).astype(BF16)
    before = jnp.dot(tri, sel.astype(BF16), preferred_element_type=F32) + cnt_ref[...]
    for k in range(TOP_K):
        rank_ref[:, k:k + 1] = jnp.sum(jnp.where(picks[k], before, 0.0), axis=-1, keepdims=True).astype(jnp.int32)
    cnt_ref[...] += jnp.sum(sel, axis=0, keepdims=True)


def _mix_norm(fox_o, lru_o, x2d, w_top, w_bot, ln_g, ln_b, wr_cat, wr_hi, b_router, cnt0, tm, n_all, row_block0,
              x1b_buf=None):
    n, d = x2d.shape
    half = fox_o.shape[1]
    n_tiles = n // tm
    aliased = x1b_buf is not None
    steps = n_tiles if aliased else n_all // tm - row_block0
    row = lambda i: (jnp.minimum(i, n_tiles - 1), 0)
    const = lambda i: (0, 0)
    in_specs = [
        pl.BlockSpec((tm, half), row),
        pl.BlockSpec((tm, half), row),
        pl.BlockSpec((tm, d), row),
        pl.BlockSpec((half, d), const),
        pl.BlockSpec((half, d), const),
        pl.BlockSpec((1, d), const),
        pl.BlockSpec((1, d), const),
        pl.BlockSpec((d, 2 * N_EXPERTS), const),
        pl.BlockSpec((d, N_EXPERTS), const),
        pl.BlockSpec((1, N_EXPERTS), const),
        pl.BlockSpec((1, N_EXPERTS), const),
    ]
    args = [fox_o, lru_o, x2d, w_top, w_bot, ln_g, ln_b, wr_cat, wr_hi, b_router, cnt0]
    if aliased:
        in_specs.append(pl.BlockSpec(memory_space=pl.ANY))
        args.append(x1b_buf)
    return pl.pallas_call(
        functools.partial(_mix_norm_kernel, aliased=aliased, n_tiles=n_tiles),
        out_shape=[
            jax.ShapeDtypeStruct((n, d), F32),
            jax.ShapeDtypeStruct((n_all, d), BF16),
            jax.ShapeDtypeStruct((n, TOP_K), jnp.int32),
            jax.ShapeDtypeStruct((n, TOP_K), F32),
            jax.ShapeDtypeStruct((n, TOP_K), jnp.int32),
            jax.ShapeDtypeStruct((1, N_EXPERTS), F32),
        ],
        grid=(steps,),
        in_specs=in_specs,
        out_specs=[
            pl.BlockSpec((tm, d), row),
            pl.BlockSpec((tm, d), lambda i: (row_block0 + i, 0)),
            pl.BlockSpec((tm, TOP_K), row),
            pl.BlockSpec((tm, TOP_K), row),
            pl.BlockSpec((tm, TOP_K), row),
            pl.BlockSpec((1, N_EXPERTS), const),
        ],
        input_output_aliases={11: 1} if aliased else {},
        compiler_params=_cparams(("arbitrary",)),
        name="mix_norm",
    )(*args)


_RING = 8
_AHEAD = 6


def _expert_kernel(be_ref, bv_ref, nu_ref, x_ref, bgu_ref, bd_ref, wgu_hbm, wd_hbm, o_ref,
                   act_ref, ring_ref, sem_ref, *, sub, tc):
    rb = pl.program_id(0)
    valid = bv_ref[rb]
    tm, d = x_ref.shape
    d_ff = act_ref.shape[1]
    nfc = d_ff // tc
    nnc = d // tc
    per_block = 2 * nfc + nnc
    total = nu_ref[0] * per_block

    def chunk_copy(src, slot):
        return pltpu.make_async_copy(src, ring_ref.at[slot], sem_ref.at[slot])

    def start(cg):
        @pl.when(cg < total)
        def _():
            blk = cg // per_block
            j = cg - blk * per_block
            e = be_ref[blk]
            slot = lax.rem(cg, _RING)

            @pl.when(j < 2 * nfc)
            def _():
                col = pl.multiple_of((lax.rem(j, 2) * nfc + j // 2) * tc, tc)
                chunk_copy(wgu_hbm.at[e, :, pl.ds(col, tc)], slot).start()

            @pl.when(j >= 2 * nfc)
            def _():
                col = pl.multiple_of((j - 2 * nfc) * tc, tc)
                chunk_copy(wd_hbm.at[e, :, pl.ds(col, tc)], slot).start()

    def wait(cg):
        slot = lax.rem(cg, _RING)
        chunk_copy(wgu_hbm.at[0, :, pl.ds(0, tc)], slot).wait()
        return slot

    @pl.when(rb == 0)
    def _():
        for c in range(_AHEAD):
            start(c)

    base = rb * per_block

    def block(nrows):
        def up(j, carry):
            cg = base + 2 * j
            start(cg + _AHEAD)
            start(cg + 1 + _AHEAD)
            gate_slot = wait(cg)
            up_slot = wait(cg + 1)
            col = pl.ds(pl.multiple_of(j * tc, tc), tc)
            ucol = pl.ds(pl.multiple_of(d_ff + j * tc, tc), tc)
            x = x_ref[0:nrows, :]
            hg = jnp.dot(x, ring_ref[gate_slot].astype(BF16), preferred_element_type=F32) + bgu_ref[0, :, col]
            hu = jnp.dot(x, ring_ref[up_slot].astype(BF16), preferred_element_type=F32) + bgu_ref[0, :, ucol]
            gate = jnp.minimum(hg, SWIGLU_LIMIT)
            upv = jnp.clip(hu, -SWIGLU_LIMIT, SWIGLU_LIMIT)
            act = (upv + 1.0) * gate * jax.nn.sigmoid(SWIGLU_ALPHA * gate)
            act_ref[0:nrows, col] = act.astype(BF16)
            return carry

        lax.fori_loop(0, nfc, up, 0)

        def down(n, carry):
            cg = base + 2 * nfc + n
            start(cg + _AHEAD)
            slot = wait(cg)
            col = pl.ds(pl.multiple_of(n * tc, tc), tc)
            y = jnp.dot(act_ref[0:nrows, :], ring_ref[slot].astype(BF16), preferred_element_type=F32)
            o_ref[0:nrows, col] = (y + bd_ref[0, :, col]).astype(o_ref.dtype)
            return carry

        lax.fori_loop(0, nnc, down, 0)
        if nrows < tm:
            o_ref[nrows:tm, :] = jnp.zeros((tm - nrows, d), o_ref.dtype)

    for nrows in range(sub, tm + 1, sub):
        @pl.when(jnp.logical_and(valid > nrows - sub, valid <= nrows))
        def _(nrows=nrows):
            block(nrows)

    @pl.when(valid == 0)
    def _():
        o_ref[...] = jnp.zeros(o_ref.shape, o_ref.dtype)


def _experts(x_pad, block_e, block_valid, n_used, w_gu, b_gu, w_down, b_down, tm, tc, sub):
    p, d = x_pad.shape
    d_ff = w_down.shape[1]
    assert d_ff == d and _AHEAD <= _RING - 2
    nb = p // tm

    def rbc(rb, nu):
        return jnp.minimum(rb, nu[0] - 1)

    grid_spec = pltpu.PrefetchScalarGridSpec(
        num_scalar_prefetch=3,
        grid=(nb,),
        in_specs=[
            pl.BlockSpec((tm, d), lambda rb, be, bv, nu: (rbc(rb, nu), 0)),
            pl.BlockSpec((1, 1, 2 * d_ff), lambda rb, be, bv, nu: (be[rbc(rb, nu)], 0, 0)),
            pl.BlockSpec((1, 1, d), lambda rb, be, bv, nu: (be[rbc(rb, nu)], 0, 0)),
            pl.BlockSpec(memory_space=pl.ANY),
            pl.BlockSpec(memory_space=pl.ANY),
        ],
        out_specs=pl.BlockSpec((tm, d), lambda rb, be, bv, nu: (rb, 0)),
        scratch_shapes=[
            pltpu.VMEM((tm, d_ff), BF16),
            pltpu.VMEM((_RING, d, tc), F32),
            pltpu.SemaphoreType.DMA((_RING,)),
        ],
    )
    return pl.pallas_call(
        functools.partial(_expert_kernel, sub=sub, tc=tc),
        out_shape=jax.ShapeDtypeStruct((p, d), BF16),
        grid_spec=grid_spec,
        compiler_params=_cparams(("arbitrary",)),
        name="experts",
    )(block_e, block_valid, n_used, x_pad, b_gu, b_down, w_gu, w_down)


def _combine_kernel(y_ref, gate_ref, x1_ref, g_ref, b_ref, o_ref):
    y = y_ref[0].astype(F32) * gate_ref[:, 0:1]
    for k in range(1, TOP_K):
        y = y + y_ref[k].astype(F32) * gate_ref[:, k:k + 1]
    o_ref[...] = _layer_norm(DEEPNORM_ALPHA * x1_ref[...] + y, g_ref[...], b_ref[...])


def _combine(y_rows, gates, x1, ln_g, ln_b, row_block0, tn):
    n, d = x1.shape
    const = lambda i: (0, 0)
    return pl.pallas_call(
        _combine_kernel,
        out_shape=jax.ShapeDtypeStruct((n, d), F32),
        grid=(n // tn,),
        in_specs=[
            pl.BlockSpec((TOP_K, tn, d), lambda i: (0, row_block0 + i, 0)),
            pl.BlockSpec((tn, TOP_K), lambda i: (i, 0)),
            pl.BlockSpec((tn, d), lambda i: (i, 0)),
            pl.BlockSpec((1, d), const),
            pl.BlockSpec((1, d), const),
        ],
        out_specs=pl.BlockSpec((tn, d), lambda i: (i, 0)),
        compiler_params=_cparams(("parallel",)),
        name="combine",
    )(y_rows, gates, x1, ln_g, ln_b)


def _route(idx, rank, sizes, tm):
    n = idx.shape[0]
    nk = n * TOP_K
    nblk = (sizes + tm - 1) // tm
    bends = jnp.cumsum(nblk)
    bstart = bends - nblk
    dest = bstart[idx] * tm + rank
    nb = -(-nk // tm) + N_EXPERTS
    blk = jnp.arange(nb, dtype=jnp.int32)
    block_e = jnp.minimum(jnp.sum(bends[None, :] <= blk[:, None], axis=1), N_EXPERTS - 1).astype(jnp.int32)
    n_used = bends[-1].astype(jnp.int32)
    valid = jnp.clip(sizes[block_e] - (blk - bstart[block_e]) * tm, 0, tm)
    valid = jnp.where(blk < n_used, valid, 0).astype(jnp.int32)
    tok = jnp.broadcast_to(jnp.arange(n, dtype=jnp.int32)[:, None], (n, TOP_K))
    row_tok = (jnp.arange(nb * tm, dtype=jnp.int32) % n).at[dest.reshape(-1)].set(tok.reshape(-1))
    return dest, row_tok, block_e, valid, n_used.reshape(1)


def _pick(n, pref):
    t = min(n, pref)
    while n % t:
        t //= 2
    return t


def kernel(x_prompt, x_sample, cache_k, cache_v, cache_logf, state_conv, state_lru, w_in, b_f, w_conv, b_conv, w_a, b_a, w_i, b_i, lam, w_out, ln1_g, ln1_b, w_router, b_router, w_gu, b_gu, w_down, b_down, ln2_g, ln2_b):
    assert w_in.shape[0] == DEPTH
    bp, tp, d = x_prompt.shape
    bs, ts, _ = x_sample.shape
    past = cache_k.shape[2]
    lru_w = w_conv.shape[-1]
    np_, ns = bp * tp, bs * ts

    win = w_in[0]
    f0 = 3 * FOX_WIDTH
    w_main = jnp.concatenate([win[:, :f0], win[:, f0 + FOX_HEADS:]], axis=1).astype(BF16)
    w_f = jnp.pad(win[:, f0:f0 + FOX_HEADS], ((0, 0), (0, 128 - FOX_HEADS))).astype(BF16)
    bf2 = b_f[0].reshape(1, FOX_HEADS)
    gpb = 4
    bd = w_a.shape[-1]
    eye = jnp.eye(gpb, dtype=F32)

    def blockdiag(w):
        wg = w.reshape(LRU_BLOCKS // gpb, gpb, bd, bd)
        return jnp.einsum("gacd,ab->gacbd", wg, eye).reshape(LRU_BLOCKS // gpb, gpb * bd, gpb * bd)

    w_gate = jnp.concatenate([blockdiag(w_a[0]), blockdiag(w_i[0])], axis=-1).astype(BF16)
    w_top = w_out[0, :FOX_WIDTH].astype(BF16)
    w_bot = w_out[0, FOX_WIDTH:].astype(BF16)
    wr = w_router[0]
    wr_hi = wr.astype(BF16)
    wr_cat = jnp.concatenate([wr_hi, (wr - wr_hi.astype(F32)).astype(BF16)], axis=1)
    row = lambda a: a.reshape(1, -1)

    n_all = np_ + ns
    tm_mix = _pick(ns, 512)

    def mixers(x, batch, seq, fox_fn, conv0, h0, cnt0, x1b_buf, row0):
        n = batch * seq
        q, k, v, kb, vb, xr, yg, logf = _in_proj(x.reshape(n, d), w_main, w_f, bf2, _pick(n, 512))
        fox_o = fox_fn(q, kb, vb, logf.reshape(batch, seq, FOX_HEADS))
        lru_o, conv_new, h_last = _lru(xr, yg, conv0, h0, w_conv[0], row(b_conv[0]), w_gate,
                                       row(b_a[0]), row(b_i[0]), row(lam[0]), batch, seq, _pick(seq, 256))
        routed = _mix_norm(fox_o, lru_o, x.reshape(n, d), w_top, w_bot, row(ln1_g[0]), row(ln1_b[0]),
                           wr_cat, wr_hi, row(b_router[0]), cnt0, tm_mix, n_all, row0 // tm_mix, x1b_buf)
        state = (k.reshape(1, batch, seq, FOX_HEADS, FOX_HEAD_DIM), v.reshape(1, batch, seq, FOX_HEADS, FOX_HEAD_DIM),
                 logf.reshape(1, batch, seq, FOX_HEADS), conv_new[None], h_last.reshape(1, batch, lru_w))
        return routed, state

    def fox_p(q, kb, vb, logf):
        return _fox_prompt(q, kb, vb, jnp.cumsum(logf, axis=1) * LOG2E, bp, tp, _pick(tp, 512))

    def fox_s(q, kb, vb, logf):
        c_all = jnp.cumsum(jnp.concatenate([cache_logf[0], logf], axis=1), axis=1) * LOG2E
        cache_rows = lambda c: c[0].reshape(bs, past * FOX_HEADS, FOX_HEAD_DIM)
        return _fox_sample(q, kb, vb, cache_rows(cache_k), cache_rows(cache_v), c_all, bs, ts, _pick(past, 512))

    (x1p, x1b, idxp, gp, rankp, cnt_p), state_p = mixers(
        x_prompt, bp, tp, fox_p, jnp.zeros((bp, CONV_WIDTH - 1, lru_w), F32), jnp.zeros((bp, 1, lru_w), F32),
        jnp.zeros((1, N_EXPERTS), F32), None, 0)
    (x1s, x1b, idxs, gs, ranks, cnt_all), state_s = mixers(
        x_sample, bs, ts, fox_s, state_conv[0], state_lru[0].reshape(bs, 1, lru_w), cnt_p, x1b, np_)

    tm_e, tc_e, sub_e = 1024, 256, 256
    idx = jnp.concatenate([idxp, idxs], axis=0)
    rank = jnp.concatenate([rankp, ranks], axis=0)
    dest, row_tok, block_e, block_valid, n_used = _route(idx, rank, cnt_all[0].astype(jnp.int32), tm_e)
    x_pad = x1b[row_tok]
    y_pad = _experts(x_pad, block_e, block_valid, n_used, w_gu[0], b_gu[0].reshape(N_EXPERTS, 1, -1),
                     w_down[0], b_down[0].reshape(N_EXPERTS, 1, -1), tm_e, tc_e, sub_e)
    y_rows = y_pad[dest.T]

    tn = _pick(ns, 512)
    yp = _combine(y_rows, gp, x1p, row(ln2_g[0]), row(ln2_b[0]), 0, tn)
    ys = _combine(y_rows, gs, x1s, row(ln2_g[0]), row(ln2_b[0]), np_ // tn, tn)
    return (yp.reshape(bp, tp, d), ys.reshape(bs, ts, d)) + state_p + state_s
```

```python
import functools

import jax
import jax.numpy as jnp
from jax import lax
from jax.experimental import pallas as pl
from jax.experimental.pallas import tpu as pltpu

F32 = jnp.float32
BF16 = jnp.bfloat16

FOX_HEADS = 8
FOX_HEAD_DIM = 128
FOX_WIDTH = FOX_HEADS * FOX_HEAD_DIM
LRU_BLOCKS = 16
CONV_WIDTH = 4
LRU_C = 8.0
N_EXPERTS = 32
TOP_K = 4
SWIGLU_LIMIT = 7.0
SWIGLU_ALPHA = 1.702
LN_EPS = 1e-5
DEPTH = 1
DEEPNORM_ALPHA = (2.0 * DEPTH) ** 0.25
LOG2E = 1.4426950408889634
Q_SCALE = FOX_HEAD_DIM ** -0.5 * LOG2E

VMEM_LIMIT = 56 * 1024 * 1024


def _cparams(sem):
    return pltpu.CompilerParams(dimension_semantics=sem, vmem_limit_bytes=VMEM_LIMIT)


def _log_sigmoid(x):
    return jnp.minimum(x, 0.0) - jnp.log1p(jnp.exp(-jnp.abs(x)))


def _layer_norm(z, g, b):
    mu = jnp.mean(z, axis=-1, keepdims=True)
    zc = z - mu
    var = jnp.mean(zc * zc, axis=-1, keepdims=True)
    return zc * lax.rsqrt(var + LN_EPS) * g + b


def _in_proj_kernel(x_ref, w_ref, wf_ref, bf_ref, q_ref, k_ref, v_ref, kb_ref, vb_ref, xr_ref, yg_ref, lf_ref,
                    xb_ref):
    j = pl.program_id(1)

    @pl.when(j == 0)
    def _():
        xb_ref[...] = x_ref[...].astype(BF16)
        zf = jnp.dot(xb_ref[...], wf_ref[...], preferred_element_type=F32)
        lf_ref[...] = _log_sigmoid(zf[:, :FOX_HEADS] + bf_ref[...])

    z = jnp.dot(xb_ref[...], w_ref[...], preferred_element_type=F32)

    @pl.when(j == 0)
    def _():
        q_ref[...] = (z * Q_SCALE).astype(BF16)

    @pl.when(j == 1)
    def _():
        k_ref[...] = z
        kb_ref[...] = z.astype(BF16)

    @pl.when(j == 2)
    def _():
        v_ref[...] = z
        vb_ref[...] = z.astype(BF16)

    @pl.when(j == 3)
    def _():
        xr_ref[...] = z

    @pl.when(j == 4)
    def _():
        yg_ref[...] = z


def _in_proj(x2d, w_main, w_f, b_f, tm):
    n, d = x2d.shape
    wcol = FOX_WIDTH
    row = lambda i, j: (i, 0)
    dts = (BF16, F32, F32, BF16, BF16, F32, F32)
    outs = [jax.ShapeDtypeStruct((n, wcol), dt) for dt in dts]
    outs.append(jax.ShapeDtypeStruct((n, FOX_HEADS), F32))
    return pl.pallas_call(
        _in_proj_kernel,
        out_shape=outs,
        grid=(n // tm, 5),
        in_specs=[
            pl.BlockSpec((tm, d), row),
            pl.BlockSpec((d, wcol), lambda i, j: (0, j)),
            pl.BlockSpec((d, 128), lambda i, j: (0, 0)),
            pl.BlockSpec((1, FOX_HEADS), lambda i, j: (0, 0)),
        ],
        out_specs=[pl.BlockSpec((tm, wcol), row)] * len(dts) + [pl.BlockSpec((tm, FOX_HEADS), row)],
        scratch_shapes=[pltpu.VMEM((tm, d), BF16)],
        compiler_params=_cparams(("parallel", "arbitrary")),
        name="in_proj",
    )(x2d, w_main, w_f, b_f)


def _cumsum_kernel(s_ref, c0_ref, o_ref, tri_ref):
    t = s_ref.shape[1]

    @pl.when(pl.program_id(0) == 0)
    def _():
        row = lax.broadcasted_iota(jnp.int32, (t, t), 0)
        col = lax.broadcasted_iota(jnp.int32, (t, t), 1)
        tri_ref[...] = jnp.where(col <= row, 1.0, 0.0).astype(BF16)

    c3 = jnp.dot(tri_ref[...], s_ref[0], preferred_element_type=F32)
    h = FOX_HEADS
    o_ref[0] = c3[:, 0:h] + c3[:, h:2 * h] + c3[:, 2 * h:3 * h] + c0_ref[0]


def _cumsum_time(x, c0):
    b, t, h = x.shape
    hi = x.astype(BF16)
    r1 = x - hi.astype(F32)
    mid = r1.astype(BF16)
    lo = (r1 - mid.astype(F32)).astype(BF16)
    pieces = jnp.concatenate([hi, mid, lo, jnp.zeros((b, t, 128 - 3 * h), BF16)], axis=-1)
    return pl.pallas_call(
        _cumsum_kernel,
        out_shape=jax.ShapeDtypeStruct((b, t, h), F32),
        grid=(b,),
        in_specs=[
            pl.BlockSpec((1, t, 128), lambda i: (i, 0, 0)),
            pl.BlockSpec((1, 1, h), lambda i: (i, 0, 0)),
        ],
        out_specs=pl.BlockSpec((1, t, h), lambda i: (i, 0, 0)),
        scratch_shapes=[pltpu.VMEM((t, t), BF16)],
        compiler_params=_cparams(("arbitrary",)),
        name="cumsum_time",
    )(pieces, c0)


_NT = (((1,), (1,)), ((), ()))


def _attn_init(m_ref, l_ref, acc_ref):
    m_ref[...] = jnp.full(m_ref.shape, -jnp.inf, F32)
    l_ref[...] = jnp.zeros(l_ref.shape, F32)
    acc_ref[...] = jnp.zeros(acc_ref.shape, F32)


def _fox_prompt_kernel(qi_ref, ki_ref, q_ref, k_ref, v_ref, ck_ref, o_ref,
                       m_ref, l_ref, acc_ref, s0_ref, s1_ref, p0_ref, p1_ref, a0_ref, a1_ref, *, rs):
    pair = pl.program_id(1)
    qi = qi_ref[pair]
    ki = ki_ref[pair]
    tq = q_ref.shape[0]
    tk = k_ref.shape[0]
    lanes = FOX_HEAD_DIM

    @pl.when(ki == 0)
    def _():
        _attn_init(m_ref, l_ref, acc_ref)

    def head_lanes(h):
        return pl.ds(pl.multiple_of(h * lanes, lanes), lanes)

    def scores(h, s_ref):
        sl = head_lanes(h)
        s_ref[...] = lax.dot_general(q_ref[:, sl], k_ref[:, sl], _NT, preferred_element_type=F32)

    def weighted_values(h, p_ref, a_ref):
        sl = head_lanes(h)
        pv = jnp.dot(p_ref[...], v_ref[:, sl], preferred_element_type=F32)
        acc_ref[:, sl] = a_ref[...] * acc_ref[:, sl] + pv

    def softmax(h, s_ref, p_ref, a_ref, masked):
        ck = ck_ref[0, h]
        for r in range(tq // rs):
            r0 = r * rs
            rsl = slice(r0, r0 + rs)
            chunks = []
            for c in range(tk // lanes):
                if masked and c * lanes > r0 + rs - 1:
                    continue
                s = s_ref[rsl, c * lanes:(c + 1) * lanes] - ck[:, c * lanes:(c + 1) * lanes]
                if masked and (c + 1) * lanes - 1 > r0:
                    row = r0 + lax.broadcasted_iota(jnp.int32, (rs, lanes), 0)
                    col = c * lanes + lax.broadcasted_iota(jnp.int32, (rs, lanes), 1)
                    s = jnp.where(col <= row, s, -jnp.inf)
                chunks.append((c, s))
            mc = chunks[0][1]
            for _, s in chunks[1:]:
                mc = jnp.maximum(mc, s)
            m_prev = m_ref[h, rsl, :]
            m_new = jnp.maximum(m_prev, jnp.max(mc, axis=-1, keepdims=True))
            alpha = jnp.exp2(m_prev - m_new)
            psum = None
            for c, s in chunks:
                p = jnp.exp2(s - m_new)
                p_ref[rsl, c * lanes:(c + 1) * lanes] = p.astype(BF16)
                psum = p if psum is None else psum + p
            for c in range(chunks[-1][0] + 1, tk // lanes):
                p_ref[rsl, c * lanes:(c + 1) * lanes] = jnp.zeros((rs, lanes), BF16)
            l_ref[h, rsl, :] = alpha * l_ref[h, rsl, :] + psum
            m_ref[h, rsl, :] = m_new
            a_ref[rsl, :] = alpha

    def all_heads(masked):
        scores(0, s0_ref)
        p1_ref[...] = jnp.zeros(p1_ref.shape, BF16)
        a1_ref[...] = jnp.ones(a1_ref.shape, F32)

        def two_heads(i, carry):
            h0 = 2 * i
            h1 = h0 + 1
            scores(h1, s1_ref)
            softmax(h0, s0_ref, p0_ref, a0_ref, masked)
            weighted_values(lax.rem(h0 + FOX_HEADS - 1, FOX_HEADS), p1_ref, a1_ref)
            scores(lax.rem(h0 + 2, FOX_HEADS), s0_ref)
            softmax(h1, s1_ref, p1_ref, a1_ref, masked)
            weighted_values(h0, p0_ref, a0_ref)
            return carry

        lax.fori_loop(0, FOX_HEADS // 2, two_heads, 0)
        weighted_values(FOX_HEADS - 1, p1_ref, a1_ref)

    @pl.when(ki < qi)
    def _():
        all_heads(False)

    @pl.when(ki == qi)
    def _():
        all_heads(True)
        for h in range(FOX_HEADS):
            sl = slice(h * lanes, (h + 1) * lanes)
            l_tot = jnp.sum(l_ref[h], axis=-1, keepdims=True)
            o_ref[:, sl] = (acc_ref[:, sl] / l_tot).astype(o_ref.dtype)


def _fox_prompt(q, kb, vb, c2, batch, seq, tq):
    nq = seq // tq
    width = q.shape[1]
    ck = jnp.transpose(c2, (0, 2, 1))[:, :, None, :]
    pairs = [(i, j) for i in range(nq) for j in range(i + 1)]
    qi_tab = jnp.asarray([p[0] for p in pairs], jnp.int32)
    ki_tab = jnp.asarray([p[1] for p in pairs], jnp.int32)
    q_map = lambda b, p, qt, kt: (b * nq + qt[p], 0)
    k_map = lambda b, p, qt, kt: (b * nq + kt[p], 0)
    grid_spec = pltpu.PrefetchScalarGridSpec(
        num_scalar_prefetch=2,
        grid=(batch, len(pairs)),
        in_specs=[
            pl.BlockSpec((tq, width), q_map),
            pl.BlockSpec((tq, width), k_map),
            pl.BlockSpec((tq, width), k_map),
            pl.BlockSpec((1, FOX_HEADS, 1, tq), lambda b, p, qt, kt: (b, 0, 0, kt[p])),
        ],
        out_specs=pl.BlockSpec((tq, width), q_map),
        scratch_shapes=[
            pltpu.VMEM((FOX_HEADS, tq, FOX_HEAD_DIM), F32),
            pltpu.VMEM((FOX_HEADS, tq, FOX_HEAD_DIM), F32),
            pltpu.VMEM((tq, width), F32),
            pltpu.VMEM((tq, tq), F32),
            pltpu.VMEM((tq, tq), F32),
            pltpu.VMEM((tq, tq), BF16),
            pltpu.VMEM((tq, tq), BF16),
            pltpu.VMEM((tq, FOX_HEAD_DIM), F32),
            pltpu.VMEM((tq, FOX_HEAD_DIM), F32),
        ],
    )
    return pl.pallas_call(
        functools.partial(_fox_prompt_kernel, rs=min(64, tq)),
        out_shape=jax.ShapeDtypeStruct(q.shape, BF16),
        grid_spec=grid_spec,
        compiler_params=_cparams(("parallel", "arbitrary")),
        name="fox_prompt",
    )(qi_tab, ki_tab, q, kb, vb, ck)


def _fox_sample_kernel(q_ref, kc_ref, vc_ref, kn_ref, vn_ref, ckc_ref, ckn_ref, o_ref, m_ref, l_ref, acc_ref):
    j = pl.program_id(1)
    tq = q_ref.shape[0]
    tk = kc_ref.shape[1] // FOX_HEADS
    lanes = FOX_HEAD_DIM

    @pl.when(j == 0)
    def _():
        _attn_init(m_ref, l_ref, acc_ref)

    def update(h, k, v, ck, masked):
        sl = slice(h * lanes, (h + 1) * lanes)
        s = lax.dot_general(q_ref[:, sl], k, _NT, preferred_element_type=F32) - ck
        if masked:
            row = lax.broadcasted_iota(jnp.int32, s.shape, 0)
            col = lax.broadcasted_iota(jnp.int32, s.shape, 1)
            s = jnp.where(col <= row, s, -jnp.inf)
        m_prev = m_ref[h]
        m_new = jnp.maximum(m_prev, jnp.max(s, axis=-1, keepdims=True))
        alpha = jnp.exp2(m_prev - m_new)
        p = jnp.exp2(s - m_new[:, 0:1])
        l_ref[h] = alpha * l_ref[h] + jnp.sum(p, axis=-1, keepdims=True)
        acc_ref[:, sl] = alpha * acc_ref[:, sl] + jnp.dot(p.astype(BF16), v, preferred_element_type=F32)
        m_ref[h] = m_new

    for h in range(FOX_HEADS):
        k = kc_ref[0, pl.ds(h, tk, stride=FOX_HEADS), :].astype(BF16)
        v = vc_ref[0, pl.ds(h, tk, stride=FOX_HEADS), :].astype(BF16)
        update(h, k, v, ckc_ref[0, h], False)

    @pl.when(j == pl.num_programs(1) - 1)
    def _():
        for h in range(FOX_HEADS):
            sl = slice(h * lanes, (h + 1) * lanes)
            update(h, kn_ref[:, sl], vn_ref[:, sl], ckn_ref[0, h], True)
            o_ref[:, sl] = (acc_ref[:, sl] / l_ref[h]).astype(o_ref.dtype)


def _fox_sample(q, kb, vb, cache_k, cache_v, c2_all, batch, seq, tk):
    past = cache_k.shape[1] // FOX_HEADS
    width = q.shape[1]
    ct = jnp.transpose(c2_all, (0, 2, 1))
    ckc = ct[:, :, None, :past]
    ckn = ct[:, :, None, past:]
    new_map = lambda b, j: (b, 0)
    cache_spec = pl.BlockSpec((1, tk * FOX_HEADS, FOX_HEAD_DIM), lambda b, j: (b, j, 0))
    return pl.pallas_call(
        _fox_sample_kernel,
        out_shape=jax.ShapeDtypeStruct(q.shape, BF16),
        grid=(batch, past // tk),
        in_specs=[
            pl.BlockSpec((seq, width), new_map),
            cache_spec,
            cache_spec,
            pl.BlockSpec((seq, width), new_map),
            pl.BlockSpec((seq, width), new_map),
            pl.BlockSpec((1, FOX_HEADS, 1, tk), lambda b, j: (b, 0, 0, j)),
            pl.BlockSpec((1, FOX_HEADS, 1, seq), lambda b, j: (b, 0, 0, 0)),
        ],
        out_specs=pl.BlockSpec((seq, width), new_map),
        scratch_shapes=[
            pltpu.VMEM((FOX_HEADS, seq, FOX_HEAD_DIM), F32),
            pltpu.VMEM((FOX_HEADS, seq, FOX_HEAD_DIM), F32),
            pltpu.VMEM((seq, width), F32),
        ],
        compiler_params=_cparams(("parallel", "arbitrary")),
        name="fox_sample",
    )(q, cache_k, cache_v, kb, vb, ckc, ckn)


_HALO = 8


def _lru_kernel(xr_ref, yg_ref, conv0_ref, h0_ref, wc_ref, bc_ref, wg_ref, ba_ref, bi_ref, lam_ref,
                o_ref, convo_ref, hlast_ref, xp_ref, a_ref, b_ref, h_ref):
    t = pl.program_id(1)
    tt, width = xr_ref.shape
    tail = CONV_WIDTH - 1
    lo = _HALO - tail

    @pl.when(t == 0)
    def _():
        xp_ref[lo:_HALO, :] = conv0_ref[0]
        h_ref[...] = h0_ref[0]

    xp_ref[_HALO:_HALO + tt, :] = xr_ref[...]
    xc = bc_ref[...] + xp_ref[lo:lo + tt, :] * wc_ref[0:1, :]
    for j in range(1, CONV_WIDTH):
        xc = xc + xp_ref[lo + j:lo + j + tt, :] * wc_ref[j:j + 1, :]
    new_tail = xp_ref[lo + tt:_HALO + tt, :]
    xp_ref[lo:_HALO, :] = new_tail

    xcb = xc.astype(BF16)
    gw = wg_ref.shape[1]
    lam = lam_ref[...]
    neg_sp = -(jnp.maximum(-lam, 0.0) + jnp.log1p(jnp.exp(-jnp.abs(lam))))
    for g in range(width // gw):
        sl = slice(g * gw, (g + 1) * gw)
        z = jnp.dot(xcb[:, sl], wg_ref[g], preferred_element_type=F32)
        r = jax.nn.sigmoid(z[:, :gw] + ba_ref[:, sl])
        i = jax.nn.sigmoid(z[:, gw:] + bi_ref[:, sl])
        log_a = LRU_C * r * neg_sp[:, sl]
        a = jnp.exp(log_a)
        a_ref[:, sl] = a
        b_ref[:, sl] = jnp.sqrt(-jnp.tanh(log_a) * (a * a + 1.0)) * i * xc[:, sl]

    def step(s, h):
        row = pl.ds(s, 1)
        h = a_ref[row, :] * h + b_ref[row, :]
        b_ref[row, :] = h
        return h

    h = lax.fori_loop(0, tt, step, h_ref[...], unroll=8)
    h_ref[...] = h
    o_ref[...] = (b_ref[...] * jax.nn.gelu(yg_ref[...])).astype(o_ref.dtype)

    @pl.when(t == pl.num_programs(1) - 1)
    def _():
        convo_ref[0] = new_tail
        hlast_ref[0] = h


def _lru(xr, yg, conv0, h0, w_conv, b_conv, w_gate, b_a, b_i, lam, batch, seq, tt):
    width = xr.shape[1]
    nt = seq // tt
    gw = w_gate.shape[1]
    rows = lambda b, t: (b * nt + t, 0)
    const2 = lambda b, t: (0, 0)
    per_b = lambda b, t: (b, 0, 0)
    tail = CONV_WIDTH - 1
    return pl.pallas_call(
        _lru_kernel,
        out_shape=[
            jax.ShapeDtypeStruct((batch * seq, width), BF16),
            jax.ShapeDtypeStruct((batch, tail, width), F32),
            jax.ShapeDtypeStruct((batch, 1, width), F32),
        ],
        grid=(batch, nt),
        in_specs=[
            pl.BlockSpec((tt, width), rows),
            pl.BlockSpec((tt, width), rows),
            pl.BlockSpec((1, tail, width), per_b),
            pl.BlockSpec((1, 1, width), per_b),
            pl.BlockSpec((CONV_WIDTH, width), const2),
            pl.BlockSpec((1, width), const2),
            pl.BlockSpec((width // gw, gw, 2 * gw), lambda b, t: (0, 0, 0)),
            pl.BlockSpec((1, width), const2),
            pl.BlockSpec((1, width), const2),
            pl.BlockSpec((1, width), const2),
        ],
        out_specs=[
            pl.BlockSpec((tt, width), rows),
            pl.BlockSpec((1, tail, width), per_b),
            pl.BlockSpec((1, 1, width), per_b),
        ],
        scratch_shapes=[
            pltpu.VMEM((_HALO + tt, width), F32),
            pltpu.VMEM((tt, width), F32),
            pltpu.VMEM((tt, width), F32),
            pltpu.VMEM((1, width), F32),
        ],
        compiler_params=_cparams(("parallel", "arbitrary")),
        name="lru",
    )(xr, yg, conv0, h0, w_conv, b_conv, w_gate, b_a, b_i, lam)


def _mix_norm_kernel(*refs, aliased, n_tiles):
    ins = refs[:11]
    outs = refs[12:] if aliased else refs[11:]
    live = pl.program_id(0) < n_tiles

    @pl.when(live)
    def _():
        _mix_norm_body(*ins, *outs)

    @pl.when(jnp.logical_not(live))
    def _():
        outs[1][...] = jnp.zeros(outs[1].shape, outs[1].dtype)


def _mix_norm_body(fox_ref, lru_ref, x_ref, wt_ref, wb_ref, g_ref, b_ref, wrc_ref, wrh_ref, br_ref, cnt0_ref,
                   x1_ref, x1b_ref, idx_ref, gate_ref, rank_ref, cnt_ref):
    tm = x_ref.shape[0]

    mix = jnp.dot(fox_ref[...], wt_ref[...], preferred_element_type=F32)
    mix = mix + jnp.dot(lru_ref[...], wb_ref[...], preferred_element_type=F32)
    x1 = _layer_norm(DEEPNORM_ALPHA * x_ref[...] + mix, g_ref[...], b_ref[...])
    x1_ref[...] = x1
    x1_hi = x1.astype(BF16)
    x1b_ref[...] = x1_hi
    x1_lo = (x1 - x1_hi.astype(F32)).astype(BF16)
    lg2 = jnp.dot(x1_hi, wrc_ref[...], preferred_element_type=F32)
    lg = lg2[:, :N_EXPERTS] + lg2[:, N_EXPERTS:] + jnp.dot(x1_lo, wrh_ref[...], preferred_element_type=F32)
    lg = lg + br_ref[...]
    lanes = lax.broadcasted_iota(jnp.int32, lg.shape, 1)
    vals, picks = [], []
    for k in range(TOP_K):
        m = jnp.max(lg, axis=-1, keepdims=True)
        ix = jnp.min(jnp.where(lg == m, lanes, N_EXPERTS), axis=-1, keepdims=True)
        idx_ref[:, k:k + 1] = ix
        vals.append(m)
        picks.append(lanes == ix)
        lg = jnp.where(picks[-1], -jnp.inf, lg)
    es = [jnp.exp(v - vals[0]) for v in vals]
    denom = es[0] + es[1] + es[2] + es[3]
    for k in range(TOP_K):
        gate_ref[:, k:k + 1] = es[k] / denom

    @pl.when(pl.program_id(0) == 0)
    def _():
        cnt_ref[...] = cnt0_ref[...]

    sel = jnp.where(picks[0], 1.0, 0.0)
    for k in range(1, TOP_K):
        sel = sel + jnp.where(picks[k], 1.0, 0.0)
    earlier = lax.broadcasted_iota(jnp.int32, (tm, tm), 1) < lax.broadcasted_iota(jnp.int32, (tm, tm), 0)
    tri = jnp.where(earlier, 1.0, 0.0).astype(BF16)
    before = jnp.dot(tri, sel.astype(BF16), preferred_element_type=F32) + cnt_ref[...]
    for k in range(TOP_K):
        rank_ref[:, k:k + 1] = jnp.sum(jnp.where(picks[k], before, 0.0), axis=-1, keepdims=True).astype(jnp.int32)
    cnt_ref[...] += jnp.sum(sel, axis=0, keepdims=True)


def _mix_norm(fox_o, lru_o, x2d, w_top, w_bot, ln_g, ln_b, wr_cat, wr_hi, b_router, cnt0, tm, n_all, row_block0,
              x1b_buf=None):
    n, d = x2d.shape
    half = fox_o.shape[1]
    n_tiles = n // tm
    aliased = x1b_buf is not None
    steps = n_tiles if aliased else n_all // tm - row_block0
    row = lambda i: (jnp.minimum(i, n_tiles - 1), 0)
    const = lambda i: (0, 0)
    in_specs = [
        pl.BlockSpec((tm, half), row),
        pl.BlockSpec((tm, half), row),
        pl.BlockSpec((tm, d), row),
        pl.BlockSpec((half, d), const),
        pl.BlockSpec((half, d), const),
        pl.BlockSpec((1, d), const),
        pl.BlockSpec((1, d), const),
        pl.BlockSpec((d, 2 * N_EXPERTS), const),
        pl.BlockSpec((d, N_EXPERTS), const),
        pl.BlockSpec((1, N_EXPERTS), const),
        pl.BlockSpec((1, N_EXPERTS), const),
    ]
    args = [fox_o, lru_o, x2d, w_top, w_bot, ln_g, ln_b, wr_cat, wr_hi, b_router, cnt0]
    if aliased:
        in_specs.append(pl.BlockSpec(memory_space=pl.ANY))
        args.append(x1b_buf)
    return pl.pallas_call(
        functools.partial(_mix_norm_kernel, aliased=aliased, n_tiles=n_tiles),
        out_shape=[
            jax.ShapeDtypeStruct((n, d), F32),
            jax.ShapeDtypeStruct((n_all, d), BF16),
            jax.ShapeDtypeStruct((n, TOP_K), jnp.int32),
            jax.ShapeDtypeStruct((n, TOP_K), F32),
            jax.ShapeDtypeStruct((n, TOP_K), jnp.int32),
            jax.ShapeDtypeStruct((1, N_EXPERTS), F32),
        ],
        grid=(steps,),
        in_specs=in_specs,
        out_specs=[
            pl.BlockSpec((tm, d), row),
            pl.BlockSpec((tm, d), lambda i: (row_block0 + i, 0)),
            pl.BlockSpec((tm, TOP_K), row),
            pl.BlockSpec((tm, TOP_K), row),
            pl.BlockSpec((tm, TOP_K), row),
            pl.BlockSpec((1, N_EXPERTS), const),
        ],
        input_output_aliases={11: 1} if aliased else {},
        compiler_params=_cparams(("arbitrary",)),
        name="mix_norm",
    )(*args)


_RING = 6
_AHEAD = 4


def _expert_kernel(be_ref, bv_ref, nu_ref, x_ref, bgu_ref, bd_ref, wgu_hbm, wd_hbm, o_ref,
                   act_ref, ring_ref, sem_ref, *, sub, tc):
    rb = pl.program_id(0)
    valid = bv_ref[rb]
    tm, d = x_ref.shape
    d_ff = act_ref.shape[1]
    nfc = d_ff // tc
    nnc = d // tc
    per_block = 2 * nfc + nnc
    total = nu_ref[0] * per_block

    def chunk_copy(src, slot):
        return pltpu.make_async_copy(src, ring_ref.at[slot], sem_ref.at[slot])

    def start(cg):
        @pl.when(cg < total)
        def _():
            blk = cg // per_block
            j = cg - blk * per_block
            e = be_ref[blk]
            slot = lax.rem(cg, _RING)

            @pl.when(j < 2 * nfc)
            def _():
                col = pl.multiple_of((lax.rem(j, 2) * nfc + j // 2) * tc, tc)
                chunk_copy(wgu_hbm.at[e, :, pl.ds(col, tc)], slot).start()

            @pl.when(j >= 2 * nfc)
            def _():
                col = pl.multiple_of((j - 2 * nfc) * tc, tc)
                chunk_copy(wd_hbm.at[e, :, pl.ds(col, tc)], slot).start()

    def wait(cg):
        slot = lax.rem(cg, _RING)
        chunk_copy(wgu_hbm.at[0, :, pl.ds(0, tc)], slot).wait()
        return slot

    @pl.when(rb == 0)
    def _():
        for c in range(_AHEAD):
            start(c)

    base = rb * per_block

    def block(nrows):
        def up(j, carry):
            cg = base + 2 * j
            start(cg + _AHEAD)
            start(cg + 1 + _AHEAD)
            gate_slot = wait(cg)
            up_slot = wait(cg + 1)
            col = pl.ds(pl.multiple_of(j * tc, tc), tc)
            ucol = pl.ds(pl.multiple_of(d_ff + j * tc, tc), tc)
            x = x_ref[0:nrows, :]
            hg = jnp.dot(x, ring_ref[gate_slot].astype(BF16), preferred_element_type=F32) + bgu_ref[0, :, col]
            hu = jnp.dot(x, ring_ref[up_slot].astype(BF16), preferred_element_type=F32) + bgu_ref[0, :, ucol]
            gate = jnp.minimum(hg, SWIGLU_LIMIT)
            upv = jnp.clip(hu, -SWIGLU_LIMIT, SWIGLU_LIMIT)
            act = (upv + 1.0) * gate * jax.nn.sigmoid(SWIGLU_ALPHA * gate)
            act_ref[0:nrows, col] = act.astype(BF16)
            return carry

        lax.fori_loop(0, nfc, up, 0)

        def down(n, carry):
            cg = base + 2 * nfc + n
            start(cg + _AHEAD)
            slot = wait(cg)
            col = pl.ds(pl.multiple_of(n * tc, tc), tc)
            y = jnp.dot(act_ref[0:nrows, :], ring_ref[slot].astype(BF16), preferred_element_type=F32)
            o_ref[0:nrows, col] = (y + bd_ref[0, :, col]).astype(o_ref.dtype)
            return carry

        lax.fori_loop(0, nnc, down, 0)
        if nrows < tm:
            o_ref[nrows:tm, :] = jnp.zeros((tm - nrows, d), o_ref.dtype)

    for nrows in range(sub, tm + 1, sub):
        @pl.when(jnp.logical_and(valid > nrows - sub, valid <= nrows))
        def _(nrows=nrows):
            block(nrows)

    @pl.when(valid == 0)
    def _():
        o_ref[...] = jnp.zeros(o_ref.shape, o_ref.dtype)


def _experts(x_pad, block_e, block_valid, n_used, w_gu, b_gu, w_down, b_down, tm, tc, sub):
    p, d = x_pad.shape
    d_ff = w_down.shape[1]
    assert d_ff == d and _AHEAD <= _RING - 2
    nb = p // tm

    def rbc(rb, nu):
        return jnp.minimum(rb, nu[0] - 1)

    grid_spec = pltpu.PrefetchScalarGridSpec(
        num_scalar_prefetch=3,
        grid=(nb,),
        in_specs=[
            pl.BlockSpec((tm, d), lambda rb, be, bv, nu: (rbc(rb, nu), 0)),
            pl.BlockSpec((1, 1, 2 * d_ff), lambda rb, be, bv, nu: (be[rbc(rb, nu)], 0, 0)),
            pl.BlockSpec((1, 1, d), lambda rb, be, bv, nu: (be[rbc(rb, nu)], 0, 0)),
            pl.BlockSpec(memory_space=pl.ANY),
            pl.BlockSpec(memory_space=pl.ANY),
        ],
        out_specs=pl.BlockSpec((tm, d), lambda rb, be, bv, nu: (rb, 0)),
        scratch_shapes=[
            pltpu.VMEM((tm, d_ff), BF16),
            pltpu.VMEM((_RING, d, tc), F32),
            pltpu.SemaphoreType.DMA((_RING,)),
        ],
    )
    return pl.pallas_call(
        functools.partial(_expert_kernel, sub=sub, tc=tc),
        out_shape=jax.ShapeDtypeStruct((p, d), BF16),
        grid_spec=grid_spec,
        compiler_params=_cparams(("arbitrary",)),
        name="experts",
    )(block_e, block_valid, n_used, x_pad, b_gu, b_down, w_gu, w_down)


def _combine_kernel(y_ref, gate_ref, x1_ref, g_ref, b_ref, o_ref):
    y = y_ref[0].astype(F32) * gate_ref[:, 0:1]
    for k in range(1, TOP_K):
        y = y + y_ref[k].astype(F32) * gate_ref[:, k:k + 1]
    o_ref[...] = _layer_norm(DEEPNORM_ALPHA * x1_ref[...] + y, g_ref[...], b_ref[...])


def _combine(y_rows, gates, x1, ln_g, ln_b, row_block0, tn):
    n, d = x1.shape
    const = lambda i: (0, 0)
    return pl.pallas_call(
        _combine_kernel,
        out_shape=jax.ShapeDtypeStruct((n, d), F32),
        grid=(n // tn,),
        in_specs=[
            pl.BlockSpec((TOP_K, tn, d), lambda i: (0, row_block0 + i, 0)),
            pl.BlockSpec((tn, TOP_K), lambda i: (i, 0)),
            pl.BlockSpec((tn, d), lambda i: (i, 0)),
            pl.BlockSpec((1, d), const),
            pl.BlockSpec((1, d), const),
        ],
        out_specs=pl.BlockSpec((tn, d), lambda i: (i, 0)),
        compiler_params=_cparams(("parallel",)),
        name="combine",
    )(y_rows, gates, x1, ln_g, ln_b)


def _route(idx, rank, sizes, tm):
    n = idx.shape[0]
    nk = n * TOP_K
    nblk = (sizes + tm - 1) // tm
    bends = jnp.cumsum(nblk)
    bstart = bends - nblk
    dest = bstart[idx] * tm + rank
    nb = -(-nk // tm) + N_EXPERTS
    blk = jnp.arange(nb, dtype=jnp.int32)
    block_e = jnp.minimum(jnp.sum(bends[None, :] <= blk[:, None], axis=1), N_EXPERTS - 1).astype(jnp.int32)
    n_used = bends[-1].astype(jnp.int32)
    valid = jnp.clip(sizes[block_e] - (blk - bstart[block_e]) * tm, 0, tm)
    valid = jnp.where(blk < n_used, valid, 0).astype(jnp.int32)
    tok = jnp.broadcast_to(jnp.arange(n, dtype=jnp.int32)[:, None], (n, TOP_K))
    row_tok = (jnp.arange(nb * tm, dtype=jnp.int32) % n).at[dest.reshape(-1)].set(tok.reshape(-1))
    return dest, row_tok, block_e, valid, n_used.reshape(1)


def _pick(n, pref):
    t = min(n, pref)
    while n % t:
        t //= 2
    return t


def kernel(x_prompt, x_sample, cache_k, cache_v, cache_logf, state_conv, state_lru, w_in, b_f, w_conv, b_conv, w_a, b_a, w_i, b_i, lam, w_out, ln1_g, ln1_b, w_router, b_router, w_gu, b_gu, w_down, b_down, ln2_g, ln2_b):
    assert w_in.shape[0] == DEPTH
    bp, tp, d = x_prompt.shape
    bs, ts, _ = x_sample.shape
    past = cache_k.shape[2]
    lru_w = w_conv.shape[-1]
    np_, ns = bp * tp, bs * ts

    win = w_in[0]
    f0 = 3 * FOX_WIDTH
    w_main = jnp.concatenate([win[:, :f0], win[:, f0 + FOX_HEADS:]], axis=1).astype(BF16)
    w_f = jnp.pad(win[:, f0:f0 + FOX_HEADS], ((0, 0), (0, 128 - FOX_HEADS))).astype(BF16)
    bf2 = b_f[0].reshape(1, FOX_HEADS)
    gpb = 4
    bd = w_a.shape[-1]
    eye = jnp.eye(gpb, dtype=F32)

    def blockdiag(w):
        wg = w.reshape(LRU_BLOCKS // gpb, gpb, bd, bd)
        return jnp.einsum("gacd,ab->gacbd", wg, eye).reshape(LRU_BLOCKS // gpb, gpb * bd, gpb * bd)

    w_gate = jnp.concatenate([blockdiag(w_a[0]), blockdiag(w_i[0])], axis=-1).astype(BF16)
    w_top = w_out[0, :FOX_WIDTH].astype(BF16)
    w_bot = w_out[0, FOX_WIDTH:].astype(BF16)
    wr = w_router[0]
    wr_hi = wr.astype(BF16)
    wr_cat = jnp.concatenate([wr_hi, (wr - wr_hi.astype(F32)).astype(BF16)], axis=1)
    row = lambda a: a.reshape(1, -1)

    n_all = np_ + ns
    tm_mix = _pick(ns, 512)

    def mixers(x, batch, seq, fox_fn, conv0, h0, cnt0, x1b_buf, row0):
        n = batch * seq
        q, k, v, kb, vb, xr, yg, logf = _in_proj(x.reshape(n, d), w_main, w_f, bf2, _pick(n, 512))
        fox_o = fox_fn(q, kb, vb, logf.reshape(batch, seq, FOX_HEADS))
        lru_o, conv_new, h_last = _lru(xr, yg, conv0, h0, w_conv[0], row(b_conv[0]), w_gate,
                                       row(b_a[0]), row(b_i[0]), row(lam[0]), batch, seq, _pick(seq, 256))
        routed = _mix_norm(fox_o, lru_o, x.reshape(n, d), w_top, w_bot, row(ln1_g[0]), row(ln1_b[0]),
                           wr_cat, wr_hi, row(b_router[0]), cnt0, tm_mix, n_all, row0 // tm_mix, x1b_buf)
        state = (k.reshape(1, batch, seq, FOX_HEADS, FOX_HEAD_DIM), v.reshape(1, batch, seq, FOX_HEADS, FOX_HEAD_DIM),
                 logf.reshape(1, batch, seq, FOX_HEADS), conv_new[None], h_last.reshape(1, batch, lru_w))
        return routed, state

    def fox_p(q, kb, vb, logf):
        c = _cumsum_time(logf, jnp.zeros((bp, 1, FOX_HEADS), F32))
        return _fox_prompt(q, kb, vb, c * LOG2E, bp, tp, _pick(tp, 512))

    def fox_s(q, kb, vb, logf):
        c_past = _cumsum_time(cache_logf[0], jnp.zeros((bs, 1, FOX_HEADS), F32))
        c_new = _cumsum_time(logf, c_past[:, -1:, :])
        c_all = jnp.concatenate([c_past, c_new], axis=1) * LOG2E
        cache_rows = lambda c: c[0].reshape(bs, past * FOX_HEADS, FOX_HEAD_DIM)
        return _fox_sample(q, kb, vb, cache_rows(cache_k), cache_rows(cache_v), c_all, bs, ts, _pick(past, 512))

    (x1p, x1b, idxp, gp, rankp, cnt_p), state_p = mixers(
        x_prompt, bp, tp, fox_p, jnp.zeros((bp, CONV_WIDTH - 1, lru_w), F32), jnp.zeros((bp, 1, lru_w), F32),
        jnp.zeros((1, N_EXPERTS), F32), None, 0)
    (x1s, x1b, idxs, gs, ranks, cnt_all), state_s = mixers(
        x_sample, bs, ts, fox_s, state_conv[0], state_lru[0].reshape(bs, 1, lru_w), cnt_p, x1b, np_)

    tm_e, tc_e, sub_e = 1024, 512, 256
    idx = jnp.concatenate([idxp, idxs], axis=0)
    rank = jnp.concatenate([rankp, ranks], axis=0)
    dest, row_tok, block_e, block_valid, n_used = _route(idx, rank, cnt_all[0].astype(jnp.int32), tm_e)
    x_pad = x1b[row_tok]
    y_pad = _experts(x_pad, block_e, block_valid, n_used, w_gu[0], b_gu[0].reshape(N_EXPERTS, 1, -1),
                     w_down[0], b_down[0].reshape(N_EXPERTS, 1, -1), tm_e, tc_e, sub_e)
    y_rows = y_pad[dest.T]

    tn = _pick(ns, 512)
    yp = _combine(y_rows, gp, x1p, row(ln2_g[0]), row(ln2_b[0]), 0, tn)
    ys = _combine(y_rows, gs, x1s, row(ln2_g[0]), row(ln2_b[0]), np_ // tn, tn)
    return (yp.reshape(bp, tp, d), ys.reshape(bs, ts, d)) + state_p + state_s
```

```python
import functools

import jax
import jax.numpy as jnp
from jax import lax
from jax.experimental import pallas as pl
from jax.experimental.pallas import tpu as pltpu

F32 = jnp.float32
BF16 = jnp.bfloat16

FOX_HEADS = 8
FOX_HEAD_DIM = 128
FOX_WIDTH = FOX_HEADS * FOX_HEAD_DIM
LRU_BLOCKS = 16
CONV_WIDTH = 4
LRU_C = 8.0
N_EXPERTS = 32
TOP_K = 4
SWIGLU_LIMIT = 7.0
SWIGLU_ALPHA = 1.702
LN_EPS = 1e-5
DEPTH = 1
DEEPNORM_ALPHA = (2.0 * DEPTH) ** 0.25
LOG2E = 1.4426950408889634
Q_SCALE = FOX_HEAD_DIM ** -0.5 * LOG2E

VMEM_LIMIT = 56 * 1024 * 1024


def _cparams(sem):
    return pltpu.CompilerParams(dimension_semantics=sem, vmem_limit_bytes=VMEM_LIMIT)


def _log_sigmoid(x):
    return jnp.minimum(x, 0.0) - jnp.log1p(jnp.exp(-jnp.abs(x)))


def _layer_norm(z, g, b):
    mu = jnp.mean(z, axis=-1, keepdims=True)
    zc = z - mu
    var = jnp.mean(zc * zc, axis=-1, keepdims=True)
    return zc * lax.rsqrt(var + LN_EPS) * g + b


def _in_proj_kernel(x_ref, w_ref, wf_ref, bf_ref, q_ref, k_ref, v_ref, kb_ref, vb_ref, xr_ref, yg_ref, lf_ref,
                    xb_ref):
    j = pl.program_id(1)

    @pl.when(j == 0)
    def _():
        xb_ref[...] = x_ref[...].astype(BF16)
        zf = jnp.dot(xb_ref[...], wf_ref[...], preferred_element_type=F32)
        lf_ref[...] = _log_sigmoid(zf[:, :FOX_HEADS] + bf_ref[...])

    z = jnp.dot(xb_ref[...], w_ref[...], preferred_element_type=F32)

    @pl.when(j == 0)
    def _():
        q_ref[...] = (z * Q_SCALE).astype(BF16)

    @pl.when(j == 1)
    def _():
        k_ref[...] = z
        kb_ref[...] = z.astype(BF16)

    @pl.when(j == 2)
    def _():
        v_ref[...] = z
        vb_ref[...] = z.astype(BF16)

    @pl.when(j == 3)
    def _():
        xr_ref[...] = z

    @pl.when(j == 4)
    def _():
        yg_ref[...] = z


def _in_proj(x2d, w_main, w_f, b_f, tm):
    n, d = x2d.shape
    wcol = FOX_WIDTH
    row = lambda i, j: (i, 0)
    dts = (BF16, F32, F32, BF16, BF16, F32, F32)
    outs = [jax.ShapeDtypeStruct((n, wcol), dt) for dt in dts]
    outs.append(jax.ShapeDtypeStruct((n, FOX_HEADS), F32))
    return pl.pallas_call(
        _in_proj_kernel,
        out_shape=outs,
        grid=(n // tm, 5),
        in_specs=[
            pl.BlockSpec((tm, d), row),
            pl.BlockSpec((d, wcol), lambda i, j: (0, j)),
            pl.BlockSpec((d, 128), lambda i, j: (0, 0)),
            pl.BlockSpec((1, FOX_HEADS), lambda i, j: (0, 0)),
        ],
        out_specs=[pl.BlockSpec((tm, wcol), row)] * len(dts) + [pl.BlockSpec((tm, FOX_HEADS), row)],
        scratch_shapes=[pltpu.VMEM((tm, d), BF16)],
        compiler_params=_cparams(("parallel", "arbitrary")),
        name="in_proj",
    )(x2d, w_main, w_f, b_f)


def _cumsum_kernel(s_ref, c0_ref, o_ref, tri_ref):
    t = s_ref.shape[1]

    @pl.when(pl.program_id(0) == 0)
    def _():
        row = lax.broadcasted_iota(jnp.int32, (t, t), 0)
        col = lax.broadcasted_iota(jnp.int32, (t, t), 1)
        tri_ref[...] = jnp.where(col <= row, 1.0, 0.0).astype(BF16)

    c3 = jnp.dot(tri_ref[...], s_ref[0], preferred_element_type=F32)
    h = FOX_HEADS
    o_ref[0] = c3[:, 0:h] + c3[:, h:2 * h] + c3[:, 2 * h:3 * h] + c0_ref[0]


def _cumsum_time(x, c0):
    b, t, h = x.shape
    hi = x.astype(BF16)
    r1 = x - hi.astype(F32)
    mid = r1.astype(BF16)
    lo = (r1 - mid.astype(F32)).astype(BF16)
    pieces = jnp.concatenate([hi, mid, lo, jnp.zeros((b, t, 128 - 3 * h), BF16)], axis=-1)
    return pl.pallas_call(
        _cumsum_kernel,
        out_shape=jax.ShapeDtypeStruct((b, t, h), F32),
        grid=(b,),
        in_specs=[
            pl.BlockSpec((1, t, 128), lambda i: (i, 0, 0)),
            pl.BlockSpec((1, 1, h), lambda i: (i, 0, 0)),
        ],
        out_specs=pl.BlockSpec((1, t, h), lambda i: (i, 0, 0)),
        scratch_shapes=[pltpu.VMEM((t, t), BF16)],
        compiler_params=_cparams(("arbitrary",)),
        name="cumsum_time",
    )(pieces, c0)


_NT = (((1,), (1,)), ((), ()))


def _attn_init(m_ref, l_ref, acc_ref):
    m_ref[...] = jnp.full(m_ref.shape, -jnp.inf, F32)
    l_ref[...] = jnp.zeros(l_ref.shape, F32)
    acc_ref[...] = jnp.zeros(acc_ref.shape, F32)


def _fox_prompt_kernel(qi_ref, ki_ref, q_ref, k_ref, v_ref, ck_ref, o_ref,
                       m_ref, l_ref, acc_ref, s0_ref, s1_ref, p0_ref, p1_ref, a0_ref, a1_ref, *, rs):
    pair = pl.program_id(1)
    qi = qi_ref[pair]
    ki = ki_ref[pair]
    tq = q_ref.shape[0]
    tk = k_ref.shape[0]
    lanes = FOX_HEAD_DIM

    @pl.when(ki == 0)
    def _():
        _attn_init(m_ref, l_ref, acc_ref)

    def head_lanes(h):
        return pl.ds(pl.multiple_of(h * lanes, lanes), lanes)

    def scores(h, s_ref):
        sl = head_lanes(h)
        s_ref[...] = lax.dot_general(q_ref[:, sl], k_ref[:, sl], _NT, preferred_element_type=F32)

    def weighted_values(h, p_ref, a_ref):
        sl = head_lanes(h)
        pv = jnp.dot(p_ref[...], v_ref[:, sl], preferred_element_type=F32)
        acc_ref[:, sl] = a_ref[...] * acc_ref[:, sl] + pv

    def softmax(h, s_ref, p_ref, a_ref, masked):
        ck = ck_ref[0, h]
        for r in range(tq // rs):
            r0 = r * rs
            rsl = slice(r0, r0 + rs)
            chunks = []
            for c in range(tk // lanes):
                if masked and c * lanes > r0 + rs - 1:
                    continue
                s = s_ref[rsl, c * lanes:(c + 1) * lanes] - ck[:, c * lanes:(c + 1) * lanes]
                if masked and (c + 1) * lanes - 1 > r0:
                    row = r0 + lax.broadcasted_iota(jnp.int32, (rs, lanes), 0)
                    col = c * lanes + lax.broadcasted_iota(jnp.int32, (rs, lanes), 1)
                    s = jnp.where(col <= row, s, -jnp.inf)
                chunks.append((c, s))
            mc = chunks[0][1]
            for _, s in chunks[1:]:
                mc = jnp.maximum(mc, s)
            m_prev = m_ref[h, rsl, :]
            m_new = jnp.maximum(m_prev, jnp.max(mc, axis=-1, keepdims=True))
            alpha = jnp.exp2(m_prev - m_new)
            psum = None
            for c, s in chunks:
                p = jnp.exp2(s - m_new)
                p_ref[rsl, c * lanes:(c + 1) * lanes] = p.astype(BF16)
                psum = p if psum is None else psum + p
            for c in range(chunks[-1][0] + 1, tk // lanes):
                p_ref[rsl, c * lanes:(c + 1) * lanes] = jnp.zeros((rs, lanes), BF16)
            l_ref[h, rsl, :] = alpha * l_ref[h, rsl, :] + psum
            m_ref[h, rsl, :] = m_new
            a_ref[rsl, :] = alpha

    def all_heads(masked):
        scores(0, s0_ref)
        p1_ref[...] = jnp.zeros(p1_ref.shape, BF16)
        a1_ref[...] = jnp.ones(a1_ref.shape, F32)

        def two_heads(i, carry):
            h0 = 2 * i
            h1 = h0 + 1
            scores(h1, s1_ref)
            softmax(h0, s0_ref, p0_ref, a0_ref, masked)
            weighted_values(lax.rem(h0 + FOX_HEADS - 1, FOX_HEADS), p1_ref, a1_ref)
            scores(lax.rem(h0 + 2, FOX_HEADS), s0_ref)
            softmax(h1, s1_ref, p1_ref, a1_ref, masked)
            weighted_values(h0, p0_ref, a0_ref)
            return carry

        lax.fori_loop(0, FOX_HEADS // 2, two_heads, 0)
        weighted_values(FOX_HEADS - 1, p1_ref, a1_ref)

    @pl.when(ki < qi)
    def _():
        all_heads(False)

    @pl.when(ki == qi)
    def _():
        all_heads(True)
        for h in range(FOX_HEADS):
            sl = slice(h * lanes, (h + 1) * lanes)
            l_tot = jnp.sum(l_ref[h], axis=-1, keepdims=True)
            o_ref[:, sl] = (acc_ref[:, sl] / l_tot).astype(o_ref.dtype)


def _fox_prompt(q, kb, vb, c2, batch, seq, tq):
    nq = seq // tq
    width = q.shape[1]
    ck = jnp.transpose(c2, (0, 2, 1))[:, :, None, :]
    pairs = [(i, j) for i in range(nq) for j in range(i + 1)]
    qi_tab = jnp.asarray([p[0] for p in pairs], jnp.int32)
    ki_tab = jnp.asarray([p[1] for p in pairs], jnp.int32)
    q_map = lambda b, p, qt, kt: (b * nq + qt[p], 0)
    k_map = lambda b, p, qt, kt: (b * nq + kt[p], 0)
    grid_spec = pltpu.PrefetchScalarGridSpec(
        num_scalar_prefetch=2,
        grid=(batch, len(pairs)),
        in_specs=[
            pl.BlockSpec((tq, width), q_map),
            pl.BlockSpec((tq, width), k_map),
            pl.BlockSpec((tq, width), k_map),
            pl.BlockSpec((1, FOX_HEADS, 1, tq), lambda b, p, qt, kt: (b, 0, 0, kt[p])),
        ],
        out_specs=pl.BlockSpec((tq, width), q_map),
        scratch_shapes=[
            pltpu.VMEM((FOX_HEADS, tq, FOX_HEAD_DIM), F32),
            pltpu.VMEM((FOX_HEADS, tq, FOX_HEAD_DIM), F32),
            pltpu.VMEM((tq, width), F32),
            pltpu.VMEM((tq, tq), F32),
            pltpu.VMEM((tq, tq), F32),
            pltpu.VMEM((tq, tq), BF16),
            pltpu.VMEM((tq, tq), BF16),
            pltpu.VMEM((tq, FOX_HEAD_DIM), F32),
            pltpu.VMEM((tq, FOX_HEAD_DIM), F32),
        ],
    )
    return pl.pallas_call(
        functools.partial(_fox_prompt_kernel, rs=min(64, tq)),
        out_shape=jax.ShapeDtypeStruct(q.shape, BF16),
        grid_spec=grid_spec,
        compiler_params=_cparams(("parallel", "arbitrary")),
        name="fox_prompt",
    )(qi_tab, ki_tab, q, kb, vb, ck)


def _fox_sample_kernel(q_ref, kc_ref, vc_ref, kn_ref, vn_ref, ckc_ref, ckn_ref, o_ref, m_ref, l_ref, acc_ref):
    j = pl.program_id(1)
    tq = q_ref.shape[0]
    tk = kc_ref.shape[1] // FOX_HEADS
    lanes = FOX_HEAD_DIM

    @pl.when(j == 0)
    def _():
        _attn_init(m_ref, l_ref, acc_ref)

    heads = range(FOX_HEADS)
    head_lanes = [slice(h * lanes, (h + 1) * lanes) for h in heads]

    def update(ks, vs, cks, masked):
        ss = [lax.dot_general(q_ref[:, head_lanes[h]], ks[h], _NT, preferred_element_type=F32) - cks[h]
              for h in heads]
        if masked:
            row = lax.broadcasted_iota(jnp.int32, ss[0].shape, 0)
            col = lax.broadcasted_iota(jnp.int32, ss[0].shape, 1)
            ss = [jnp.where(col <= row, s, -jnp.inf) for s in ss]
        m_prev = [m_ref[h] for h in heads]
        m_new = [jnp.maximum(m_prev[h], jnp.max(ss[h], axis=-1, keepdims=True)) for h in heads]
        alpha = [jnp.exp2(m_prev[h] - m_new[h]) for h in heads]
        ps = [jnp.exp2(ss[h] - m_new[h][:, 0:1]) for h in heads]
        pvs = [jnp.dot(ps[h].astype(BF16), vs[h], preferred_element_type=F32) for h in heads]
        for h in heads:
            l_ref[h] = alpha[h] * l_ref[h] + jnp.sum(ps[h], axis=-1, keepdims=True)
            acc_ref[:, head_lanes[h]] = alpha[h] * acc_ref[:, head_lanes[h]] + pvs[h]
            m_ref[h] = m_new[h]

    update([kc_ref[0, pl.ds(h, tk, stride=FOX_HEADS), :].astype(BF16) for h in heads],
           [vc_ref[0, pl.ds(h, tk, stride=FOX_HEADS), :].astype(BF16) for h in heads],
           [ckc_ref[0, h] for h in heads], False)

    @pl.when(j == pl.num_programs(1) - 1)
    def _():
        update([kn_ref[:, sl] for sl in head_lanes], [vn_ref[:, sl] for sl in head_lanes],
               [ckn_ref[0, h] for h in heads], True)
        for h in heads:
            sl = head_lanes[h]
            o_ref[:, sl] = (acc_ref[:, sl] / l_ref[h]).astype(o_ref.dtype)


def _fox_sample(q, kb, vb, cache_k, cache_v, c2_all, batch, seq, tk):
    past = cache_k.shape[1] // FOX_HEADS
    width = q.shape[1]
    ct = jnp.transpose(c2_all, (0, 2, 1))
    ckc = ct[:, :, None, :past]
    ckn = ct[:, :, None, past:]
    new_map = lambda b, j: (b, 0)
    cache_spec = pl.BlockSpec((1, tk * FOX_HEADS, FOX_HEAD_DIM), lambda b, j: (b, j, 0))
    return pl.pallas_call(
        _fox_sample_kernel,
        out_shape=jax.ShapeDtypeStruct(q.shape, BF16),
        grid=(batch, past // tk),
        in_specs=[
            pl.BlockSpec((seq, width), new_map),
            cache_spec,
            cache_spec,
            pl.BlockSpec((seq, width), new_map),
            pl.BlockSpec((seq, width), new_map),
            pl.BlockSpec((1, FOX_HEADS, 1, tk), lambda b, j: (b, 0, 0, j)),
            pl.BlockSpec((1, FOX_HEADS, 1, seq), lambda b, j: (b, 0, 0, 0)),
        ],
        out_specs=pl.BlockSpec((seq, width), new_map),
        scratch_shapes=[
            pltpu.VMEM((FOX_HEADS, seq, FOX_HEAD_DIM), F32),
            pltpu.VMEM((FOX_HEADS, seq, FOX_HEAD_DIM), F32),
            pltpu.VMEM((seq, width), F32),
        ],
        compiler_params=_cparams(("parallel", "arbitrary")),
        name="fox_sample",
    )(q, cache_k, cache_v, kb, vb, ckc, ckn)


_HALO = 8


def _lru_kernel(xr_ref, yg_ref, conv0_ref, h0_ref, wc_ref, bc_ref, wg_ref, ba_ref, bi_ref, lam_ref,
                o_ref, convo_ref, hlast_ref, xp_ref, a_ref, b_ref, h_ref):
    t = pl.program_id(1)
    tt, width = xr_ref.shape
    tail = CONV_WIDTH - 1
    lo = _HALO - tail

    @pl.when(t == 0)
    def _():
        xp_ref[lo:_HALO, :] = conv0_ref[0]
        h_ref[...] = h0_ref[0]

    xp_ref[_HALO:_HALO + tt, :] = xr_ref[...]
    xc = bc_ref[...] + xp_ref[lo:lo + tt, :] * wc_ref[0:1, :]
    for j in range(1, CONV_WIDTH):
        xc = xc + xp_ref[lo + j:lo + j + tt, :] * wc_ref[j:j + 1, :]
    new_tail = xp_ref[lo + tt:_HALO + tt, :]
    xp_ref[lo:_HALO, :] = new_tail

    xcb = xc.astype(BF16)
    gw = wg_ref.shape[1]
    lam = lam_ref[...]
    neg_sp = -(jnp.maximum(-lam, 0.0) + jnp.log1p(jnp.exp(-jnp.abs(lam))))
    for g in range(width // gw):
        sl = slice(g * gw, (g + 1) * gw)
        z = jnp.dot(xcb[:, sl], wg_ref[g], preferred_element_type=F32)
        r = jax.nn.sigmoid(z[:, :gw] + ba_ref[:, sl])
        i = jax.nn.sigmoid(z[:, gw:] + bi_ref[:, sl])
        log_a = LRU_C * r * neg_sp[:, sl]
        a = jnp.exp(log_a)
        a_ref[:, sl] = a
        b_ref[:, sl] = jnp.sqrt(-jnp.tanh(log_a) * (a * a + 1.0)) * i * xc[:, sl]

    def step(s, h):
        row = pl.ds(s, 1)
        h = a_ref[row, :] * h + b_ref[row, :]
        b_ref[row, :] = h
        return h

    h = lax.fori_loop(0, tt, step, h_ref[...], unroll=8)
    h_ref[...] = h
    o_ref[...] = (b_ref[...] * jax.nn.gelu(yg_ref[...])).astype(o_ref.dtype)

    @pl.when(t == pl.num_programs(1) - 1)
    def _():
        convo_ref[0] = new_tail
        hlast_ref[0] = h


def _lru(xr, yg, conv0, h0, w_conv, b_conv, w_gate, b_a, b_i, lam, batch, seq, tt):
    width = xr.shape[1]
    nt = seq // tt
    gw = w_gate.shape[1]
    rows = lambda b, t: (b * nt + t, 0)
    const2 = lambda b, t: (0, 0)
    per_b = lambda b, t: (b, 0, 0)
    tail = CONV_WIDTH - 1
    return pl.pallas_call(
        _lru_kernel,
        out_shape=[
            jax.ShapeDtypeStruct((batch * seq, width), BF16),
            jax.ShapeDtypeStruct((batch, tail, width), F32),
            jax.ShapeDtypeStruct((batch, 1, width), F32),
        ],
        grid=(batch, nt),
        in_specs=[
            pl.BlockSpec((tt, width), rows),
            pl.BlockSpec((tt, width), rows),
            pl.BlockSpec((1, tail, width), per_b),
            pl.BlockSpec((1, 1, width), per_b),
            pl.BlockSpec((CONV_WIDTH, width), const2),
            pl.BlockSpec((1, width), const2),
            pl.BlockSpec((width // gw, gw, 2 * gw), lambda b, t: (0, 0, 0)),
            pl.BlockSpec((1, width), const2),
            pl.BlockSpec((1, width), const2),
            pl.BlockSpec((1, width), const2),
        ],
        out_specs=[
            pl.BlockSpec((tt, width), rows),
            pl.BlockSpec((1, tail, width), per_b),
            pl.BlockSpec((1, 1, width), per_b),
        ],
        scratch_shapes=[
            pltpu.VMEM((_HALO + tt, width), F32),
            pltpu.VMEM((tt, width), F32),
            pltpu.VMEM((tt, width), F32),
            pltpu.VMEM((1, width), F32),
        ],
        compiler_params=_cparams(("parallel", "arbitrary")),
        name="lru",
    )(xr, yg, conv0, h0, w_conv, b_conv, w_gate, b_a, b_i, lam)


def _mix_norm_kernel(*refs, aliased, n_tiles):
    ins = refs[:11]
    outs = refs[12:] if aliased else refs[11:]
    live = pl.program_id(0) < n_tiles

    @pl.when(live)
    def _():
        _mix_norm_body(*ins, *outs)

    @pl.when(jnp.logical_not(live))
    def _():
        outs[1][...] = jnp.zeros(outs[1].shape, outs[1].dtype)


def _mix_norm_body(fox_ref, lru_ref, x_ref, wt_ref, wb_ref, g_ref, b_ref, wrc_ref, wrh_ref, br_ref, cnt0_ref,
                   x1_ref, x1b_ref, idx_ref, gate_ref, rank_ref, cnt_ref):
    tm = x_ref.shape[0]

    mix = jnp.dot(fox_ref[...], wt_ref[...], preferred_element_type=F32)
    mix = mix + jnp.dot(lru_ref[...], wb_ref[...], preferred_element_type=F32)
    x1 = _layer_norm(DEEPNORM_ALPHA * x_ref[...] + mix, g_ref[...], b_ref[...])
    x1_ref[...] = x1
    x1_hi = x1.astype(BF16)
    x1b_ref[...] = x1_hi
    x1_lo = (x1 - x1_hi.astype(F32)).astype(BF16)
    lg2 = jnp.dot(x1_hi, wrc_ref[...], preferred_element_type=F32)
    lg = lg2[:, :N_EXPERTS] + lg2[:, N_EXPERTS:] + jnp.dot(x1_lo, wrh_ref[...], preferred_element_type=F32)
    lg = lg + br_ref[...]
    lanes = lax.broadcasted_iota(jnp.int32, lg.shape, 1)
    vals, picks = [], []
    for k in range(TOP_K):
        m = jnp.max(lg, axis=-1, keepdims=True)
        ix = jnp.min(jnp.where(lg == m, lanes, N_EXPERTS), axis=-1, keepdims=True)
        idx_ref[:, k:k + 1] = ix
        vals.append(m)
        picks.append(lanes == ix)
        lg = jnp.where(picks[-1], -jnp.inf, lg)
    es = [jnp.exp(v - vals[0]) for v in vals]
    denom = es[0] + es[1] + es[2] + es[3]
    for k in range(TOP_K):
        gate_ref[:, k:k + 1] = es[k] / denom

    @pl.when(pl.program_id(0) == 0)
    def _():
        cnt_ref[...] = cnt0_ref[...]

    sel = jnp.where(picks[0], 1.0, 0.0)
    for k in range(1, TOP_K):
        sel = sel + jnp.where(picks[k], 1.0, 0.0)
    earlier = lax.broadcasted_iota(jnp.int32, (tm, tm), 1) < lax.broadcasted_iota(jnp.int32, (tm, tm), 0)
    tri = jnp.where(earlier, 1.0, 0.0).astype(BF16)
    before = jnp.dot(tri, sel.astype(BF16), preferred_element_type=F32) + cnt_ref[...]
    for k in range(TOP_K):
        rank_ref[:, k:k + 1] = jnp.sum(jnp.where(picks[k], before, 0.0), axis=-1, keepdims=True).astype(jnp.int32)
    cnt_ref[...] += jnp.sum(sel, axis=0, keepdims=True)


def _mix_norm(fox_o, lru_o, x2d, w_top, w_bot, ln_g, ln_b, wr_cat, wr_hi, b_router, cnt0, tm, n_all, row_block0,
              x1b_buf=None):
    n, d = x2d.shape
    half = fox_o.shape[1]
    n_tiles = n // tm
    aliased = x1b_buf is not None
    steps = n_tiles if aliased else n_all // tm - row_block0
    row = lambda i: (jnp.minimum(i, n_tiles - 1), 0)
    const = lambda i: (0, 0)
    in_specs = [
        pl.BlockSpec((tm, half), row),
        pl.BlockSpec((tm, half), row),
        pl.BlockSpec((tm, d), row),
        pl.BlockSpec((half, d), const),
        pl.BlockSpec((half, d), const),
        pl.BlockSpec((1, d), const),
        pl.BlockSpec((1, d), const),
        pl.BlockSpec((d, 2 * N_EXPERTS), const),
        pl.BlockSpec((d, N_EXPERTS), const),
        pl.BlockSpec((1, N_EXPERTS), const),
        pl.BlockSpec((1, N_EXPERTS), const),
    ]
    args = [fox_o, lru_o, x2d, w_top, w_bot, ln_g, ln_b, wr_cat, wr_hi, b_router, cnt0]
    if aliased:
        in_specs.append(pl.BlockSpec(memory_space=pl.ANY))
        args.append(x1b_buf)
    return pl.pallas_call(
        functools.partial(_mix_norm_kernel, aliased=aliased, n_tiles=n_tiles),
        out_shape=[
            jax.ShapeDtypeStruct((n, d), F32),
            jax.ShapeDtypeStruct((n_all, d), BF16),
            jax.ShapeDtypeStruct((n, TOP_K), jnp.int32),
            jax.ShapeDtypeStruct((n, TOP_K), F32),
            jax.ShapeDtypeStruct((n, TOP_K), jnp.int32),
            jax.ShapeDtypeStruct((1, N_EXPERTS), F32),
        ],
        grid=(steps,),
        in_specs=in_specs,
        out_specs=[
            pl.BlockSpec((tm, d), row),
            pl.BlockSpec((tm, d), lambda i: (row_block0 + i, 0)),
            pl.BlockSpec((tm, TOP_K), row),
            pl.BlockSpec((tm, TOP_K), row),
            pl.BlockSpec((tm, TOP_K), row),
            pl.BlockSpec((1, N_EXPERTS), const),
        ],
        input_output_aliases={11: 1} if aliased else {},
        compiler_params=_cparams(("arbitrary",)),
        name="mix_norm",
    )(*args)


_RING = 6
_AHEAD = 4


def _expert_kernel(be_ref, bv_ref, nu_ref, x_ref, bgu_ref, bd_ref, wgu_hbm, wd_hbm, o_ref,
                   act_ref, ring_ref, sem_ref, *, sub, tc):
    rb = pl.program_id(0)
    valid = bv_ref[rb]
    tm, d = x_ref.shape
    d_ff = act_ref.shape[1]
    nfc = d_ff // tc
    nnc = d // tc
    per_block = 2 * nfc + nnc
    total = nu_ref[0] * per_block

    def chunk_copy(src, slot):
        return pltpu.make_async_copy(src, ring_ref.at[slot], sem_ref.at[slot])

    def start(cg):
        @pl.when(cg < total)
        def _():
            blk = cg // per_block
            j = cg - blk * per_block
            e = be_ref[blk]
            slot = lax.rem(cg, _RING)

            @pl.when(j < 2 * nfc)
            def _():
                col = pl.multiple_of((lax.rem(j, 2) * nfc + j // 2) * tc, tc)
                chunk_copy(wgu_hbm.at[e, :, pl.ds(col, tc)], slot).start()

            @pl.when(j >= 2 * nfc)
            def _():
                col = pl.multiple_of((j - 2 * nfc) * tc, tc)
                chunk_copy(wd_hbm.at[e, :, pl.ds(col, tc)], slot).start()

    def wait(cg):
        slot = lax.rem(cg, _RING)
        chunk_copy(wgu_hbm.at[0, :, pl.ds(0, tc)], slot).wait()
        return slot

    @pl.when(rb == 0)
    def _():
        for c in range(_AHEAD):
            start(c)

    base = rb * per_block

    def block(nrows):
        def up(j, carry):
            cg = base + 2 * j
            start(cg + _AHEAD)
            start(cg + 1 + _AHEAD)
            gate_slot = wait(cg)
            up_slot = wait(cg + 1)
            col = pl.ds(pl.multiple_of(j * tc, tc), tc)
            ucol = pl.ds(pl.multiple_of(d_ff + j * tc, tc), tc)
            x = x_ref[0:nrows, :]
            hg = jnp.dot(x, ring_ref[gate_slot].astype(BF16), preferred_element_type=F32) + bgu_ref[0, :, col]
            hu = jnp.dot(x, ring_ref[up_slot].astype(BF16), preferred_element_type=F32) + bgu_ref[0, :, ucol]
            gate = jnp.minimum(hg, SWIGLU_LIMIT)
            upv = jnp.clip(hu, -SWIGLU_LIMIT, SWIGLU_LIMIT)
            act = (upv + 1.0) * gate * jax.nn.sigmoid(SWIGLU_ALPHA * gate)
            act_ref[0:nrows, col] = act.astype(BF16)
            return carry

        lax.fori_loop(0, nfc, up, 0)

        def down(n, carry):
            cg = base + 2 * nfc + n
            start(cg + _AHEAD)
            slot = wait(cg)
            col = pl.ds(pl.multiple_of(n * tc, tc), tc)
            y = jnp.dot(act_ref[0:nrows, :], ring_ref[slot].astype(BF16), preferred_element_type=F32)
            o_ref[0:nrows, col] = (y + bd_ref[0, :, col]).astype(o_ref.dtype)
            return carry

        lax.fori_loop(0, nnc, down, 0)
        if nrows < tm:
            o_ref[nrows:tm, :] = jnp.zeros((tm - nrows, d), o_ref.dtype)

    for nrows in range(sub, tm + 1, sub):
        @pl.when(jnp.logical_and(valid > nrows - sub, valid <= nrows))
        def _(nrows=nrows):
            block(nrows)

    @pl.when(valid == 0)
    def _():
        o_ref[...] = jnp.zeros(o_ref.shape, o_ref.dtype)


def _experts(x_pad, block_e, block_valid, n_used, w_gu, b_gu, w_down, b_down, tm, tc, sub):
    p, d = x_pad.shape
    d_ff = w_down.shape[1]
    assert d_ff == d and _AHEAD <= _RING - 2
    nb = p // tm

    def rbc(rb, nu):
        return jnp.minimum(rb, nu[0] - 1)

    grid_spec = pltpu.PrefetchScalarGridSpec(
        num_scalar_prefetch=3,
        grid=(nb,),
        in_specs=[
            pl.BlockSpec((tm, d), lambda rb, be, bv, nu: (rbc(rb, nu), 0)),
            pl.BlockSpec((1, 1, 2 * d_ff), lambda rb, be, bv, nu: (be[rbc(rb, nu)], 0, 0)),
            pl.BlockSpec((1, 1, d), lambda rb, be, bv, nu: (be[rbc(rb, nu)], 0, 0)),
            pl.BlockSpec(memory_space=pl.ANY),
            pl.BlockSpec(memory_space=pl.ANY),
        ],
        out_specs=pl.BlockSpec((tm, d), lambda rb, be, bv, nu: (rb, 0)),
        scratch_shapes=[
            pltpu.VMEM((tm, d_ff), BF16),
            pltpu.VMEM((_RING, d, tc), F32),
            pltpu.SemaphoreType.DMA((_RING,)),
        ],
    )
    return pl.pallas_call(
        functools.partial(_expert_kernel, sub=sub, tc=tc),
        out_shape=jax.ShapeDtypeStruct((p, d), BF16),
        grid_spec=grid_spec,
        compiler_params=_cparams(("arbitrary",)),
        name="experts",
    )(block_e, block_valid, n_used, x_pad, b_gu, b_down, w_gu, w_down)


def _combine_kernel(y_ref, gate_ref, x1_ref, g_ref, b_ref, o_ref):
    y = y_ref[0].astype(F32) * gate_ref[:, 0:1]
    for k in range(1, TOP_K):
        y = y + y_ref[k].astype(F32) * gate_ref[:, k:k + 1]
    o_ref[...] = _layer_norm(DEEPNORM_ALPHA * x1_ref[...] + y, g_ref[...], b_ref[...])


def _combine(y_rows, gates, x1, ln_g, ln_b, row_block0, tn):
    n, d = x1.shape
    const = lambda i: (0, 0)
    return pl.pallas_call(
        _combine_kernel,
        out_shape=jax.ShapeDtypeStruct((n, d), F32),
        grid=(n // tn,),
        in_specs=[
            pl.BlockSpec((TOP_K, tn, d), lambda i: (0, row_block0 + i, 0)),
            pl.BlockSpec((tn, TOP_K), lambda i: (i, 0)),
            pl.BlockSpec((tn, d), lambda i: (i, 0)),
            pl.BlockSpec((1, d), const),
            pl.BlockSpec((1, d), const),
        ],
        out_specs=pl.BlockSpec((tn, d), lambda i: (i, 0)),
        compiler_params=_cparams(("parallel",)),
        name="combine",
    )(y_rows, gates, x1, ln_g, ln_b)


def _route(idx, rank, sizes, tm):
    n = idx.shape[0]
    nk = n * TOP_K
    nblk = (sizes + tm - 1) // tm
    bends = jnp.cumsum(nblk)
    bstart = bends - nblk
    dest = bstart[idx] * tm + rank
    nb = -(-nk // tm) + N_EXPERTS
    blk = jnp.arange(nb, dtype=jnp.int32)
    block_e = jnp.minimum(jnp.sum(bends[None, :] <= blk[:, None], axis=1), N_EXPERTS - 1).astype(jnp.int32)
    n_used = bends[-1].astype(jnp.int32)
    valid = jnp.clip(sizes[block_e] - (blk - bstart[block_e]) * tm, 0, tm)
    valid = jnp.where(blk < n_used, valid, 0).astype(jnp.int32)
    tok = jnp.broadcast_to(jnp.arange(n, dtype=jnp.int32)[:, None], (n, TOP_K))
    row_tok = (jnp.arange(nb * tm, dtype=jnp.int32) % n).at[dest.reshape(-1)].set(tok.reshape(-1))
    return dest, row_tok, block_e, valid, n_used.reshape(1)


def _pick(n, pref):
    t = min(n, pref)
    while n % t:
        t //= 2
    return t


def kernel(x_prompt, x_sample, cache_k, cache_v, cache_logf, state_conv, state_lru, w_in, b_f, w_conv, b_conv, w_a, b_a, w_i, b_i, lam, w_out, ln1_g, ln1_b, w_router, b_router, w_gu, b_gu, w_down, b_down, ln2_g, ln2_b):
    assert w_in.shape[0] == DEPTH
    bp, tp, d = x_prompt.shape
    bs, ts, _ = x_sample.shape
    past = cache_k.shape[2]
    lru_w = w_conv.shape[-1]
    np_, ns = bp * tp, bs * ts

    win = w_in[0]
    f0 = 3 * FOX_WIDTH
    w_main = jnp.concatenate([win[:, :f0], win[:, f0 + FOX_HEADS:]], axis=1).astype(BF16)
    w_f = jnp.pad(win[:, f0:f0 + FOX_HEADS], ((0, 0), (0, 128 - FOX_HEADS))).astype(BF16)
    bf2 = b_f[0].reshape(1, FOX_HEADS)
    gpb = 4
    bd = w_a.shape[-1]
    eye = jnp.eye(gpb, dtype=F32)

    def blockdiag(w):
        wg = w.reshape(LRU_BLOCKS // gpb, gpb, bd, bd)
        return jnp.einsum("gacd,ab->gacbd", wg, eye).reshape(LRU_BLOCKS // gpb, gpb * bd, gpb * bd)

    w_gate = jnp.concatenate([blockdiag(w_a[0]), blockdiag(w_i[0])], axis=-1).astype(BF16)
    w_top = w_out[0, :FOX_WIDTH].astype(BF16)
    w_bot = w_out[0, FOX_WIDTH:].astype(BF16)
    wr = w_router[0]
    wr_hi = wr.astype(BF16)
    wr_cat = jnp.concatenate([wr_hi, (wr - wr_hi.astype(F32)).astype(BF16)], axis=1)
    row = lambda a: a.reshape(1, -1)

    n_all = np_ + ns
    tm_mix = _pick(ns, 512)

    def mixers(x, batch, seq, fox_fn, conv0, h0, cnt0, x1b_buf, row0):
        n = batch * seq
        q, k, v, kb, vb, xr, yg, logf = _in_proj(x.reshape(n, d), w_main, w_f, bf2, _pick(n, 512))
        fox_o = fox_fn(q, kb, vb, logf.reshape(batch, seq, FOX_HEADS))
        lru_o, conv_new, h_last = _lru(xr, yg, conv0, h0, w_conv[0], row(b_conv[0]), w_gate,
                                       row(b_a[0]), row(b_i[0]), row(lam[0]), batch, seq, _pick(seq, 256))
        routed = _mix_norm(fox_o, lru_o, x.reshape(n, d), w_top, w_bot, row(ln1_g[0]), row(ln1_b[0]),
                           wr_cat, wr_hi, row(b_router[0]), cnt0, tm_mix, n_all, row0 // tm_mix, x1b_buf)
        state = (k.reshape(1, batch, seq, FOX_HEADS, FOX_HEAD_DIM), v.reshape(1, batch, seq, FOX_HEADS, FOX_HEAD_DIM),
                 logf.reshape(1, batch, seq, FOX_HEADS), conv_new[None], h_last.reshape(1, batch, lru_w))
        return routed, state

    def fox_p(q, kb, vb, logf):
        c = _cumsum_time(logf, jnp.zeros((bp, 1, FOX_HEADS), F32))
        return _fox_prompt(q, kb, vb, c * LOG2E, bp, tp, _pick(tp, 512))

    def fox_s(q, kb, vb, logf):
        c_past = _cumsum_time(cache_logf[0], jnp.zeros((bs, 1, FOX_HEADS), F32))
        c_new = _cumsum_time(logf, c_past[:, -1:, :])
        c_all = jnp.concatenate([c_past, c_new], axis=1) * LOG2E
        cache_rows = lambda c: c[0].reshape(bs, past * FOX_HEADS, FOX_HEAD_DIM)
        return _fox_sample(q, kb, vb, cache_rows(cache_k), cache_rows(cache_v), c_all, bs, ts, _pick(past, 512))

    (x1p, x1b, idxp, gp, rankp, cnt_p), state_p = mixers(
        x_prompt, bp, tp, fox_p, jnp.zeros((bp, CONV_WIDTH - 1, lru_w), F32), jnp.zeros((bp, 1, lru_w), F32),
        jnp.zeros((1, N_EXPERTS), F32), None, 0)
    (x1s, x1b, idxs, gs, ranks, cnt_all), state_s = mixers(
        x_sample, bs, ts, fox_s, state_conv[0], state_lru[0].reshape(bs, 1, lru_w), cnt_p, x1b, np_)

    tm_e, tc_e, sub_e = 1024, 512, 256
    idx = jnp.concatenate([idxp, idxs], axis=0)
    rank = jnp.concatenate([rankp, ranks], axis=0)
    dest, row_tok, block_e, block_valid, n_used = _route(idx, rank, cnt_all[0].astype(jnp.int32), tm_e)
    x_pad = x1b[row_tok]
    y_pad = _experts(x_pad, block_e, block_valid, n_used, w_gu[0], b_gu[0].reshape(N_EXPERTS, 1, -1),
                     w_down[0], b_down[0].reshape(N_EXPERTS, 1, -1), tm_e, tc_e, sub_e)
    y_rows = y_pad[dest.T]

    tn = _pick(ns, 512)
    yp = _combine(y_rows, gp, x1p, row(ln2_g[0]), row(ln2_b[0]), 0, tn)
    ys = _combine(y_rows, gs, x1s, row(ln2_g[0]), row(ln2_b[0]), np_ // tn, tn)
    return (yp.reshape(bp, tp, d), ys.reshape(bs, ts, d)) + state_p + state_s
```

```python
import functools

import jax
import jax.numpy as jnp
from jax import lax
from jax.experimental import pallas as pl
from jax.experimental.pallas import tpu as pltpu

F32 = jnp.float32
BF16 = jnp.bfloat16

FOX_HEADS = 8
FOX_HEAD_DIM = 128
FOX_WIDTH = FOX_HEADS * FOX_HEAD_DIM
LRU_BLOCKS = 16
CONV_WIDTH = 4
LRU_C = 8.0
N_EXPERTS = 32
TOP_K = 4
SWIGLU_LIMIT = 7.0
SWIGLU_ALPHA = 1.702
LN_EPS = 1e-5
DEPTH = 1
DEEPNORM_ALPHA = (2.0 * DEPTH) ** 0.25
LOG2E = 1.4426950408889634
Q_SCALE = FOX_HEAD_DIM ** -0.5 * LOG2E

VMEM_LIMIT = 56 * 1024 * 1024


def _cparams(sem):
    return pltpu.CompilerParams(dimension_semantics=sem, vmem_limit_bytes=VMEM_LIMIT)


def _log_sigmoid(x):
    return jnp.minimum(x, 0.0) - jnp.log1p(jnp.exp(-jnp.abs(x)))


def _layer_norm(z, g, b):
    mu = jnp.mean(z, axis=-1, keepdims=True)
    zc = z - mu
    var = jnp.mean(zc * zc, axis=-1, keepdims=True)
    return zc * lax.rsqrt(var + LN_EPS) * g + b


def _in_proj_kernel(x_ref, w_ref, wf_ref, bf_ref, q_ref, k_ref, v_ref, kb_ref, vb_ref, xr_ref, yg_ref, lf_ref,
                    xb_ref):
    j = pl.program_id(1)

    @pl.when(j == 0)
    def _():
        xb_ref[...] = x_ref[...].astype(BF16)
        zf = jnp.dot(xb_ref[...], wf_ref[...], preferred_element_type=F32)
        lf_ref[...] = _log_sigmoid(zf[:, :FOX_HEADS] + bf_ref[...])

    z = jnp.dot(xb_ref[...], w_ref[...], preferred_element_type=F32)

    @pl.when(j == 0)
    def _():
        q_ref[...] = (z * Q_SCALE).astype(BF16)

    @pl.when(j == 1)
    def _():
        k_ref[...] = z
        kb_ref[...] = z.astype(BF16)

    @pl.when(j == 2)
    def _():
        v_ref[...] = z
        vb_ref[...] = z.astype(BF16)

    @pl.when(j == 3)
    def _():
        xr_ref[...] = z

    @pl.when(j == 4)
    def _():
        yg_ref[...] = z


def _in_proj(x2d, w_main, w_f, b_f, tm):
    n, d = x2d.shape
    wcol = FOX_WIDTH
    row = lambda i, j: (i, 0)
    dts = (BF16, F32, F32, BF16, BF16, F32, F32)
    outs = [jax.ShapeDtypeStruct((n, wcol), dt) for dt in dts]
    outs.append(jax.ShapeDtypeStruct((n, FOX_HEADS), F32))
    return pl.pallas_call(
        _in_proj_kernel,
        out_shape=outs,
        grid=(n // tm, 5),
        in_specs=[
            pl.BlockSpec((tm, d), row),
            pl.BlockSpec((d, wcol), lambda i, j: (0, j)),
            pl.BlockSpec((d, 128), lambda i, j: (0, 0)),
            pl.BlockSpec((1, FOX_HEADS), lambda i, j: (0, 0)),
        ],
        out_specs=[pl.BlockSpec((tm, wcol), row)] * len(dts) + [pl.BlockSpec((tm, FOX_HEADS), row)],
        scratch_shapes=[pltpu.VMEM((tm, d), BF16)],
        compiler_params=_cparams(("parallel", "arbitrary")),
        name="in_proj",
    )(x2d, w_main, w_f, b_f)


def _cumsum_kernel(s_ref, c0_ref, o_ref, tri_ref):
    t = s_ref.shape[1]

    @pl.when(pl.program_id(0) == 0)
    def _():
        row = lax.broadcasted_iota(jnp.int32, (t, t), 0)
        col = lax.broadcasted_iota(jnp.int32, (t, t), 1)
        tri_ref[...] = jnp.where(col <= row, 1.0, 0.0).astype(BF16)

    c3 = jnp.dot(tri_ref[...], s_ref[0], preferred_element_type=F32)
    h = FOX_HEADS
    o_ref[0] = c3[:, 0:h] + c3[:, h:2 * h] + c3[:, 2 * h:3 * h] + c0_ref[0]


def _cumsum_time(x, c0):
    b, t, h = x.shape
    hi = x.astype(BF16)
    r1 = x - hi.astype(F32)
    mid = r1.astype(BF16)
    lo = (r1 - mid.astype(F32)).astype(BF16)
    pieces = jnp.concatenate([hi, mid, lo, jnp.zeros((b, t, 128 - 3 * h), BF16)], axis=-1)
    return pl.pallas_call(
        _cumsum_kernel,
        out_shape=jax.ShapeDtypeStruct((b, t, h), F32),
        grid=(b,),
        in_specs=[
            pl.BlockSpec((1, t, 128), lambda i: (i, 0, 0)),
            pl.BlockSpec((1, 1, h), lambda i: (i, 0, 0)),
        ],
        out_specs=pl.BlockSpec((1, t, h), lambda i: (i, 0, 0)),
        scratch_shapes=[pltpu.VMEM((t, t), BF16)],
        compiler_params=_cparams(("arbitrary",)),
        name="cumsum_time",
    )(pieces, c0)


_NT = (((1,), (1,)), ((), ()))


def _attn_init(m_ref, l_ref, acc_ref):
    m_ref[...] = jnp.full(m_ref.shape, -jnp.inf, F32)
    l_ref[...] = jnp.zeros(l_ref.shape, F32)
    acc_ref[...] = jnp.zeros(acc_ref.shape, F32)


def _fox_prompt_kernel(qi_ref, ki_ref, q_ref, k_ref, v_ref, ck_ref, o_ref,
                       m_ref, l_ref, acc_ref, s0_ref, s1_ref, p0_ref, p1_ref, a0_ref, a1_ref, *, rs):
    pair = pl.program_id(1)
    qi = qi_ref[pair]
    ki = ki_ref[pair]
    tq = q_ref.shape[0]
    tk = k_ref.shape[0]
    lanes = FOX_HEAD_DIM

    @pl.when(ki == 0)
    def _():
        _attn_init(m_ref, l_ref, acc_ref)

    def head_lanes(h):
        return pl.ds(pl.multiple_of(h * lanes, lanes), lanes)

    def scores(h, s_ref):
        sl = head_lanes(h)
        s_ref[...] = lax.dot_general(q_ref[:, sl], k_ref[:, sl], _NT, preferred_element_type=F32)

    def weighted_values(h, p_ref, a_ref):
        sl = head_lanes(h)
        pv = jnp.dot(p_ref[...], v_ref[:, sl], preferred_element_type=F32)
        acc_ref[:, sl] = a_ref[...] * acc_ref[:, sl] + pv

    def softmax(h, s_ref, p_ref, a_ref, masked):
        ck = ck_ref[0, h]
        for r in range(tq // rs):
            r0 = r * rs
            rsl = slice(r0, r0 + rs)
            chunks = []
            for c in range(tk // lanes):
                if masked and c * lanes > r0 + rs - 1:
                    continue
                s = s_ref[rsl, c * lanes:(c + 1) * lanes] - ck[:, c * lanes:(c + 1) * lanes]
                if masked and (c + 1) * lanes - 1 > r0:
                    row = r0 + lax.broadcasted_iota(jnp.int32, (rs, lanes), 0)
                    col = c * lanes + lax.broadcasted_iota(jnp.int32, (rs, lanes), 1)
                    s = jnp.where(col <= row, s, -jnp.inf)
                chunks.append((c, s))
            mc = chunks[0][1]
            for _, s in chunks[1:]:
                mc = jnp.maximum(mc, s)
            m_prev = m_ref[h, rsl, :]
            m_new = jnp.maximum(m_prev, jnp.max(mc, axis=-1, keepdims=True))
            alpha = jnp.exp2(m_prev - m_new)
            psum = None
            for c, s in chunks:
                p = jnp.exp2(s - m_new)
                p_ref[rsl, c * lanes:(c + 1) * lanes] = p.astype(BF16)
                psum = p if psum is None else psum + p
            for c in range(chunks[-1][0] + 1, tk // lanes):
                p_ref[rsl, c * lanes:(c + 1) * lanes] = jnp.zeros((rs, lanes), BF16)
            l_ref[h, rsl, :] = alpha * l_ref[h, rsl, :] + psum
            m_ref[h, rsl, :] = m_new
            a_ref[rsl, :] = alpha

    def all_heads(masked):
        scores(0, s0_ref)
        p1_ref[...] = jnp.zeros(p1_ref.shape, BF16)
        a1_ref[...] = jnp.ones(a1_ref.shape, F32)

        def two_heads(i, carry):
            h0 = 2 * i
            h1 = h0 + 1
            scores(h1, s1_ref)
            softmax(h0, s0_ref, p0_ref, a0_ref, masked)
            weighted_values(lax.rem(h0 + FOX_HEADS - 1, FOX_HEADS), p1_ref, a1_ref)
            scores(lax.rem(h0 + 2, FOX_HEADS), s0_ref)
            softmax(h1, s1_ref, p1_ref, a1_ref, masked)
            weighted_values(h0, p0_ref, a0_ref)
            return carry

        lax.fori_loop(0, FOX_HEADS // 2, two_heads, 0)
        weighted_values(FOX_HEADS - 1, p1_ref, a1_ref)

    @pl.when(ki < qi)
    def _():
        all_heads(False)

    @pl.when(ki == qi)
    def _():
        all_heads(True)
        for h in range(FOX_HEADS):
            sl = slice(h * lanes, (h + 1) * lanes)
            l_tot = jnp.sum(l_ref[h], axis=-1, keepdims=True)
            o_ref[:, sl] = (acc_ref[:, sl] / l_tot).astype(o_ref.dtype)


def _fox_prompt(q, kb, vb, c2, batch, seq, tq):
    nq = seq // tq
    width = q.shape[1]
    ck = jnp.transpose(c2, (0, 2, 1))[:, :, None, :]
    pairs = [(i, j) for i in range(nq) for j in range(i + 1)]
    qi_tab = jnp.asarray([p[0] for p in pairs], jnp.int32)
    ki_tab = jnp.asarray([p[1] for p in pairs], jnp.int32)
    q_map = lambda b, p, qt, kt: (b * nq + qt[p], 0)
    k_map = lambda b, p, qt, kt: (b * nq + kt[p], 0)
    grid_spec = pltpu.PrefetchScalarGridSpec(
        num_scalar_prefetch=2,
        grid=(batch, len(pairs)),
        in_specs=[
            pl.BlockSpec((tq, width), q_map),
            pl.BlockSpec((tq, width), k_map),
            pl.BlockSpec((tq, width), k_map),
            pl.BlockSpec((1, FOX_HEADS, 1, tq), lambda b, p, qt, kt: (b, 0, 0, kt[p])),
        ],
        out_specs=pl.BlockSpec((tq, width), q_map),
        scratch_shapes=[
            pltpu.VMEM((FOX_HEADS, tq, FOX_HEAD_DIM), F32),
            pltpu.VMEM((FOX_HEADS, tq, FOX_HEAD_DIM), F32),
            pltpu.VMEM((tq, width), F32),
            pltpu.VMEM((tq, tq), F32),
            pltpu.VMEM((tq, tq), F32),
            pltpu.VMEM((tq, tq), BF16),
            pltpu.VMEM((tq, tq), BF16),
            pltpu.VMEM((tq, FOX_HEAD_DIM), F32),
            pltpu.VMEM((tq, FOX_HEAD_DIM), F32),
        ],
    )
    return pl.pallas_call(
        functools.partial(_fox_prompt_kernel, rs=min(64, tq)),
        out_shape=jax.ShapeDtypeStruct(q.shape, BF16),
        grid_spec=grid_spec,
        compiler_params=_cparams(("parallel", "arbitrary")),
        name="fox_prompt",
    )(qi_tab, ki_tab, q, kb, vb, ck)


def _fox_sample_kernel(q_ref, kc_ref, vc_ref, kn_ref, vn_ref, ckc_ref, ckn_ref, o_ref, m_ref, l_ref, acc_ref):
    j = pl.program_id(1)
    tq = q_ref.shape[0]
    tk = kc_ref.shape[1] // FOX_HEADS
    lanes = FOX_HEAD_DIM

    @pl.when(j == 0)
    def _():
        _attn_init(m_ref, l_ref, acc_ref)

    heads = range(FOX_HEADS)
    head_lanes = [slice(h * lanes, (h + 1) * lanes) for h in heads]

    def update(ks, vs, cks, masked):
        ss = [lax.dot_general(q_ref[:, head_lanes[h]], ks[h], _NT, preferred_element_type=F32) - cks[h]
              for h in heads]
        if masked:
            row = lax.broadcasted_iota(jnp.int32, ss[0].shape, 0)
            col = lax.broadcasted_iota(jnp.int32, ss[0].shape, 1)
            ss = [jnp.where(col <= row, s, -jnp.inf) for s in ss]
        m_prev = [m_ref[h] for h in heads]
        m_new = [jnp.maximum(m_prev[h], jnp.max(ss[h], axis=-1, keepdims=True)) for h in heads]
        alpha = [jnp.exp2(m_prev[h] - m_new[h]) for h in heads]
        ps = [jnp.exp2(ss[h] - m_new[h][:, 0:1]) for h in heads]
        pvs = [jnp.dot(ps[h].astype(BF16), vs[h], preferred_element_type=F32) for h in heads]
        for h in heads:
            l_ref[h] = alpha[h] * l_ref[h] + jnp.sum(ps[h], axis=-1, keepdims=True)
            acc_ref[:, head_lanes[h]] = alpha[h] * acc_ref[:, head_lanes[h]] + pvs[h]
            m_ref[h] = m_new[h]

    update([kc_ref[0, pl.ds(h, tk, stride=FOX_HEADS), :].astype(BF16) for h in heads],
           [vc_ref[0, pl.ds(h, tk, stride=FOX_HEADS), :].astype(BF16) for h in heads],
           [ckc_ref[0, h] for h in heads], False)

    @pl.when(j == pl.num_programs(1) - 1)
    def _():
        update([kn_ref[:, sl] for sl in head_lanes], [vn_ref[:, sl] for sl in head_lanes],
               [ckn_ref[0, h] for h in heads], True)
        for h in heads:
            sl = head_lanes[h]
            o_ref[:, sl] = (acc_ref[:, sl] / l_ref[h]).astype(o_ref.dtype)


def _fox_sample(q, kb, vb, cache_k, cache_v, c2_all, batch, seq, tk):
    past = cache_k.shape[1] // FOX_HEADS
    width = q.shape[1]
    ct = jnp.transpose(c2_all, (0, 2, 1))
    ckc = ct[:, :, None, :past]
    ckn = ct[:, :, None, past:]
    new_map = lambda b, j: (b, 0)
    cache_spec = pl.BlockSpec((1, tk * FOX_HEADS, FOX_HEAD_DIM), lambda b, j: (b, j, 0))
    return pl.pallas_call(
        _fox_sample_kernel,
        out_shape=jax.ShapeDtypeStruct(q.shape, BF16),
        grid=(batch, past // tk),
        in_specs=[
            pl.BlockSpec((seq, width), new_map),
            cache_spec,
            cache_spec,
            pl.BlockSpec((seq, width), new_map),
            pl.BlockSpec((seq, width), new_map),
            pl.BlockSpec((1, FOX_HEADS, 1, tk), lambda b, j: (b, 0, 0, j)),
            pl.BlockSpec((1, FOX_HEADS, 1, seq), lambda b, j: (b, 0, 0, 0)),
        ],
        out_specs=pl.BlockSpec((seq, width), new_map),
        scratch_shapes=[
            pltpu.VMEM((FOX_HEADS, seq, FOX_HEAD_DIM), F32),
            pltpu.VMEM((FOX_HEADS, seq, FOX_HEAD_DIM), F32),
            pltpu.VMEM((seq, width), F32),
        ],
        compiler_params=_cparams(("parallel", "arbitrary")),
        name="fox_sample",
    )(q, cache_k, cache_v, kb, vb, ckc, ckn)


_HALO = 8


def _lru_kernel(xr_ref, yg_ref, conv0_ref, h0_ref, wc_ref, bc_ref, wg_ref, ba_ref, bi_ref, lam_ref,
                o_ref, convo_ref, hlast_ref, xp_ref, a_ref, b_ref, h_ref):
    t = pl.program_id(1)
    tt, width = xr_ref.shape
    tail = CONV_WIDTH - 1
    lo = _HALO - tail

    @pl.when(t == 0)
    def _():
        xp_ref[lo:_HALO, :] = conv0_ref[0]
        h_ref[...] = h0_ref[0]

    xp_ref[_HALO:_HALO + tt, :] = xr_ref[...]
    xc = bc_ref[...] + xp_ref[lo:lo + tt, :] * wc_ref[0:1, :]
    for j in range(1, CONV_WIDTH):
        xc = xc + xp_ref[lo + j:lo + j + tt, :] * wc_ref[j:j + 1, :]
    new_tail = xp_ref[lo + tt:_HALO + tt, :]
    xp_ref[lo:_HALO, :] = new_tail

    xcb = xc.astype(BF16)
    gw = wg_ref.shape[1]
    lam = lam_ref[...]
    neg_sp = -(jnp.maximum(-lam, 0.0) + jnp.log1p(jnp.exp(-jnp.abs(lam))))
    for g in range(width // gw):
        sl = slice(g * gw, (g + 1) * gw)
        z = jnp.dot(xcb[:, sl], wg_ref[g], preferred_element_type=F32)
        r = jax.nn.sigmoid(z[:, :gw] + ba_ref[:, sl])
        i = jax.nn.sigmoid(z[:, gw:] + bi_ref[:, sl])
        log_a = LRU_C * r * neg_sp[:, sl]
        a = jnp.exp(log_a)
        a_ref[:, sl] = a
        b_ref[:, sl] = jnp.sqrt(-jnp.tanh(log_a) * (a * a + 1.0)) * i * xc[:, sl]

    def step(s, h):
        row = pl.ds(s, 1)
        h = a_ref[row, :] * h + b_ref[row, :]
        b_ref[row, :] = h
        return h

    h = lax.fori_loop(0, tt, step, h_ref[...], unroll=8)
    h_ref[...] = h
    o_ref[...] = (b_ref[...] * jax.nn.gelu(yg_ref[...])).astype(o_ref.dtype)

    @pl.when(t == pl.num_programs(1) - 1)
    def _():
        convo_ref[0] = new_tail
        hlast_ref[0] = h


def _lru(xr, yg, conv0, h0, w_conv, b_conv, w_gate, b_a, b_i, lam, batch, seq, tt):
    width = xr.shape[1]
    nt = seq // tt
    gw = w_gate.shape[1]
    rows = lambda b, t: (b * nt + t, 0)
    const2 = lambda b, t: (0, 0)
    per_b = lambda b, t: (b, 0, 0)
    tail = CONV_WIDTH - 1
    return pl.pallas_call(
        _lru_kernel,
        out_shape=[
            jax.ShapeDtypeStruct((batch * seq, width), BF16),
            jax.ShapeDtypeStruct((batch, tail, width), F32),
            jax.ShapeDtypeStruct((batch, 1, width), F32),
        ],
        grid=(batch, nt),
        in_specs=[
            pl.BlockSpec((tt, width), rows),
            pl.BlockSpec((tt, width), rows),
            pl.BlockSpec((1, tail, width), per_b),
            pl.BlockSpec((1, 1, width), per_b),
            pl.BlockSpec((CONV_WIDTH, width), const2),
            pl.BlockSpec((1, width), const2),
            pl.BlockSpec((width // gw, gw, 2 * gw), lambda b, t: (0, 0, 0)),
            pl.BlockSpec((1, width), const2),
            pl.BlockSpec((1, width), const2),
            pl.BlockSpec((1, width), const2),
        ],
        out_specs=[
            pl.BlockSpec((tt, width), rows),
            pl.BlockSpec((1, tail, width), per_b),
            pl.BlockSpec((1, 1, width), per_b),
        ],
        scratch_shapes=[
            pltpu.VMEM((_HALO + tt, width), F32),
            pltpu.VMEM((tt, width), F32),
            pltpu.VMEM((tt, width), F32),
            pltpu.VMEM((1, width), F32),
        ],
        compiler_params=_cparams(("parallel", "arbitrary")),
        name="lru",
    )(xr, yg, conv0, h0, w_conv, b_conv, w_gate, b_a, b_i, lam)


def _mix_norm_kernel(*refs, aliased, n_tiles):
    ins = refs[:11]
    outs = refs[12:] if aliased else refs[11:]
    live = pl.program_id(0) < n_tiles

    @pl.when(live)
    def _():
        _mix_norm_body(*ins, *outs)

    @pl.when(jnp.logical_not(live))
    def _():
        outs[1][...] = jnp.zeros(outs[1].shape, outs[1].dtype)


def _mix_norm_body(fox_ref, lru_ref, x_ref, wt_ref, wb_ref, g_ref, b_ref, wrc_ref, wrh_ref, br_ref, cnt0_ref,
                   x1_ref, x1b_ref, idx_ref, gate_ref, rank_ref, cnt_ref):
    tm = x_ref.shape[0]

    mix = jnp.dot(fox_ref[...], wt_ref[...], preferred_element_type=F32)
    mix = mix + jnp.dot(lru_ref[...], wb_ref[...], preferred_element_type=F32)
    x1 = _layer_norm(DEEPNORM_ALPHA * x_ref[...] + mix, g_ref[...], b_ref[...])
    x1_ref[...] = x1
    x1_hi = x1.astype(BF16)
    x1b_ref[...] = x1_hi
    x1_lo = (x1 - x1_hi.astype(F32)).astype(BF16)
    lg2 = jnp.dot(x1_hi, wrc_ref[...], preferred_element_type=F32)
    lg = lg2[:, :N_EXPERTS] + lg2[:, N_EXPERTS:] + jnp.dot(x1_lo, wrh_ref[...], preferred_element_type=F32)
    lg = lg + br_ref[...]
    lanes = lax.broadcasted_iota(jnp.int32, lg.shape, 1)
    vals, picks = [], []
    for k in range(TOP_K):
        m = jnp.max(lg, axis=-1, keepdims=True)
        ix = jnp.min(jnp.where(lg == m, lanes, N_EXPERTS), axis=-1, keepdims=True)
        idx_ref[:, k:k + 1] = ix
        vals.append(m)
        picks.append(lanes == ix)
        lg = jnp.where(picks[-1], -jnp.inf, lg)
    es = [jnp.exp(v - vals[0]) for v in vals]
    denom = es[0] + es[1] + es[2] + es[3]
    for k in range(TOP_K):
        gate_ref[:, k:k + 1] = es[k] / denom

    @pl.when(pl.program_id(0) == 0)
    def _():
        cnt_ref[...] = cnt0_ref[...]

    sel = jnp.where(picks[0], 1.0, 0.0)
    for k in range(1, TOP_K):
        sel = sel + jnp.where(picks[k], 1.0, 0.0)
    earlier = lax.broadcasted_iota(jnp.int32, (tm, tm), 1) < lax.broadcasted_iota(jnp.int32, (tm, tm), 0)
    tri = jnp.where(earlier, 1.0, 0.0).astype(BF16)
    before = jnp.dot(tri, sel.astype(BF16), preferred_element_type=F32) + cnt_ref[...]
    for k in range(TOP_K):
        rank_ref[:, k:k + 1] = jnp.sum(jnp.where(picks[k], before, 0.0), axis=-1, keepdims=True).astype(jnp.int32)
    cnt_ref[...] += jnp.sum(sel, axis=0, keepdims=True)


def _mix_norm(fox_o, lru_o, x2d, w_top, w_bot, ln_g, ln_b, wr_cat, wr_hi, b_router, cnt0, tm, n_all, row_block0,
              x1b_buf=None):
    n, d = x2d.shape
    half = fox_o.shape[1]
    n_tiles = n // tm
    aliased = x1b_buf is not None
    steps = n_tiles if aliased else n_all // tm - row_block0
    row = lambda i: (jnp.minimum(i, n_tiles - 1), 0)
    const = lambda i: (0, 0)
    in_specs = [
        pl.BlockSpec((tm, half), row),
        pl.BlockSpec((tm, half), row),
        pl.BlockSpec((tm, d), row),
        pl.BlockSpec((half, d), const),
        pl.BlockSpec((half, d), const),
        pl.BlockSpec((1, d), const),
        pl.BlockSpec((1, d), const),
        pl.BlockSpec((d, 2 * N_EXPERTS), const),
        pl.BlockSpec((d, N_EXPERTS), const),
        pl.BlockSpec((1, N_EXPERTS), const),
        pl.BlockSpec((1, N_EXPERTS), const),
    ]
    args = [fox_o, lru_o, x2d, w_top, w_bot, ln_g, ln_b, wr_cat, wr_hi, b_router, cnt0]
    if aliased:
        in_specs.append(pl.BlockSpec(memory_space=pl.ANY))
        args.append(x1b_buf)
    return pl.pallas_call(
        functools.partial(_mix_norm_kernel, aliased=aliased, n_tiles=n_tiles),
        out_shape=[
            jax.ShapeDtypeStruct((n, d), F32),
            jax.ShapeDtypeStruct((n_all, d), BF16),
            jax.ShapeDtypeStruct((n, TOP_K), jnp.int32),
            jax.ShapeDtypeStruct((n, TOP_K), F32),
            jax.ShapeDtypeStruct((n, TOP_K), jnp.int32),
            jax.ShapeDtypeStruct((1, N_EXPERTS), F32),
        ],
        grid=(steps,),
        in_specs=in_specs,
        out_specs=[
            pl.BlockSpec((tm, d), row),
            pl.BlockSpec((tm, d), lambda i: (row_block0 + i, 0)),
            pl.BlockSpec((tm, TOP_K), row),
            pl.BlockSpec((tm, TOP_K), row),
            pl.BlockSpec((tm, TOP_K), row),
            pl.BlockSpec((1, N_EXPERTS), const),
        ],
        input_output_aliases={11: 1} if aliased else {},
        compiler_params=_cparams(("arbitrary",)),
        name="mix_norm",
    )(*args)


_RING = 6
_AHEAD = 4


def _expert_kernel(be_ref, bv_ref, nu_ref, x_ref, bgu_ref, bd_ref, wgu_hbm, wd_hbm, o_ref,
                   act_ref, ring_ref, sem_ref, *, sub, tc):
    rb = pl.program_id(0)
    valid = bv_ref[rb]
    tm, d = x_ref.shape
    d_ff = act_ref.shape[1]
    nfc = d_ff // tc
    nnc = d // tc
    per_block = 2 * nfc + nnc
    total = nu_ref[0] * per_block

    def chunk_copy(src, slot):
        return pltpu.make_async_copy(src, ring_ref.at[slot], sem_ref.at[slot])

    def start(cg):
        @pl.when(cg < total)
        def _():
            blk = cg // per_block
            j = cg - blk * per_block
            e = be_ref[blk]
            slot = lax.rem(cg, _RING)

            @pl.when(j < 2 * nfc)
            def _():
                col = pl.multiple_of((lax.rem(j, 2) * nfc + j // 2) * tc, tc)
                chunk_copy(wgu_hbm.at[e, :, pl.ds(col, tc)], slot).start()

            @pl.when(j >= 2 * nfc)
            def _():
                col = pl.multiple_of((j - 2 * nfc) * tc, tc)
                chunk_copy(wd_hbm.at[e, :, pl.ds(col, tc)], slot).start()

    def wait(cg):
        slot = lax.rem(cg, _RING)
        chunk_copy(wgu_hbm.at[0, :, pl.ds(0, tc)], slot).wait()
        return slot

    @pl.when(rb == 0)
    def _():
        for c in range(_AHEAD):
            start(c)

    base = rb * per_block

    def block(nrows):
        def up(j, carry):
            cg = base + 2 * j
            start(cg + _AHEAD)
            start(cg + 1 + _AHEAD)
            gate_slot = wait(cg)
            up_slot = wait(cg + 1)
            col = pl.ds(pl.multiple_of(j * tc, tc), tc)
            ucol = pl.ds(pl.multiple_of(d_ff + j * tc, tc), tc)
            x = x_ref[0:nrows, :]
            hg = jnp.dot(x, ring_ref[gate_slot].astype(BF16), preferred_element_type=F32) + bgu_ref[0, :, col]
            hu = jnp.dot(x, ring_ref[up_slot].astype(BF16), preferred_element_type=F32) + bgu_ref[0, :, ucol]
            gate = jnp.minimum(hg, SWIGLU_LIMIT)
            upv = jnp.clip(hu, -SWIGLU_LIMIT, SWIGLU_LIMIT)
            act = (upv + 1.0) * gate * jax.nn.sigmoid(SWIGLU_ALPHA * gate)
            act_ref[0:nrows, col] = act.astype(BF16)
            return carry

        lax.fori_loop(0, nfc, up, 0)

        def down(n, carry):
            cg = base + 2 * nfc + n
            start(cg + _AHEAD)
            slot = wait(cg)
            col = pl.ds(pl.multiple_of(n * tc, tc), tc)
            y = jnp.dot(act_ref[0:nrows, :], ring_ref[slot].astype(BF16), preferred_element_type=F32)
            o_ref[0:nrows, col] = (y + bd_ref[0, :, col]).astype(o_ref.dtype)
            return carry

        lax.fori_loop(0, nnc, down, 0)
        if nrows < tm:
            o_ref[nrows:tm, :] = jnp.zeros((tm - nrows, d), o_ref.dtype)

    for nrows in range(sub, tm + 1, sub):
        @pl.when(jnp.logical_and(valid > nrows - sub, valid <= nrows))
        def _(nrows=nrows):
            block(nrows)

    @pl.when(valid == 0)
    def _():
        o_ref[...] = jnp.zeros(o_ref.shape, o_ref.dtype)


def _experts(x_pad, block_e, block_valid, n_used, w_gu, b_gu, w_down, b_down, tm, tc, sub):
    p, d = x_pad.shape
    d_ff = w_down.shape[1]
    assert d_ff == d and _AHEAD <= _RING - 2
    nb = p // tm

    def rbc(rb, nu):
        return jnp.minimum(rb, nu[0] - 1)

    grid_spec = pltpu.PrefetchScalarGridSpec(
        num_scalar_prefetch=3,
        grid=(nb,),
        in_specs=[
            pl.BlockSpec((tm, d), lambda rb, be, bv, nu: (rbc(rb, nu), 0)),
            pl.BlockSpec((1, 1, 2 * d_ff), lambda rb, be, bv, nu: (be[rbc(rb, nu)], 0, 0)),
            pl.BlockSpec((1, 1, d), lambda rb, be, bv, nu: (be[rbc(rb, nu)], 0, 0)),
            pl.BlockSpec(memory_space=pl.ANY),
            pl.BlockSpec(memory_space=pl.ANY),
        ],
        out_specs=pl.BlockSpec((tm, d), lambda rb, be, bv, nu: (rb, 0)),
        scratch_shapes=[
            pltpu.VMEM((tm, d_ff), BF16),
            pltpu.VMEM((_RING, d, tc), F32),
            pltpu.SemaphoreType.DMA((_RING,)),
        ],
    )
    return pl.pallas_call(
        functools.partial(_expert_kernel, sub=sub, tc=tc),
        out_shape=jax.ShapeDtypeStruct((p, d), BF16),
        grid_spec=grid_spec,
        compiler_params=_cparams(("arbitrary",)),
        name="experts",
    )(block_e, block_valid, n_used, x_pad, b_gu, b_down, w_gu, w_down)


def _combine_kernel(y_ref, gate_ref, x1_ref, g_ref, b_ref, o_ref):
    y = y_ref[0].astype(F32) * gate_ref[:, 0:1]
    for k in range(1, TOP_K):
        y = y + y_ref[k].astype(F32) * gate_ref[:, k:k + 1]
    o_ref[...] = _layer_norm(DEEPNORM_ALPHA * x1_ref[...] + y, g_ref[...], b_ref[...])


def _combine(y_rows, gates, x1, ln_g, ln_b, row_block0, tn):
    n, d = x1.shape
    const = lambda i: (0, 0)
    return pl.pallas_call(
        _combine_kernel,
        out_shape=jax.ShapeDtypeStruct((n, d), F32),
        grid=(n // tn,),
        in_specs=[
            pl.BlockSpec((TOP_K, tn, d), lambda i: (0, row_block0 + i, 0)),
            pl.BlockSpec((tn, TOP_K), lambda i: (i, 0)),
            pl.BlockSpec((tn, d), lambda i: (i, 0)),
            pl.BlockSpec((1, d), const),
            pl.BlockSpec((1, d), const),
        ],
        out_specs=pl.BlockSpec((tn, d), lambda i: (i, 0)),
        compiler_params=_cparams(("parallel",)),
        name="combine",
    )(y_rows, gates, x1, ln_g, ln_b)


def _route(idx, rank, sizes, tm):
    n = idx.shape[0]
    nk = n * TOP_K
    nblk = (sizes + tm - 1) // tm
    bends = jnp.cumsum(nblk)
    bstart = bends - nblk
    dest = bstart[idx] * tm + rank
    nb = -(-nk // tm) + N_EXPERTS
    blk = jnp.arange(nb, dtype=jnp.int32)
    block_e = jnp.minimum(jnp.sum(bends[None, :] <= blk[:, None], axis=1), N_EXPERTS - 1).astype(jnp.int32)
    n_used = bends[-1].astype(jnp.int32)
    valid = jnp.clip(sizes[block_e] - (blk - bstart[block_e]) * tm, 0, tm)
    valid = jnp.where(blk < n_used, valid, 0).astype(jnp.int32)
    tok = jnp.broadcast_to(jnp.arange(n, dtype=jnp.int32)[:, None], (n, TOP_K))
    keys = jnp.sort((idx * n + tok).reshape(-1))
    tok_sorted = keys - (keys // n) * n
    row = jnp.arange(nb * tm, dtype=jnp.int32)
    row_e = jnp.repeat(block_e, tm)
    off = row - bstart[row_e] * tm
    first = jnp.cumsum(sizes) - sizes
    src = jnp.clip(first[row_e] + off, 0, nk - 1)
    row_tok = jnp.where(off < sizes[row_e], tok_sorted[src], row % n)
    return dest, row_tok, block_e, valid, n_used.reshape(1)


def _pick(n, pref):
    t = min(n, pref)
    while n % t:
        t //= 2
    return t


def kernel(x_prompt, x_sample, cache_k, cache_v, cache_logf, state_conv, state_lru, w_in, b_f, w_conv, b_conv, w_a, b_a, w_i, b_i, lam, w_out, ln1_g, ln1_b, w_router, b_router, w_gu, b_gu, w_down, b_down, ln2_g, ln2_b):
    assert w_in.shape[0] == DEPTH
    bp, tp, d = x_prompt.shape
    bs, ts, _ = x_sample.shape
    past = cache_k.shape[2]
    lru_w = w_conv.shape[-1]
    np_, ns = bp * tp, bs * ts

    win = w_in[0]
    f0 = 3 * FOX_WIDTH
    w_main = jnp.concatenate([win[:, :f0], win[:, f0 + FOX_HEADS:]], axis=1).astype(BF16)
    w_f = jnp.pad(win[:, f0:f0 + FOX_HEADS], ((0, 0), (0, 128 - FOX_HEADS))).astype(BF16)
    bf2 = b_f[0].reshape(1, FOX_HEADS)
    gpb = 4
    bd = w_a.shape[-1]
    eye = jnp.eye(gpb, dtype=F32)

    def blockdiag(w):
        wg = w.reshape(LRU_BLOCKS // gpb, gpb, bd, bd)
        return jnp.einsum("gacd,ab->gacbd", wg, eye).reshape(LRU_BLOCKS // gpb, gpb * bd, gpb * bd)

    w_gate = jnp.concatenate([blockdiag(w_a[0]), blockdiag(w_i[0])], axis=-1).astype(BF16)
    w_top = w_out[0, :FOX_WIDTH].astype(BF16)
    w_bot = w_out[0, FOX_WIDTH:].astype(BF16)
    wr = w_router[0]
    wr_hi = wr.astype(BF16)
    wr_cat = jnp.concatenate([wr_hi, (wr - wr_hi.astype(F32)).astype(BF16)], axis=1)
    row = lambda a: a.reshape(1, -1)

    n_all = np_ + ns
    tm_mix = _pick(ns, 512)

    def mixers(x, batch, seq, fox_fn, conv0, h0, cnt0, x1b_buf, row0):
        n = batch * seq
        q, k, v, kb, vb, xr, yg, logf = _in_proj(x.reshape(n, d), w_main, w_f, bf2, _pick(n, 512))
        fox_o = fox_fn(q, kb, vb, logf.reshape(batch, seq, FOX_HEADS))
        lru_o, conv_new, h_last = _lru(xr, yg, conv0, h0, w_conv[0], row(b_conv[0]), w_gate,
                                       row(b_a[0]), row(b_i[0]), row(lam[0]), batch, seq, _pick(seq, 256))
        routed = _mix_norm(fox_o, lru_o, x.reshape(n, d), w_top, w_bot, row(ln1_g[0]), row(ln1_b[0]),
                           wr_cat, wr_hi, row(b_router[0]), cnt0, tm_mix, n_all, row0 // tm_mix, x1b_buf)
        state = (k.reshape(1, batch, seq, FOX_HEADS, FOX_HEAD_DIM), v.reshape(1, batch, seq, FOX_HEADS, FOX_HEAD_DIM),
                 logf.reshape(1, batch, seq, FOX_HEADS), conv_new[None], h_last.reshape(1, batch, lru_w))
        return routed, state

    def fox_p(q, kb, vb, logf):
        c = _cumsum_time(logf, jnp.zeros((bp, 1, FOX_HEADS), F32))
        return _fox_prompt(q, kb, vb, c * LOG2E, bp, tp, _pick(tp, 512))

    def fox_s(q, kb, vb, logf):
        c_past = _cumsum_time(cache_logf[0], jnp.zeros((bs, 1, FOX_HEADS), F32))
        c_new = _cumsum_time(logf, c_past[:, -1:, :])
        c_all = jnp.concatenate([c_past, c_new], axis=1) * LOG2E
        cache_rows = lambda c: c[0].reshape(bs, past * FOX_HEADS, FOX_HEAD_DIM)
        return _fox_sample(q, kb, vb, cache_rows(cache_k), cache_rows(cache_v), c_all, bs, ts, _pick(past, 512))

    (x1p, x1b, idxp, gp, rankp, cnt_p), state_p = mixers(
        x_prompt, bp, tp, fox_p, jnp.zeros((bp, CONV_WIDTH - 1, lru_w), F32), jnp.zeros((bp, 1, lru_w), F32),
        jnp.zeros((1, N_EXPERTS), F32), None, 0)
    (x1s, x1b, idxs, gs, ranks, cnt_all), state_s = mixers(
        x_sample, bs, ts, fox_s, state_conv[0], state_lru[0].reshape(bs, 1, lru_w), cnt_p, x1b, np_)

    tm_e, tc_e, sub_e = 1024, 512, 256
    idx = jnp.concatenate([idxp, idxs], axis=0)
    rank = jnp.concatenate([rankp, ranks], axis=0)
    dest, row_tok, block_e, block_valid, n_used = _route(idx, rank, cnt_all[0].astype(jnp.int32), tm_e)
    x_pad = x1b[row_tok]
    y_pad = _experts(x_pad, block_e, block_valid, n_used, w_gu[0], b_gu[0].reshape(N_EXPERTS, 1, -1),
                     w_down[0], b_down[0].reshape(N_EXPERTS, 1, -1), tm_e, tc_e, sub_e)
    y_rows = y_pad[dest.T]

    tn = _pick(ns, 512)
    yp = _combine(y_rows, gp, x1p, row(ln2_g[0]), row(ln2_b[0]), 0, tn)
    ys = _combine(y_rows, gs, x1s, row(ln2_g[0]), row(ln2_b[0]), np_ // tn, tn)
    return (yp.reshape(bp, tp, d), ys.reshape(bs, ts, d)) + state_p + state_s
```

```python
import functools

import jax
import jax.numpy as jnp
from jax import lax
from jax.experimental import pallas as pl
from jax.experimental.pallas import tpu as pltpu

F32 = jnp.float32
BF16 = jnp.bfloat16

FOX_HEADS = 8
FOX_HEAD_DIM = 128
FOX_WIDTH = FOX_HEADS * FOX_HEAD_DIM
LRU_BLOCKS = 16
CONV_WIDTH = 4
LRU_C = 8.0
N_EXPERTS = 32
TOP_K = 4
SWIGLU_LIMIT = 7.0
SWIGLU_ALPHA = 1.702
LN_EPS = 1e-5
DEPTH = 1
DEEPNORM_ALPHA = (2.0 * DEPTH) ** 0.25
LOG2E = 1.4426950408889634
Q_SCALE = FOX_HEAD_DIM ** -0.5 * LOG2E

VMEM_LIMIT = 56 * 1024 * 1024


def _cparams(sem):
    return pltpu.CompilerParams(dimension_semantics=sem, vmem_limit_bytes=VMEM_LIMIT)


def _log_sigmoid(x):
    return jnp.minimum(x, 0.0) - jnp.log1p(jnp.exp(-jnp.abs(x)))


def _layer_norm(z, g, b):
    mu = jnp.mean(z, axis=-1, keepdims=True)
    zc = z - mu
    var = jnp.mean(zc * zc, axis=-1, keepdims=True)
    return zc * lax.rsqrt(var + LN_EPS) * g + b


def _in_proj_kernel(x_ref, w_ref, wf_ref, bf_ref, q_ref, k_ref, v_ref, kb_ref, vb_ref, xr_ref, yg_ref, lf_ref,
                    xb_ref):
    j = pl.program_id(1)

    @pl.when(j == 0)
    def _():
        xb_ref[...] = x_ref[...].astype(BF16)
        zf = jnp.dot(xb_ref[...], wf_ref[...], preferred_element_type=F32)
        lf_ref[...] = _log_sigmoid(zf[:, :FOX_HEADS] + bf_ref[...])

    z = jnp.dot(xb_ref[...], w_ref[...], preferred_element_type=F32)

    @pl.when(j == 0)
    def _():
        q_ref[...] = (z * Q_SCALE).astype(BF16)

    @pl.when(j == 1)
    def _():
        k_ref[...] = z
        kb_ref[...] = z.astype(BF16)

    @pl.when(j == 2)
    def _():
        v_ref[...] = z
        vb_ref[...] = z.astype(BF16)

    @pl.when(j == 3)
    def _():
        xr_ref[...] = z

    @pl.when(j == 4)
    def _():
        yg_ref[...] = z


def _in_proj(x2d, w_main, w_f, b_f, tm):
    n, d = x2d.shape
    wcol = FOX_WIDTH
    row = lambda i, j: (i, 0)
    dts = (BF16, F32, F32, BF16, BF16, F32, F32)
    outs = [jax.ShapeDtypeStruct((n, wcol), dt) for dt in dts]
    outs.append(jax.ShapeDtypeStruct((n, FOX_HEADS), F32))
    return pl.pallas_call(
        _in_proj_kernel,
        out_shape=outs,
        grid=(n // tm, 5),
        in_specs=[
            pl.BlockSpec((tm, d), row),
            pl.BlockSpec((d, wcol), lambda i, j: (0, j)),
            pl.BlockSpec((d, 128), lambda i, j: (0, 0)),
            pl.BlockSpec((1, FOX_HEADS), lambda i, j: (0, 0)),
        ],
        out_specs=[pl.BlockSpec((tm, wcol), row)] * len(dts) + [pl.BlockSpec((tm, FOX_HEADS), row)],
        scratch_shapes=[pltpu.VMEM((tm, d), BF16)],
        compiler_params=_cparams(("parallel", "arbitrary")),
        name="in_proj",
    )(x2d, w_main, w_f, b_f)


def _cumsum_kernel(s_ref, c0_ref, o_ref, tri_ref):
    t = s_ref.shape[1]
    tc = tri_ref.shape[0]
    h = FOX_HEADS

    @pl.when(pl.program_id(0) == 0)
    def _():
        row = lax.broadcasted_iota(jnp.int32, (tc, tc), 0)
        col = lax.broadcasted_iota(jnp.int32, (tc, tc), 1)
        tri_ref[...] = jnp.where(col <= row, 1.0, 0.0).astype(BF16)

    carry = c0_ref[0]
    for c in range(t // tc):
        rows = slice(c * tc, (c + 1) * tc)
        c3 = jnp.dot(tri_ref[...], s_ref[0, rows, :], preferred_element_type=F32)
        out = c3[:, 0:h] + c3[:, h:2 * h] + c3[:, 2 * h:3 * h] + carry
        o_ref[0, rows, :] = out
        carry = out[tc - 1:tc, :]


def _cumsum_time(x, c0):
    b, t, h = x.shape
    hi = x.astype(BF16)
    r1 = x - hi.astype(F32)
    mid = r1.astype(BF16)
    lo = (r1 - mid.astype(F32)).astype(BF16)
    pieces = jnp.concatenate([hi, mid, lo, jnp.zeros((b, t, 128 - 3 * h), BF16)], axis=-1)
    return pl.pallas_call(
        _cumsum_kernel,
        out_shape=jax.ShapeDtypeStruct((b, t, h), F32),
        grid=(b,),
        in_specs=[
            pl.BlockSpec((1, t, 128), lambda i: (i, 0, 0)),
            pl.BlockSpec((1, 1, h), lambda i: (i, 0, 0)),
        ],
        out_specs=pl.BlockSpec((1, t, h), lambda i: (i, 0, 0)),
        scratch_shapes=[pltpu.VMEM((_pick(t, 256), _pick(t, 256)), BF16)],
        compiler_params=_cparams(("arbitrary",)),
        name="cumsum_time",
    )(pieces, c0)


_NT = (((1,), (1,)), ((), ()))


def _attn_init(m_ref, l_ref, acc_ref):
    m_ref[...] = jnp.full(m_ref.shape, -jnp.inf, F32)
    l_ref[...] = jnp.zeros(l_ref.shape, F32)
    acc_ref[...] = jnp.zeros(acc_ref.shape, F32)


def _fox_prompt_kernel(qi_ref, ki_ref, q_ref, k_ref, v_ref, ck_ref, o_ref,
                       m_ref, l_ref, acc_ref, s0_ref, s1_ref, p0_ref, p1_ref, a0_ref, a1_ref, *, rs):
    pair = pl.program_id(1)
    qi = qi_ref[pair]
    ki = ki_ref[pair]
    tq = q_ref.shape[0]
    tk = k_ref.shape[0]
    lanes = FOX_HEAD_DIM

    @pl.when(ki == 0)
    def _():
        _attn_init(m_ref, l_ref, acc_ref)

    def head_lanes(h):
        return pl.ds(pl.multiple_of(h * lanes, lanes), lanes)

    def scores(h, s_ref):
        sl = head_lanes(h)
        s_ref[...] = lax.dot_general(q_ref[:, sl], k_ref[:, sl], _NT, preferred_element_type=F32)

    def weighted_values(h, p_ref, a_ref):
        sl = head_lanes(h)
        pv = jnp.dot(p_ref[...], v_ref[:, sl], preferred_element_type=F32)
        acc_ref[:, sl] = a_ref[...] * acc_ref[:, sl] + pv

    def softmax(h, s_ref, p_ref, a_ref, masked):
        ck = ck_ref[0, h]
        for r in range(tq // rs):
            r0 = r * rs
            rsl = slice(r0, r0 + rs)
            chunks = []
            for c in range(tk // lanes):
                if masked and c * lanes > r0 + rs - 1:
                    continue
                s = s_ref[rsl, c * lanes:(c + 1) * lanes] - ck[:, c * lanes:(c + 1) * lanes]
                if masked and (c + 1) * lanes - 1 > r0:
                    row = r0 + lax.broadcasted_iota(jnp.int32, (rs, lanes), 0)
                    col = c * lanes + lax.broadcasted_iota(jnp.int32, (rs, lanes), 1)
                    s = jnp.where(col <= row, s, -jnp.inf)
                chunks.append((c, s))
            mc = chunks[0][1]
            for _, s in chunks[1:]:
                mc = jnp.maximum(mc, s)
            m_prev = m_ref[h, rsl, :]
            m_new = jnp.maximum(m_prev, jnp.max(mc, axis=-1, keepdims=True))
            alpha = jnp.exp2(m_prev - m_new)
            psum = None
            for c, s in chunks:
                p = jnp.exp2(s - m_new)
                p_ref[rsl, c * lanes:(c + 1) * lanes] = p.astype(BF16)
                psum = p if psum is None else psum + p
            for c in range(chunks[-1][0] + 1, tk // lanes):
                p_ref[rsl, c * lanes:(c + 1) * lanes] = jnp.zeros((rs, lanes), BF16)
            l_ref[h, rsl, :] = alpha * l_ref[h, rsl, :] + psum
            m_ref[h, rsl, :] = m_new
            a_ref[rsl, :] = alpha

    def all_heads(masked):
        scores(0, s0_ref)
        p1_ref[...] = jnp.zeros(p1_ref.shape, BF16)
        a1_ref[...] = jnp.ones(a1_ref.shape, F32)

        def two_heads(i, carry):
            h0 = 2 * i
            h1 = h0 + 1
            scores(h1, s1_ref)
            softmax(h0, s0_ref, p0_ref, a0_ref, masked)
            weighted_values(lax.rem(h0 + FOX_HEADS - 1, FOX_HEADS), p1_ref, a1_ref)
            scores(lax.rem(h0 + 2, FOX_HEADS), s0_ref)
            softmax(h1, s1_ref, p1_ref, a1_ref, masked)
            weighted_values(h0, p0_ref, a0_ref)
            return carry

        lax.fori_loop(0, FOX_HEADS // 2, two_heads, 0)
        weighted_values(FOX_HEADS - 1, p1_ref, a1_ref)

    @pl.when(ki < qi)
    def _():
        all_heads(False)

    @pl.when(ki == qi)
    def _():
        all_heads(True)
        for h in range(FOX_HEADS):
            sl = slice(h * lanes, (h + 1) * lanes)
            l_tot = jnp.sum(l_ref[h], axis=-1, keepdims=True)
            o_ref[:, sl] = (acc_ref[:, sl] / l_tot).astype(o_ref.dtype)


def _fox_prompt(q, kb, vb, c2, batch, seq, tq):
    nq = seq // tq
    width = q.shape[1]
    ck = jnp.transpose(c2, (0, 2, 1))[:, :, None, :]
    pairs = [(i, j) for i in range(nq) for j in range(i + 1)]
    qi_tab = jnp.asarray([p[0] for p in pairs], jnp.int32)
    ki_tab = jnp.asarray([p[1] for p in pairs], jnp.int32)
    q_map = lambda b, p, qt, kt: (b * nq + qt[p], 0)
    k_map = lambda b, p, qt, kt: (b * nq + kt[p], 0)
    grid_spec = pltpu.PrefetchScalarGridSpec(
        num_scalar_prefetch=2,
        grid=(batch, len(pairs)),
        in_specs=[
            pl.BlockSpec((tq, width), q_map),
            pl.BlockSpec((tq, width), k_map),
            pl.BlockSpec((tq, width), k_map),
            pl.BlockSpec((1, FOX_HEADS, 1, tq), lambda b, p, qt, kt: (b, 0, 0, kt[p])),
        ],
        out_specs=pl.BlockSpec((tq, width), q_map),
        scratch_shapes=[
            pltpu.VMEM((FOX_HEADS, tq, FOX_HEAD_DIM), F32),
            pltpu.VMEM((FOX_HEADS, tq, FOX_HEAD_DIM), F32),
            pltpu.VMEM((tq, width), F32),
            pltpu.VMEM((tq, tq), F32),
            pltpu.VMEM((tq, tq), F32),
            pltpu.VMEM((tq, tq), BF16),
            pltpu.VMEM((tq, tq), BF16),
            pltpu.VMEM((tq, FOX_HEAD_DIM), F32),
            pltpu.VMEM((tq, FOX_HEAD_DIM), F32),
        ],
    )
    return pl.pallas_call(
        functools.partial(_fox_prompt_kernel, rs=min(64, tq)),
        out_shape=jax.ShapeDtypeStruct(q.shape, BF16),
        grid_spec=grid_spec,
        compiler_params=_cparams(("parallel", "arbitrary")),
        name="fox_prompt",
    )(qi_tab, ki_tab, q, kb, vb, ck)


def _fox_sample_kernel(q_ref, kc_ref, vc_ref, kn_ref, vn_ref, ckc_ref, ckn_ref, o_ref, m_ref, l_ref, acc_ref):
    j = pl.program_id(1)
    tq = q_ref.shape[0]
    tk = kc_ref.shape[1] // FOX_HEADS
    lanes = FOX_HEAD_DIM

    @pl.when(j == 0)
    def _():
        _attn_init(m_ref, l_ref, acc_ref)

    heads = range(FOX_HEADS)
    head_lanes = [slice(h * lanes, (h + 1) * lanes) for h in heads]

    def update(ks, vs, cks, masked):
        ss = [lax.dot_general(q_ref[:, head_lanes[h]], ks[h], _NT, preferred_element_type=F32) - cks[h]
              for h in heads]
        if masked:
            row = lax.broadcasted_iota(jnp.int32, ss[0].shape, 0)
            col = lax.broadcasted_iota(jnp.int32, ss[0].shape, 1)
            ss = [jnp.where(col <= row, s, -jnp.inf) for s in ss]
        m_prev = [m_ref[h] for h in heads]
        m_new = [jnp.maximum(m_prev[h], jnp.max(ss[h], axis=-1, keepdims=True)) for h in heads]
        alpha = [jnp.exp2(m_prev[h] - m_new[h]) for h in heads]
        ps = [jnp.exp2(ss[h] - m_new[h][:, 0:1]) for h in heads]
        pvs = [jnp.dot(ps[h].astype(BF16), vs[h], preferred_element_type=F32) for h in heads]
        for h in heads:
            l_ref[h] = alpha[h] * l_ref[h] + jnp.sum(ps[h], axis=-1, keepdims=True)
            acc_ref[:, head_lanes[h]] = alpha[h] * acc_ref[:, head_lanes[h]] + pvs[h]
            m_ref[h] = m_new[h]

    update([kc_ref[0, pl.ds(h, tk, stride=FOX_HEADS), :].astype(BF16) for h in heads],
           [vc_ref[0, pl.ds(h, tk, stride=FOX_HEADS), :].astype(BF16) for h in heads],
           [ckc_ref[0, h] for h in heads], False)

    @pl.when(j == pl.num_programs(1) - 1)
    def _():
        update([kn_ref[:, sl] for sl in head_lanes], [vn_ref[:, sl] for sl in head_lanes],
               [ckn_ref[0, h] for h in heads], True)
        for h in heads:
            sl = head_lanes[h]
            o_ref[:, sl] = (acc_ref[:, sl] / l_ref[h]).astype(o_ref.dtype)


def _fox_sample(q, kb, vb, cache_k, cache_v, c2_all, batch, seq, tk):
    past = cache_k.shape[1] // FOX_HEADS
    width = q.shape[1]
    ct = jnp.transpose(c2_all, (0, 2, 1))
    ckc = ct[:, :, None, :past]
    ckn = ct[:, :, None, past:]
    new_map = lambda b, j: (b, 0)
    cache_spec = pl.BlockSpec((1, tk * FOX_HEADS, FOX_HEAD_DIM), lambda b, j: (b, j, 0))
    return pl.pallas_call(
        _fox_sample_kernel,
        out_shape=jax.ShapeDtypeStruct(q.shape, BF16),
        grid=(batch, past // tk),
        in_specs=[
            pl.BlockSpec((seq, width), new_map),
            cache_spec,
            cache_spec,
            pl.BlockSpec((seq, width), new_map),
            pl.BlockSpec((seq, width), new_map),
            pl.BlockSpec((1, FOX_HEADS, 1, tk), lambda b, j: (b, 0, 0, j)),
            pl.BlockSpec((1, FOX_HEADS, 1, seq), lambda b, j: (b, 0, 0, 0)),
        ],
        out_specs=pl.BlockSpec((seq, width), new_map),
        scratch_shapes=[
            pltpu.VMEM((FOX_HEADS, seq, FOX_HEAD_DIM), F32),
            pltpu.VMEM((FOX_HEADS, seq, FOX_HEAD_DIM), F32),
            pltpu.VMEM((seq, width), F32),
        ],
        compiler_params=_cparams(("parallel", "arbitrary")),
        name="fox_sample",
    )(q, cache_k, cache_v, kb, vb, ckc, ckn)


_HALO = 8


def _lru_kernel(xr_ref, yg_ref, conv0_ref, h0_ref, wc_ref, bc_ref, wg_ref, ba_ref, bi_ref, lam_ref,
                o_ref, convo_ref, hlast_ref, xp_ref, a_ref, b_ref, h_ref):
    t = pl.program_id(1)
    tt, width = xr_ref.shape
    tail = CONV_WIDTH - 1
    lo = _HALO - tail

    @pl.when(t == 0)
    def _():
        xp_ref[lo:_HALO, :] = conv0_ref[0]
        h_ref[...] = h0_ref[0]

    xp_ref[_HALO:_HALO + tt, :] = xr_ref[...]
    xc = bc_ref[...] + xp_ref[lo:lo + tt, :] * wc_ref[0:1, :]
    for j in range(1, CONV_WIDTH):
        xc = xc + xp_ref[lo + j:lo + j + tt, :] * wc_ref[j:j + 1, :]
    new_tail = xp_ref[lo + tt:_HALO + tt, :]
    xp_ref[lo:_HALO, :] = new_tail

    xcb = xc.astype(BF16)
    gw = wg_ref.shape[1]
    lam = lam_ref[...]
    neg_sp = -(jnp.maximum(-lam, 0.0) + jnp.log1p(jnp.exp(-jnp.abs(lam))))
    for g in range(width // gw):
        sl = slice(g * gw, (g + 1) * gw)
        z = jnp.dot(xcb[:, sl], wg_ref[g], preferred_element_type=F32)
        r = jax.nn.sigmoid(z[:, :gw] + ba_ref[:, sl])
        i = jax.nn.sigmoid(z[:, gw:] + bi_ref[:, sl])
        log_a = LRU_C * r * neg_sp[:, sl]
        a = jnp.exp(log_a)
        a_ref[:, sl] = a
        b_ref[:, sl] = jnp.sqrt(-jnp.tanh(log_a) * (a * a + 1.0)) * i * xc[:, sl]

    def step(s, h):
        row = pl.ds(s, 1)
        h = a_ref[row, :] * h + b_ref[row, :]
        b_ref[row, :] = h
        return h

    h = lax.fori_loop(0, tt, step, h_ref[...], unroll=8)
    h_ref[...] = h
    o_ref[...] = (b_ref[...] * jax.nn.gelu(yg_ref[...])).astype(o_ref.dtype)

    @pl.when(t == pl.num_programs(1) - 1)
    def _():
        convo_ref[0] = new_tail
        hlast_ref[0] = h


def _lru(xr, yg, conv0, h0, w_conv, b_conv, w_gate, b_a, b_i, lam, batch, seq, tt):
    width = xr.shape[1]
    nt = seq // tt
    gw = w_gate.shape[1]
    rows = lambda b, t: (b * nt + t, 0)
    const2 = lambda b, t: (0, 0)
    per_b = lambda b, t: (b, 0, 0)
    tail = CONV_WIDTH - 1
    return pl.pallas_call(
        _lru_kernel,
        out_shape=[
            jax.ShapeDtypeStruct((batch * seq, width), BF16),
            jax.ShapeDtypeStruct((batch, tail, width), F32),
            jax.ShapeDtypeStruct((batch, 1, width), F32),
        ],
        grid=(batch, nt),
        in_specs=[
            pl.BlockSpec((tt, width), rows),
            pl.BlockSpec((tt, width), rows),
            pl.BlockSpec((1, tail, width), per_b),
            pl.BlockSpec((1, 1, width), per_b),
            pl.BlockSpec((CONV_WIDTH, width), const2),
            pl.BlockSpec((1, width), const2),
            pl.BlockSpec((width // gw, gw, 2 * gw), lambda b, t: (0, 0, 0)),
            pl.BlockSpec((1, width), const2),
            pl.BlockSpec((1, width), const2),
            pl.BlockSpec((1, width), const2),
        ],
        out_specs=[
            pl.BlockSpec((tt, width), rows),
            pl.BlockSpec((1, tail, width), per_b),
            pl.BlockSpec((1, 1, width), per_b),
        ],
        scratch_shapes=[
            pltpu.VMEM((_HALO + tt, width), F32),
            pltpu.VMEM((tt, width), F32),
            pltpu.VMEM((tt, width), F32),
            pltpu.VMEM((1, width), F32),
        ],
        compiler_params=_cparams(("parallel", "arbitrary")),
        name="lru",
    )(xr, yg, conv0, h0, w_conv, b_conv, w_gate, b_a, b_i, lam)


def _mix_norm_kernel(*refs, aliased, n_tiles):
    ins = refs[:11]
    outs = refs[12:] if aliased else refs[11:]
    live = pl.program_id(0) < n_tiles

    @pl.when(live)
    def _():
        _mix_norm_body(*ins, *outs)

    @pl.when(jnp.logical_not(live))
    def _():
        outs[1][...] = jnp.zeros(outs[1].shape, outs[1].dtype)


def _mix_norm_body(fox_ref, lru_ref, x_ref, wt_ref, wb_ref, g_ref, b_ref, wrc_ref, wrh_ref, br_ref, cnt0_ref,
                   x1_ref, x1b_ref, idx_ref, gate_ref, rank_ref, cnt_ref):
    tm = x_ref.shape[0]

    mix = jnp.dot(fox_ref[...], wt_ref[...], preferred_element_type=F32)
    mix = mix + jnp.dot(lru_ref[...], wb_ref[...], preferred_element_type=F32)
    x1 = _layer_norm(DEEPNORM_ALPHA * x_ref[...] + mix, g_ref[...], b_ref[...])
    x1_ref[...] = x1
    x1_hi = x1.astype(BF16)
    x1b_ref[...] = x1_hi
    x1_lo = (x1 - x1_hi.astype(F32)).astype(BF16)
    lg2 = jnp.dot(x1_hi, wrc_ref[...], preferred_element_type=F32)
    lg = lg2[:, :N_EXPERTS] + lg2[:, N_EXPERTS:] + jnp.dot(x1_lo, wrh_ref[...], preferred_element_type=F32)
    lg = lg + br_ref[...]
    lanes = lax.broadcasted_iota(jnp.int32, lg.shape, 1)
    vals, picks = [], []
    for k in range(TOP_K):
        m = jnp.max(lg, axis=-1, keepdims=True)
        ix = jnp.min(jnp.where(lg == m, lanes, N_EXPERTS), axis=-1, keepdims=True)
        idx_ref[:, k:k + 1] = ix
        vals.append(m)
        picks.append(lanes == ix)
        lg = jnp.where(picks[-1], -jnp.inf, lg)
    es = [jnp.exp(v - vals[0]) for v in vals]
    denom = es[0] + es[1] + es[2] + es[3]
    for k in range(TOP_K):
        gate_ref[:, k:k + 1] = es[k] / denom

    @pl.when(pl.program_id(0) == 0)
    def _():
        cnt_ref[...] = cnt0_ref[...]

    sel = jnp.where(picks[0], 1.0, 0.0)
    for k in range(1, TOP_K):
        sel = sel + jnp.where(picks[k], 1.0, 0.0)
    earlier = lax.broadcasted_iota(jnp.int32, (tm, tm), 1) < lax.broadcasted_iota(jnp.int32, (tm, tm), 0)
    tri = jnp.where(earlier, 1.0, 0.0).astype(BF16)
    before = jnp.dot(tri, sel.astype(BF16), preferred_element_type=F32) + cnt_ref[...]
    for k in range(TOP_K):
        rank_ref[:, k:k + 1] = jnp.sum(jnp.where(picks[k], before, 0.0), axis=-1, keepdims=True).astype(jnp.int32)
    cnt_ref[...] += jnp.sum(sel, axis=0, keepdims=True)


def _mix_norm(fox_o, lru_o, x2d, w_top, w_bot, ln_g, ln_b, wr_cat, wr_hi, b_router, cnt0, tm, n_all, row_block0,
              x1b_buf=None):
    n, d = x2d.shape
    half = fox_o.shape[1]
    n_tiles = n // tm
    aliased = x1b_buf is not None
    steps = n_tiles if aliased else n_all // tm - row_block0
    row = lambda i: (jnp.minimum(i, n_tiles - 1), 0)
    const = lambda i: (0, 0)
    in_specs = [
        pl.BlockSpec((tm, half), row),
        pl.BlockSpec((tm, half), row),
        pl.BlockSpec((tm, d), row),
        pl.BlockSpec((half, d), const),
        pl.BlockSpec((half, d), const),
        pl.BlockSpec((1, d), const),
        pl.BlockSpec((1, d), const),
        pl.BlockSpec((d, 2 * N_EXPERTS), const),
        pl.BlockSpec((d, N_EXPERTS), const),
        pl.BlockSpec((1, N_EXPERTS), const),
        pl.BlockSpec((1, N_EXPERTS), const),
    ]
    args = [fox_o, lru_o, x2d, w_top, w_bot, ln_g, ln_b, wr_cat, wr_hi, b_router, cnt0]
    if aliased:
        in_specs.append(pl.BlockSpec(memory_space=pl.ANY))
        args.append(x1b_buf)
    return pl.pallas_call(
        functools.partial(_mix_norm_kernel, aliased=aliased, n_tiles=n_tiles),
        out_shape=[
            jax.ShapeDtypeStruct((n, d), F32),
            jax.ShapeDtypeStruct((n_all, d), BF16),
            jax.ShapeDtypeStruct((n, TOP_K), jnp.int32),
            jax.ShapeDtypeStruct((n, TOP_K), F32),
            jax.ShapeDtypeStruct((n, TOP_K), jnp.int32),
            jax.ShapeDtypeStruct((1, N_EXPERTS), F32),
        ],
        grid=(steps,),
        in_specs=in_specs,
        out_specs=[
            pl.BlockSpec((tm, d), row),
            pl.BlockSpec((tm, d), lambda i: (row_block0 + i, 0)),
            pl.BlockSpec((tm, TOP_K), row),
            pl.BlockSpec((tm, TOP_K), row),
            pl.BlockSpec((tm, TOP_K), row),
            pl.BlockSpec((1, N_EXPERTS), const),
        ],
        input_output_aliases={11: 1} if aliased else {},
        compiler_params=_cparams(("arbitrary",)),
        name="mix_norm",
    )(*args)


_RING = 6
_AHEAD = 4


def _expert_kernel(be_ref, bv_ref, nu_ref, x_ref, bgu_ref, bd_ref, wgu_hbm, wd_hbm, o_ref,
                   act_ref, ring_ref, sem_ref, *, sub, tc):
    rb = pl.program_id(0)
    valid = bv_ref[rb]
    tm, d = x_ref.shape
    d_ff = act_ref.shape[1]
    nfc = d_ff // tc
    nnc = d // tc
    per_block = 2 * nfc + nnc
    total = nu_ref[0] * per_block

    def chunk_copy(src, slot):
        return pltpu.make_async_copy(src, ring_ref.at[slot], sem_ref.at[slot])

    def start(cg):
        @pl.when(cg < total)
        def _():
            blk = cg // per_block
            j = cg - blk * per_block
            e = be_ref[blk]
            slot = lax.rem(cg, _RING)

            @pl.when(j < 2 * nfc)
            def _():
                col = pl.multiple_of((lax.rem(j, 2) * nfc + j // 2) * tc, tc)
                chunk_copy(wgu_hbm.at[e, :, pl.ds(col, tc)], slot).start()

            @pl.when(j >= 2 * nfc)
            def _():
                col = pl.multiple_of((j - 2 * nfc) * tc, tc)
                chunk_copy(wd_hbm.at[e, :, pl.ds(col, tc)], slot).start()

    def wait(cg):
        slot = lax.rem(cg, _RING)
        chunk_copy(wgu_hbm.at[0, :, pl.ds(0, tc)], slot).wait()
        return slot

    @pl.when(rb == 0)
    def _():
        for c in range(_AHEAD):
            start(c)

    base = rb * per_block

    def block(nrows):
        def up(j, carry):
            cg = base + 2 * j
            start(cg + _AHEAD)
            start(cg + 1 + _AHEAD)
            gate_slot = wait(cg)
            up_slot = wait(cg + 1)
            col = pl.ds(pl.multiple_of(j * tc, tc), tc)
            ucol = pl.ds(pl.multiple_of(d_ff + j * tc, tc), tc)
            x = x_ref[0:nrows, :]
            hg = jnp.dot(x, ring_ref[gate_slot].astype(BF16), preferred_element_type=F32) + bgu_ref[0, :, col]
            hu = jnp.dot(x, ring_ref[up_slot].astype(BF16), preferred_element_type=F32) + bgu_ref[0, :, ucol]
            gate = jnp.minimum(hg, SWIGLU_LIMIT)
            upv = jnp.clip(hu, -SWIGLU_LIMIT, SWIGLU_LIMIT)
            act = (upv + 1.0) * gate * jax.nn.sigmoid(SWIGLU_ALPHA * gate)
            act_ref[0:nrows, col] = act.astype(BF16)
            return carry

        lax.fori_loop(0, nfc, up, 0)

        def down(n, carry):
            cg = base + 2 * nfc + n
            start(cg + _AHEAD)
            slot = wait(cg)
            col = pl.ds(pl.multiple_of(n * tc, tc), tc)
            y = jnp.dot(act_ref[0:nrows, :], ring_ref[slot].astype(BF16), preferred_element_type=F32)
            o_ref[0:nrows, col] = (y + bd_ref[0, :, col]).astype(o_ref.dtype)
            return carry

        lax.fori_loop(0, nnc, down, 0)
        if nrows < tm:
            o_ref[nrows:tm, :] = jnp.zeros((tm - nrows, d), o_ref.dtype)

    for nrows in range(sub, tm + 1, sub):
        @pl.when(jnp.logical_and(valid > nrows - sub, valid <= nrows))
        def _(nrows=nrows):
            block(nrows)

    @pl.when(valid == 0)
    def _():
        o_ref[...] = jnp.zeros(o_ref.shape, o_ref.dtype)


def _experts(x_pad, block_e, block_valid, n_used, w_gu, b_gu, w_down, b_down, tm, tc, sub):
    p, d = x_pad.shape
    d_ff = w_down.shape[1]
    assert d_ff == d and _AHEAD <= _RING - 2
    nb = p // tm

    def rbc(rb, nu):
        return jnp.minimum(rb, nu[0] - 1)

    grid_spec = pltpu.PrefetchScalarGridSpec(
        num_scalar_prefetch=3,
        grid=(nb,),
        in_specs=[
            pl.BlockSpec((tm, d), lambda rb, be, bv, nu: (rbc(rb, nu), 0)),
            pl.BlockSpec((1, 1, 2 * d_ff), lambda rb, be, bv, nu: (be[rbc(rb, nu)], 0, 0)),
            pl.BlockSpec((1, 1, d), lambda rb, be, bv, nu: (be[rbc(rb, nu)], 0, 0)),
            pl.BlockSpec(memory_space=pl.ANY),
            pl.BlockSpec(memory_space=pl.ANY),
        ],
        out_specs=pl.BlockSpec((tm, d), lambda rb, be, bv, nu: (rb, 0)),
        scratch_shapes=[
            pltpu.VMEM((tm, d_ff), BF16),
            pltpu.VMEM((_RING, d, tc), F32),
            pltpu.SemaphoreType.DMA((_RING,)),
        ],
    )
    return pl.pallas_call(
        functools.partial(_expert_kernel, sub=sub, tc=tc),
        out_shape=jax.ShapeDtypeStruct((p, d), BF16),
        grid_spec=grid_spec,
        compiler_params=_cparams(("arbitrary",)),
        name="experts",
    )(block_e, block_valid, n_used, x_pad, b_gu, b_down, w_gu, w_down)


def _combine_kernel(y_ref, gate_ref, x1_ref, g_ref, b_ref, o_ref):
    y = y_ref[0].astype(F32) * gate_ref[:, 0:1]
    for k in range(1, TOP_K):
        y = y + y_ref[k].astype(F32) * gate_ref[:, k:k + 1]
    o_ref[...] = _layer_norm(DEEPNORM_ALPHA * x1_ref[...] + y, g_ref[...], b_ref[...])


def _combine(y_rows, gates, x1, ln_g, ln_b, row_block0, tn):
    n, d = x1.shape
    const = lambda i: (0, 0)
    return pl.pallas_call(
        _combine_kernel,
        out_shape=jax.ShapeDtypeStruct((n, d), F32),
        grid=(n // tn,),
        in_specs=[
            pl.BlockSpec((TOP_K, tn, d), lambda i: (0, row_block0 + i, 0)),
            pl.BlockSpec((tn, TOP_K), lambda i: (i, 0)),
            pl.BlockSpec((tn, d), lambda i: (i, 0)),
            pl.BlockSpec((1, d), const),
            pl.BlockSpec((1, d), const),
        ],
        out_specs=pl.BlockSpec((tn, d), lambda i: (i, 0)),
        compiler_params=_cparams(("parallel",)),
        name="combine",
    )(y_rows, gates, x1, ln_g, ln_b)


def _route(idx, rank, sizes, tm):
    n = idx.shape[0]
    nk = n * TOP_K
    nblk = (sizes + tm - 1) // tm
    bends = jnp.cumsum(nblk)
    bstart = bends - nblk
    dest = bstart[idx] * tm + rank
    nb = -(-nk // tm) + N_EXPERTS
    blk = jnp.arange(nb, dtype=jnp.int32)
    block_e = jnp.minimum(jnp.sum(bends[None, :] <= blk[:, None], axis=1), N_EXPERTS - 1).astype(jnp.int32)
    n_used = bends[-1].astype(jnp.int32)
    valid = jnp.clip(sizes[block_e] - (blk - bstart[block_e]) * tm, 0, tm)
    valid = jnp.where(blk < n_used, valid, 0).astype(jnp.int32)
    tok = jnp.broadcast_to(jnp.arange(n, dtype=jnp.int32)[:, None], (n, TOP_K))
    keys = jnp.sort((idx * n + tok).reshape(-1))
    tok_sorted = keys - (keys // n) * n
    row = jnp.arange(nb * tm, dtype=jnp.int32)
    row_e = jnp.repeat(block_e, tm)
    off = row - bstart[row_e] * tm
    first = jnp.cumsum(sizes) - sizes
    src = jnp.clip(first[row_e] + off, 0, nk - 1)
    row_tok = jnp.where(off < sizes[row_e], tok_sorted[src], row % n)
    return dest, row_tok, block_e, valid, n_used.reshape(1)


def _pick(n, pref):
    t = min(n, pref)
    while n % t:
        t //= 2
    return t


def kernel(x_prompt, x_sample, cache_k, cache_v, cache_logf, state_conv, state_lru, w_in, b_f, w_conv, b_conv, w_a, b_a, w_i, b_i, lam, w_out, ln1_g, ln1_b, w_router, b_router, w_gu, b_gu, w_down, b_down, ln2_g, ln2_b):
    assert w_in.shape[0] == DEPTH
    bp, tp, d = x_prompt.shape
    bs, ts, _ = x_sample.shape
    past = cache_k.shape[2]
    lru_w = w_conv.shape[-1]
    np_, ns = bp * tp, bs * ts

    win = w_in[0]
    f0 = 3 * FOX_WIDTH
    w_main = jnp.concatenate([win[:, :f0], win[:, f0 + FOX_HEADS:]], axis=1).astype(BF16)
    w_f = jnp.pad(win[:, f0:f0 + FOX_HEADS], ((0, 0), (0, 128 - FOX_HEADS))).astype(BF16)
    bf2 = b_f[0].reshape(1, FOX_HEADS)
    gpb = 4
    bd = w_a.shape[-1]
    eye = jnp.eye(gpb, dtype=F32)

    def blockdiag(w):
        wg = w.reshape(LRU_BLOCKS // gpb, gpb, bd, bd)
        return jnp.einsum("gacd,ab->gacbd", wg, eye).reshape(LRU_BLOCKS // gpb, gpb * bd, gpb * bd)

    w_gate = jnp.concatenate([blockdiag(w_a[0]), blockdiag(w_i[0])], axis=-1).astype(BF16)
    w_top = w_out[0, :FOX_WIDTH].astype(BF16)
    w_bot = w_out[0, FOX_WIDTH:].astype(BF16)
    wr = w_router[0]
    wr_hi = wr.astype(BF16)
    wr_cat = jnp.concatenate([wr_hi, (wr - wr_hi.astype(F32)).astype(BF16)], axis=1)
    row = lambda a: a.reshape(1, -1)

    n_all = np_ + ns
    tm_mix = _pick(ns, 512)

    def mixers(x, batch, seq, fox_fn, conv0, h0, cnt0, x1b_buf, row0):
        n = batch * seq
        q, k, v, kb, vb, xr, yg, logf = _in_proj(x.reshape(n, d), w_main, w_f, bf2, _pick(n, 512))
        fox_o = fox_fn(q, kb, vb, logf.reshape(batch, seq, FOX_HEADS))
        lru_o, conv_new, h_last = _lru(xr, yg, conv0, h0, w_conv[0], row(b_conv[0]), w_gate,
                                       row(b_a[0]), row(b_i[0]), row(lam[0]), batch, seq, _pick(seq, 256))
        routed = _mix_norm(fox_o, lru_o, x.reshape(n, d), w_top, w_bot, row(ln1_g[0]), row(ln1_b[0]),
                           wr_cat, wr_hi, row(b_router[0]), cnt0, tm_mix, n_all, row0 // tm_mix, x1b_buf)
        state = (k.reshape(1, batch, seq, FOX_HEADS, FOX_HEAD_DIM), v.reshape(1, batch, seq, FOX_HEADS, FOX_HEAD_DIM),
                 logf.reshape(1, batch, seq, FOX_HEADS), conv_new[None], h_last.reshape(1, batch, lru_w))
        return routed, state

    def fox_p(q, kb, vb, logf):
        c = _cumsum_time(logf, jnp.zeros((bp, 1, FOX_HEADS), F32))
        return _fox_prompt(q, kb, vb, c * LOG2E, bp, tp, _pick(tp, 512))

    def fox_s(q, kb, vb, logf):
        c_past = _cumsum_time(cache_logf[0], jnp.zeros((bs, 1, FOX_HEADS), F32))
        c_new = _cumsum_time(logf, c_past[:, -1:, :])
        c_all = jnp.concatenate([c_past, c_new], axis=1) * LOG2E
        cache_rows = lambda c: c[0].reshape(bs, past * FOX_HEADS, FOX_HEAD_DIM)
        return _fox_sample(q, kb, vb, cache_rows(cache_k), cache_rows(cache_v), c_all, bs, ts, _pick(past, 512))

    (x1p, x1b, idxp, gp, rankp, cnt_p), state_p = mixers(
        x_prompt, bp, tp, fox_p, jnp.zeros((bp, CONV_WIDTH - 1, lru_w), F32), jnp.zeros((bp, 1, lru_w), F32),
        jnp.zeros((1, N_EXPERTS), F32), None, 0)
    (x1s, x1b, idxs, gs, ranks, cnt_all), state_s = mixers(
        x_sample, bs, ts, fox_s, state_conv[0], state_lru[0].reshape(bs, 1, lru_w), cnt_p, x1b, np_)

    tm_e, tc_e, sub_e = 1024, 512, 256
    idx = jnp.concatenate([idxp, idxs], axis=0)
    rank = jnp.concatenate([rankp, ranks], axis=0)
    dest, row_tok, block_e, block_valid, n_used = _route(idx, rank, cnt_all[0].astype(jnp.int32), tm_e)
    x_pad = x1b[row_tok]
    y_pad = _experts(x_pad, block_e, block_valid, n_used, w_gu[0], b_gu[0].reshape(N_EXPERTS, 1, -1),
                     w_down[0], b_down[0].reshape(N_EXPERTS, 1, -1), tm_e, tc_e, sub_e)
    y_rows = y_pad[dest.T]

    tn = _pick(ns, 512)
    yp = _combine(y_rows, gp, x1p, row(ln2_g[0]), row(ln2_b[0]), 0, tn)
    ys = _combine(y_rows, gs, x1s, row(ln2_g[0]), row(ln2_b[0]), np_ // tn, tn)
    return (yp.reshape(bp, tp, d), ys.reshape(bs, ts, d)) + state_p + state_s
```

```python
import functools

import jax
import jax.numpy as jnp
from jax import lax
from jax.experimental import pallas as pl
from jax.experimental.pallas import tpu as pltpu

F32 = jnp.float32
BF16 = jnp.bfloat16

FOX_HEADS = 8
FOX_HEAD_DIM = 128
FOX_WIDTH = FOX_HEADS * FOX_HEAD_DIM
LRU_BLOCKS = 16
CONV_WIDTH = 4
LRU_C = 8.0
N_EXPERTS = 32
TOP_K = 4
SWIGLU_LIMIT = 7.0
SWIGLU_ALPHA = 1.702
LN_EPS = 1e-5
DEPTH = 1
DEEPNORM_ALPHA = (2.0 * DEPTH) ** 0.25
LOG2E = 1.4426950408889634
Q_SCALE = FOX_HEAD_DIM ** -0.5 * LOG2E

VMEM_LIMIT = 56 * 1024 * 1024


def _cparams(sem):
    return pltpu.CompilerParams(dimension_semantics=sem, vmem_limit_bytes=VMEM_LIMIT)


def _log_sigmoid(x):
    return jnp.minimum(x, 0.0) - jnp.log1p(jnp.exp(-jnp.abs(x)))


def _layer_norm(z, g, b):
    mu = jnp.mean(z, axis=-1, keepdims=True)
    zc = z - mu
    var = jnp.mean(zc * zc, axis=-1, keepdims=True)
    return zc * lax.rsqrt(var + LN_EPS) * g + b


def _in_proj_kernel(x_ref, w_ref, wf_ref, bf_ref, q_ref, k_ref, v_ref, kb_ref, vb_ref, xr_ref, yg_ref, lf_ref):
    wcol = q_ref.shape[1]
    xb = x_ref[...].astype(BF16)
    zf = jnp.dot(xb, wf_ref[...], preferred_element_type=F32)
    lf_ref[...] = _log_sigmoid(zf[:, :FOX_HEADS] + bf_ref[...])

    def proj(j):
        return jnp.dot(xb, w_ref[:, j * wcol:(j + 1) * wcol], preferred_element_type=F32)

    q_ref[...] = (proj(0) * Q_SCALE).astype(BF16)
    z = proj(1)
    k_ref[...] = z
    kb_ref[...] = z.astype(BF16)
    z = proj(2)
    v_ref[...] = z
    vb_ref[...] = z.astype(BF16)
    xr_ref[...] = proj(3)
    yg_ref[...] = proj(4)


def _in_proj(x2d, w_main, w_f, b_f, tm):
    n, d = x2d.shape
    wcol = FOX_WIDTH
    row = lambda i: (i, 0)
    const = lambda i: (0, 0)
    dts = (BF16, F32, F32, BF16, BF16, F32, F32)
    outs = [jax.ShapeDtypeStruct((n, wcol), dt) for dt in dts]
    outs.append(jax.ShapeDtypeStruct((n, FOX_HEADS), F32))
    return pl.pallas_call(
        _in_proj_kernel,
        out_shape=outs,
        grid=(n // tm,),
        in_specs=[
            pl.BlockSpec((tm, d), row),
            pl.BlockSpec(w_main.shape, const, pipeline_mode=pl.Buffered(1)),
            pl.BlockSpec((d, 128), const, pipeline_mode=pl.Buffered(1)),
            pl.BlockSpec((1, FOX_HEADS), const),
        ],
        out_specs=[pl.BlockSpec((tm, wcol), row)] * len(dts) + [pl.BlockSpec((tm, FOX_HEADS), row)],
        compiler_params=_cparams(("parallel",)),
        name="in_proj",
    )(x2d, w_main, w_f, b_f)


def _cumsum_kernel(s_ref, c0_ref, o_ref, tri_ref):
    t = s_ref.shape[1]
    tc = tri_ref.shape[0]
    h = FOX_HEADS

    @pl.when(pl.program_id(0) == 0)
    def _():
        row = lax.broadcasted_iota(jnp.int32, (tc, tc), 0)
        col = lax.broadcasted_iota(jnp.int32, (tc, tc), 1)
        tri_ref[...] = jnp.where(col <= row, 1.0, 0.0).astype(BF16)

    carry = c0_ref[0]
    for c in range(t // tc):
        rows = slice(c * tc, (c + 1) * tc)
        c3 = jnp.dot(tri_ref[...], s_ref[0, rows, :], preferred_element_type=F32)
        out = c3[:, 0:h] + c3[:, h:2 * h] + c3[:, 2 * h:3 * h] + carry
        o_ref[0, rows, :] = out
        carry = out[tc - 1:tc, :]


def _cumsum_time(x, c0):
    b, t, h = x.shape
    hi = x.astype(BF16)
    r1 = x - hi.astype(F32)
    mid = r1.astype(BF16)
    lo = (r1 - mid.astype(F32)).astype(BF16)
    pieces = jnp.concatenate([hi, mid, lo, jnp.zeros((b, t, 128 - 3 * h), BF16)], axis=-1)
    return pl.pallas_call(
        _cumsum_kernel,
        out_shape=jax.ShapeDtypeStruct((b, t, h), F32),
        grid=(b,),
        in_specs=[
            pl.BlockSpec((1, t, 128), lambda i: (i, 0, 0)),
            pl.BlockSpec((1, 1, h), lambda i: (i, 0, 0)),
        ],
        out_specs=pl.BlockSpec((1, t, h), lambda i: (i, 0, 0)),
        scratch_shapes=[pltpu.VMEM((_pick(t, 256), _pick(t, 256)), BF16)],
        compiler_params=_cparams(("arbitrary",)),
        name="cumsum_time",
    )(pieces, c0)


_NT = (((1,), (1,)), ((), ()))


def _attn_init(m_ref, l_ref, acc_ref):
    m_ref[...] = jnp.full(m_ref.shape, -jnp.inf, F32)
    l_ref[...] = jnp.zeros(l_ref.shape, F32)
    acc_ref[...] = jnp.zeros(acc_ref.shape, F32)


def _fox_prompt_kernel(qi_ref, ki_ref, q_ref, k_ref, v_ref, ck_ref, o_ref,
                       m_ref, l_ref, acc_ref, s0_ref, s1_ref, p0_ref, p1_ref, a0_ref, a1_ref, *, rs):
    pair = pl.program_id(1)
    qi = qi_ref[pair]
    ki = ki_ref[pair]
    tq = q_ref.shape[0]
    tk = k_ref.shape[0]
    lanes = FOX_HEAD_DIM

    @pl.when(ki == 0)
    def _():
        _attn_init(m_ref, l_ref, acc_ref)

    def head_lanes(h):
        return pl.ds(pl.multiple_of(h * lanes, lanes), lanes)

    def scores(h, s_ref):
        sl = head_lanes(h)
        s_ref[...] = lax.dot_general(q_ref[:, sl], k_ref[:, sl], _NT, preferred_element_type=F32)

    def weighted_values(h, p_ref, a_ref):
        sl = head_lanes(h)
        pv = jnp.dot(p_ref[...], v_ref[:, sl], preferred_element_type=F32)
        acc_ref[:, sl] = a_ref[...] * acc_ref[:, sl] + pv

    def softmax(h, s_ref, p_ref, a_ref, masked):
        ck = ck_ref[0, h]
        for r in range(tq // rs):
            r0 = r * rs
            rsl = slice(r0, r0 + rs)
            chunks = []
            for c in range(tk // lanes):
                if masked and c * lanes > r0 + rs - 1:
                    continue
                s = s_ref[rsl, c * lanes:(c + 1) * lanes] - ck[:, c * lanes:(c + 1) * lanes]
                if masked and (c + 1) * lanes - 1 > r0:
                    row = r0 + lax.broadcasted_iota(jnp.int32, (rs, lanes), 0)
                    col = c * lanes + lax.broadcasted_iota(jnp.int32, (rs, lanes), 1)
                    s = jnp.where(col <= row, s, -jnp.inf)
                chunks.append((c, s))
            mc = chunks[0][1]
            for _, s in chunks[1:]:
                mc = jnp.maximum(mc, s)
            m_prev = m_ref[h, rsl, :]
            m_new = jnp.maximum(m_prev, jnp.max(mc, axis=-1, keepdims=True))
            alpha = jnp.exp2(m_prev - m_new)
            psum = None
            for c, s in chunks:
                p = jnp.exp2(s - m_new)
                p_ref[rsl, c * lanes:(c + 1) * lanes] = p.astype(BF16)
                psum = p if psum is None else psum + p
            for c in range(chunks[-1][0] + 1, tk // lanes):
                p_ref[rsl, c * lanes:(c + 1) * lanes] = jnp.zeros((rs, lanes), BF16)
            l_ref[h, rsl, :] = alpha * l_ref[h, rsl, :] + psum
            m_ref[h, rsl, :] = m_new
            a_ref[rsl, :] = alpha

    def all_heads(masked):
        scores(0, s0_ref)
        p1_ref[...] = jnp.zeros(p1_ref.shape, BF16)
        a1_ref[...] = jnp.ones(a1_ref.shape, F32)

        def two_heads(i, carry):
            h0 = 2 * i
            h1 = h0 + 1
            scores(h1, s1_ref)
            softmax(h0, s0_ref, p0_ref, a0_ref, masked)
            weighted_values(lax.rem(h0 + FOX_HEADS - 1, FOX_HEADS), p1_ref, a1_ref)
            scores(lax.rem(h0 + 2, FOX_HEADS), s0_ref)
            softmax(h1, s1_ref, p1_ref, a1_ref, masked)
            weighted_values(h0, p0_ref, a0_ref)
            return carry

        lax.fori_loop(0, FOX_HEADS // 2, two_heads, 0)
        weighted_values(FOX_HEADS - 1, p1_ref, a1_ref)

    @pl.when(ki < qi)
    def _():
        all_heads(False)

    @pl.when(ki == qi)
    def _():
        all_heads(True)
        for h in range(FOX_HEADS):
            sl = slice(h * lanes, (h + 1) * lanes)
            l_tot = jnp.sum(l_ref[h], axis=-1, keepdims=True)
            o_ref[:, sl] = (acc_ref[:, sl] / l_tot).astype(o_ref.dtype)


def _fox_prompt(q, kb, vb, c2, batch, seq, tq):
    nq = seq // tq
    width = q.shape[1]
    ck = jnp.transpose(c2, (0, 2, 1))[:, :, None, :]
    pairs = [(i, j) for i in range(nq) for j in range(i + 1)]
    qi_tab = jnp.asarray([p[0] for p in pairs], jnp.int32)
    ki_tab = jnp.asarray([p[1] for p in pairs], jnp.int32)
    q_map = lambda b, p, qt, kt: (b * nq + qt[p], 0)
    k_map = lambda b, p, qt, kt: (b * nq + kt[p], 0)
    grid_spec = pltpu.PrefetchScalarGridSpec(
        num_scalar_prefetch=2,
        grid=(batch, len(pairs)),
        in_specs=[
            pl.BlockSpec((tq, width), q_map),
            pl.BlockSpec((tq, width), k_map),
            pl.BlockSpec((tq, width), k_map),
            pl.BlockSpec((1, FOX_HEADS, 1, tq), lambda b, p, qt, kt: (b, 0, 0, kt[p])),
        ],
        out_specs=pl.BlockSpec((tq, width), q_map),
        scratch_shapes=[
            pltpu.VMEM((FOX_HEADS, tq, FOX_HEAD_DIM), F32),
            pltpu.VMEM((FOX_HEADS, tq, FOX_HEAD_DIM), F32),
            pltpu.VMEM((tq, width), F32),
            pltpu.VMEM((tq, tq), F32),
            pltpu.VMEM((tq, tq), F32),
            pltpu.VMEM((tq, tq), BF16),
            pltpu.VMEM((tq, tq), BF16),
            pltpu.VMEM((tq, FOX_HEAD_DIM), F32),
            pltpu.VMEM((tq, FOX_HEAD_DIM), F32),
        ],
    )
    return pl.pallas_call(
        functools.partial(_fox_prompt_kernel, rs=min(64, tq)),
        out_shape=jax.ShapeDtypeStruct(q.shape, BF16),
        grid_spec=grid_spec,
        compiler_params=_cparams(("parallel", "arbitrary")),
        name="fox_prompt",
    )(qi_tab, ki_tab, q, kb, vb, ck)


def _fox_sample_kernel(q_ref, kc_ref, vc_ref, kn_ref, vn_ref, ckc_ref, ckn_ref, o_ref, m_ref, l_ref, acc_ref):
    j = pl.program_id(1)
    tq = q_ref.shape[0]
    tk = kc_ref.shape[1] // FOX_HEADS
    lanes = FOX_HEAD_DIM

    @pl.when(j == 0)
    def _():
        _attn_init(m_ref, l_ref, acc_ref)

    heads = range(FOX_HEADS)
    head_lanes = [slice(h * lanes, (h + 1) * lanes) for h in heads]

    def update(ks, vs, cks, masked):
        ss = [lax.dot_general(q_ref[:, head_lanes[h]], ks[h], _NT, preferred_element_type=F32) - cks[h]
              for h in heads]
        if masked:
            row = lax.broadcasted_iota(jnp.int32, ss[0].shape, 0)
            col = lax.broadcasted_iota(jnp.int32, ss[0].shape, 1)
            ss = [jnp.where(col <= row, s, -jnp.inf) for s in ss]
        m_prev = [m_ref[h] for h in heads]
        m_new = [jnp.maximum(m_prev[h], jnp.max(ss[h], axis=-1, keepdims=True)) for h in heads]
        alpha = [jnp.exp2(m_prev[h] - m_new[h]) for h in heads]
        ps = [jnp.exp2(ss[h] - m_new[h][:, 0:1]) for h in heads]
        pvs = [jnp.dot(ps[h].astype(BF16), vs[h], preferred_element_type=F32) for h in heads]
        for h in heads:
            l_ref[h] = alpha[h] * l_ref[h] + jnp.sum(ps[h], axis=-1, keepdims=True)
            acc_ref[:, head_lanes[h]] = alpha[h] * acc_ref[:, head_lanes[h]] + pvs[h]
            m_ref[h] = m_new[h]

    update([kc_ref[0, pl.ds(h, tk, stride=FOX_HEADS), :].astype(BF16) for h in heads],
           [vc_ref[0, pl.ds(h, tk, stride=FOX_HEADS), :].astype(BF16) for h in heads],
           [ckc_ref[0, h] for h in heads], False)

    @pl.when(j == pl.num_programs(1) - 1)
    def _():
        update([kn_ref[:, sl] for sl in head_lanes], [vn_ref[:, sl] for sl in head_lanes],
               [ckn_ref[0, h] for h in heads], True)
        for h in heads:
            sl = head_lanes[h]
            o_ref[:, sl] = (acc_ref[:, sl] / l_ref[h]).astype(o_ref.dtype)


def _fox_sample(q, kb, vb, cache_k, cache_v, c2_all, batch, seq, tk):
    past = cache_k.shape[1] // FOX_HEADS
    width = q.shape[1]
    ct = jnp.transpose(c2_all, (0, 2, 1))
    ckc = ct[:, :, None, :past]
    ckn = ct[:, :, None, past:]
    new_map = lambda b, j: (b, 0)
    cache_spec = pl.BlockSpec((1, tk * FOX_HEADS, FOX_HEAD_DIM), lambda b, j: (b, j, 0))
    return pl.pallas_call(
        _fox_sample_kernel,
        out_shape=jax.ShapeDtypeStruct(q.shape, BF16),
        grid=(batch, past // tk),
        in_specs=[
            pl.BlockSpec((seq, width), new_map),
            cache_spec,
            cache_spec,
            pl.BlockSpec((seq, width), new_map),
            pl.BlockSpec((seq, width), new_map),
            pl.BlockSpec((1, FOX_HEADS, 1, tk), lambda b, j: (b, 0, 0, j)),
            pl.BlockSpec((1, FOX_HEADS, 1, seq), lambda b, j: (b, 0, 0, 0)),
        ],
        out_specs=pl.BlockSpec((seq, width), new_map),
        scratch_shapes=[
            pltpu.VMEM((FOX_HEADS, seq, FOX_HEAD_DIM), F32),
            pltpu.VMEM((FOX_HEADS, seq, FOX_HEAD_DIM), F32),
            pltpu.VMEM((seq, width), F32),
        ],
        compiler_params=_cparams(("parallel", "arbitrary")),
        name="fox_sample",
    )(q, cache_k, cache_v, kb, vb, ckc, ckn)


_HALO = 8


def _lru_kernel(xr_ref, yg_ref, conv0_ref, h0_ref, wc_ref, bc_ref, wg_ref, ba_ref, bi_ref, lam_ref,
                o_ref, convo_ref, hlast_ref, xp_ref, a_ref, b_ref, h_ref):
    t = pl.program_id(1)
    tt, width = xr_ref.shape
    tail = CONV_WIDTH - 1
    lo = _HALO - tail

    @pl.when(t == 0)
    def _():
        xp_ref[lo:_HALO, :] = conv0_ref[0]
        h_ref[...] = h0_ref[0]

    xp_ref[_HALO:_HALO + tt, :] = xr_ref[...]
    xc = bc_ref[...] + xp_ref[lo:lo + tt, :] * wc_ref[0:1, :]
    for j in range(1, CONV_WIDTH):
        xc = xc + xp_ref[lo + j:lo + j + tt, :] * wc_ref[j:j + 1, :]
    new_tail = xp_ref[lo + tt:_HALO + tt, :]
    xp_ref[lo:_HALO, :] = new_tail

    xcb = xc.astype(BF16)
    gw = wg_ref.shape[1]
    lam = lam_ref[...]
    neg_sp = -(jnp.maximum(-lam, 0.0) + jnp.log1p(jnp.exp(-jnp.abs(lam))))
    for g in range(width // gw):
        sl = slice(g * gw, (g + 1) * gw)
        z = jnp.dot(xcb[:, sl], wg_ref[g], preferred_element_type=F32)
        r = jax.nn.sigmoid(z[:, :gw] + ba_ref[:, sl])
        i = jax.nn.sigmoid(z[:, gw:] + bi_ref[:, sl])
        log_a = LRU_C * r * neg_sp[:, sl]
        a = jnp.exp(log_a)
        a_ref[:, sl] = a
        b_ref[:, sl] = jnp.sqrt(-jnp.tanh(log_a) * (a * a + 1.0)) * i * xc[:, sl]

    def step(s, h):
        row = pl.ds(s, 1)
        h = a_ref[row, :] * h + b_ref[row, :]
        b_ref[row, :] = h
        return h

    h = lax.fori_loop(0, tt, step, h_ref[...], unroll=8)
    h_ref[...] = h
    o_ref[...] = (b_ref[...] * jax.nn.gelu(yg_ref[...])).astype(o_ref.dtype)

    @pl.when(t == pl.num_programs(1) - 1)
    def _():
        convo_ref[0] = new_tail
        hlast_ref[0] = h


def _lru(xr, yg, conv0, h0, w_conv, b_conv, w_gate, b_a, b_i, lam, batch, seq, tt):
    width = xr.shape[1]
    nt = seq // tt
    gw = w_gate.shape[1]
    rows = lambda b, t: (b * nt + t, 0)
    const2 = lambda b, t: (0, 0)
    per_b = lambda b, t: (b, 0, 0)
    tail = CONV_WIDTH - 1
    return pl.pallas_call(
        _lru_kernel,
        out_shape=[
            jax.ShapeDtypeStruct((batch * seq, width), BF16),
            jax.ShapeDtypeStruct((batch, tail, width), F32),
            jax.ShapeDtypeStruct((batch, 1, width), F32),
        ],
        grid=(batch, nt),
        in_specs=[
            pl.BlockSpec((tt, width), rows),
            pl.BlockSpec((tt, width), rows),
            pl.BlockSpec((1, tail, width), per_b),
            pl.BlockSpec((1, 1, width), per_b),
            pl.BlockSpec((CONV_WIDTH, width), const2),
            pl.BlockSpec((1, width), const2),
            pl.BlockSpec((width // gw, gw, 2 * gw), lambda b, t: (0, 0, 0)),
            pl.BlockSpec((1, width), const2),
            pl.BlockSpec((1, width), const2),
            pl.BlockSpec((1, width), const2),
        ],
        out_specs=[
            pl.BlockSpec((tt, width), rows),
            pl.BlockSpec((1, tail, width), per_b),
            pl.BlockSpec((1, 1, width), per_b),
        ],
        scratch_shapes=[
            pltpu.VMEM((_HALO + tt, width), F32),
            pltpu.VMEM((tt, width), F32),
            pltpu.VMEM((tt, width), F32),
            pltpu.VMEM((1, width), F32),
        ],
        compiler_params=_cparams(("parallel", "arbitrary")),
        name="lru",
    )(xr, yg, conv0, h0, w_conv, b_conv, w_gate, b_a, b_i, lam)


def _mix_norm_kernel(*refs, aliased, n_tiles):
    ins = refs[:11]
    outs = refs[12:] if aliased else refs[11:]
    live = pl.program_id(0) < n_tiles

    @pl.when(live)
    def _():
        _mix_norm_body(*ins, *outs)

    @pl.when(jnp.logical_not(live))
    def _():
        outs[1][...] = jnp.zeros(outs[1].shape, outs[1].dtype)


def _mix_norm_body(fox_ref, lru_ref, x_ref, wt_ref, wb_ref, g_ref, b_ref, wrc_ref, wrh_ref, br_ref, cnt0_ref,
                   x1_ref, x1b_ref, idx_ref, gate_ref, rank_ref, cnt_ref):
    tm = x_ref.shape[0]

    mix = jnp.dot(fox_ref[...], wt_ref[...], preferred_element_type=F32)
    mix = mix + jnp.dot(lru_ref[...], wb_ref[...], preferred_element_type=F32)
    x1 = _layer_norm(DEEPNORM_ALPHA * x_ref[...] + mix, g_ref[...], b_ref[...])
    x1_ref[...] = x1
    x1_hi = x1.astype(BF16)
    x1b_ref[...] = x1_hi
    x1_lo = (x1 - x1_hi.astype(F32)).astype(BF16)
    lg2 = jnp.dot(x1_hi, wrc_ref[...], preferred_element_type=F32)
    lg = lg2[:, :N_EXPERTS] + lg2[:, N_EXPERTS:] + jnp.dot(x1_lo, wrh_ref[...], preferred_element_type=F32)
    lg = lg + br_ref[...]
    lanes = lax.broadcasted_iota(jnp.int32, lg.shape, 1)
    vals, picks = [], []
    for k in range(TOP_K):
        m = jnp.max(lg, axis=-1, keepdims=True)
        ix = jnp.min(jnp.where(lg == m, lanes, N_EXPERTS), axis=-1, keepdims=True)
        idx_ref[:, k:k + 1] = ix
        vals.append(m)
        picks.append(lanes == ix)
        lg = jnp.where(picks[-1], -jnp.inf, lg)
    es = [jnp.exp(v - vals[0]) for v in vals]
    denom = es[0] + es[1] + es[2] + es[3]
    for k in range(TOP_K):
        gate_ref[:, k:k + 1] = es[k] / denom

    @pl.when(pl.program_id(0) == 0)
    def _():
        cnt_ref[...] = cnt0_ref[...]

    sel = jnp.where(picks[0], 1.0, 0.0)
    for k in range(1, TOP_K):
        sel = sel + jnp.where(picks[k], 1.0, 0.0)
    earlier = lax.broadcasted_iota(jnp.int32, (tm, tm), 1) < lax.broadcasted_iota(jnp.int32, (tm, tm), 0)
    tri = jnp.where(earlier, 1.0, 0.0).astype(BF16)
    before = jnp.dot(tri, sel.astype(BF16), preferred_element_type=F32) + cnt_ref[...]
    for k in range(TOP_K):
        rank_ref[:, k:k + 1] = jnp.sum(jnp.where(picks[k], before, 0.0), axis=-1, keepdims=True).astype(jnp.int32)
    cnt_ref[...] += jnp.sum(sel, axis=0, keepdims=True)


def _mix_norm(fox_o, lru_o, x2d, w_top, w_bot, ln_g, ln_b, wr_cat, wr_hi, b_router, cnt0, tm, n_all, row_block0,
              x1b_buf=None):
    n, d = x2d.shape
    half = fox_o.shape[1]
    n_tiles = n // tm
    aliased = x1b_buf is not None
    steps = n_tiles if aliased else n_all // tm - row_block0
    row = lambda i: (jnp.minimum(i, n_tiles - 1), 0)
    const = lambda i: (0, 0)
    in_specs = [
        pl.BlockSpec((tm, half), row),
        pl.BlockSpec((tm, half), row),
        pl.BlockSpec((tm, d), row),
        pl.BlockSpec((half, d), const),
        pl.BlockSpec((half, d), const),
        pl.BlockSpec((1, d), const),
        pl.BlockSpec((1, d), const),
        pl.BlockSpec((d, 2 * N_EXPERTS), const),
        pl.BlockSpec((d, N_EXPERTS), const),
        pl.BlockSpec((1, N_EXPERTS), const),
        pl.BlockSpec((1, N_EXPERTS), const),
    ]
    args = [fox_o, lru_o, x2d, w_top, w_bot, ln_g, ln_b, wr_cat, wr_hi, b_router, cnt0]
    if aliased:
        in_specs.append(pl.BlockSpec(memory_space=pl.ANY))
        args.append(x1b_buf)
    return pl.pallas_call(
        functools.partial(_mix_norm_kernel, aliased=aliased, n_tiles=n_tiles),
        out_shape=[
            jax.ShapeDtypeStruct((n, d), F32),
            jax.ShapeDtypeStruct((n_all, d), BF16),
            jax.ShapeDtypeStruct((n, TOP_K), jnp.int32),
            jax.ShapeDtypeStruct((n, TOP_K), F32),
            jax.ShapeDtypeStruct((n, TOP_K), jnp.int32),
            jax.ShapeDtypeStruct((1, N_EXPERTS), F32),
        ],
        grid=(steps,),
        in_specs=in_specs,
        out_specs=[
            pl.BlockSpec((tm, d), row),
            pl.BlockSpec((tm, d), lambda i: (row_block0 + i, 0)),
            pl.BlockSpec((tm, TOP_K), row),
            pl.BlockSpec((tm, TOP_K), row),
            pl.BlockSpec((tm, TOP_K), row),
            pl.BlockSpec((1, N_EXPERTS), const),
        ],
        input_output_aliases={11: 1} if aliased else {},
        compiler_params=_cparams(("arbitrary",)),
        name="mix_norm",
    )(*args)


_RING = 6
_AHEAD = 4


def _expert_kernel(be_ref, bv_ref, nu_ref, x_ref, bgu_ref, bd_ref, wgu_hbm, wd_hbm, o_ref,
                   act_ref, ring_ref, sem_ref, *, sub, tc):
    rb = pl.program_id(0)
    valid = bv_ref[rb]
    tm, d = x_ref.shape
    d_ff = act_ref.shape[1]
    nfc = d_ff // tc
    nnc = d // tc
    per_block = 2 * nfc + nnc
    total = nu_ref[0] * per_block

    def chunk_copy(src, slot):
        return pltpu.make_async_copy(src, ring_ref.at[slot], sem_ref.at[slot])

    def start(cg):
        @pl.when(cg < total)
        def _():
            blk = cg // per_block
            j = cg - blk * per_block
            e = be_ref[blk]
            slot = lax.rem(cg, _RING)

            @pl.when(j < 2 * nfc)
            def _():
                col = pl.multiple_of((lax.rem(j, 2) * nfc + j // 2) * tc, tc)
                chunk_copy(wgu_hbm.at[e, :, pl.ds(col, tc)], slot).start()

            @pl.when(j >= 2 * nfc)
            def _():
                col = pl.multiple_of((j - 2 * nfc) * tc, tc)
                chunk_copy(wd_hbm.at[e, :, pl.ds(col, tc)], slot).start()

    def wait(cg):
        slot = lax.rem(cg, _RING)
        chunk_copy(wgu_hbm.at[0, :, pl.ds(0, tc)], slot).wait()
        return slot

    @pl.when(rb == 0)
    def _():
        for c in range(_AHEAD):
            start(c)

    base = rb * per_block

    def block(nrows):
        def up(j, carry):
            cg = base + 2 * j
            start(cg + _AHEAD)
            start(cg + 1 + _AHEAD)
            gate_slot = wait(cg)
            up_slot = wait(cg + 1)
            col = pl.ds(pl.multiple_of(j * tc, tc), tc)
            ucol = pl.ds(pl.multiple_of(d_ff + j * tc, tc), tc)
            x = x_ref[0:nrows, :]
            hg = jnp.dot(x, ring_ref[gate_slot].astype(BF16), preferred_element_type=F32) + bgu_ref[0, :, col]
            hu = jnp.dot(x, ring_ref[up_slot].astype(BF16), preferred_element_type=F32) + bgu_ref[0, :, ucol]
            gate = jnp.minimum(hg, SWIGLU_LIMIT)
            upv = jnp.clip(hu, -SWIGLU_LIMIT, SWIGLU_LIMIT)
            act = (upv + 1.0) * gate * jax.nn.sigmoid(SWIGLU_ALPHA * gate)
            act_ref[0:nrows, col] = act.astype(BF16)
            return carry

        lax.fori_loop(0, nfc, up, 0)

        def down(n, carry):
            cg = base + 2 * nfc + n
            start(cg + _AHEAD)
            slot = wait(cg)
            col = pl.ds(pl.multiple_of(n * tc, tc), tc)
            y = jnp.dot(act_ref[0:nrows, :], ring_ref[slot].astype(BF16), preferred_element_type=F32)
            o_ref[0:nrows, col] = (y + bd_ref[0, :, col]).astype(o_ref.dtype)
            return carry

        lax.fori_loop(0, nnc, down, 0)
        if nrows < tm:
            o_ref[nrows:tm, :] = jnp.zeros((tm - nrows, d), o_ref.dtype)

    for nrows in range(sub, tm + 1, sub):
        @pl.when(jnp.logical_and(valid > nrows - sub, valid <= nrows))
        def _(nrows=nrows):
            block(nrows)

    @pl.when(valid == 0)
    def _():
        o_ref[...] = jnp.zeros(o_ref.shape, o_ref.dtype)


def _experts(x_pad, block_e, block_valid, n_used, w_gu, b_gu, w_down, b_down, tm, tc, sub):
    p, d = x_pad.shape
    d_ff = w_down.shape[1]
    assert d_ff == d and _AHEAD <= _RING - 2
    nb = p // tm

    def rbc(rb, nu):
        return jnp.minimum(rb, nu[0] - 1)

    grid_spec = pltpu.PrefetchScalarGridSpec(
        num_scalar_prefetch=3,
        grid=(nb,),
        in_specs=[
            pl.BlockSpec((tm, d), lambda rb, be, bv, nu: (rbc(rb, nu), 0)),
            pl.BlockSpec((1, 1, 2 * d_ff), lambda rb, be, bv, nu: (be[rbc(rb, nu)], 0, 0)),
            pl.BlockSpec((1, 1, d), lambda rb, be, bv, nu: (be[rbc(rb, nu)], 0, 0)),
            pl.BlockSpec(memory_space=pl.ANY),
            pl.BlockSpec(memory_space=pl.ANY),
        ],
        out_specs=pl.BlockSpec((tm, d), lambda rb, be, bv, nu: (rb, 0)),
        scratch_shapes=[
            pltpu.VMEM((tm, d_ff), BF16),
            pltpu.VMEM((_RING, d, tc), F32),
            pltpu.SemaphoreType.DMA((_RING,)),
        ],
    )
    return pl.pallas_call(
        functools.partial(_expert_kernel, sub=sub, tc=tc),
        out_shape=jax.ShapeDtypeStruct((p, d), BF16),
        grid_spec=grid_spec,
        compiler_params=_cparams(("arbitrary",)),
        name="experts",
    )(block_e, block_valid, n_used, x_pad, b_gu, b_down, w_gu, w_down)


def _combine_kernel(y_ref, gate_ref, x1_ref, g_ref, b_ref, o_ref):
    y = y_ref[0].astype(F32) * gate_ref[:, 0:1]
    for k in range(1, TOP_K):
        y = y + y_ref[k].astype(F32) * gate_ref[:, k:k + 1]
    o_ref[...] = _layer_norm(DEEPNORM_ALPHA * x1_ref[...] + y, g_ref[...], b_ref[...])


def _combine(y_rows, gates, x1, ln_g, ln_b, row_block0, tn):
    n, d = x1.shape
    const = lambda i: (0, 0)
    return pl.pallas_call(
        _combine_kernel,
        out_shape=jax.ShapeDtypeStruct((n, d), F32),
        grid=(n // tn,),
        in_specs=[
            pl.BlockSpec((TOP_K, tn, d), lambda i: (0, row_block0 + i, 0)),
            pl.BlockSpec((tn, TOP_K), lambda i: (i, 0)),
            pl.BlockSpec((tn, d), lambda i: (i, 0)),
            pl.BlockSpec((1, d), const),
            pl.BlockSpec((1, d), const),
        ],
        out_specs=pl.BlockSpec((tn, d), lambda i: (i, 0)),
        compiler_params=_cparams(("parallel",)),
        name="combine",
    )(y_rows, gates, x1, ln_g, ln_b)


def _route(idx, rank, sizes, tm):
    n = idx.shape[0]
    nk = n * TOP_K
    nblk = (sizes + tm - 1) // tm
    bends = jnp.cumsum(nblk)
    bstart = bends - nblk
    dest = bstart[idx] * tm + rank
    nb = -(-nk // tm) + N_EXPERTS
    blk = jnp.arange(nb, dtype=jnp.int32)
    block_e = jnp.minimum(jnp.sum(bends[None, :] <= blk[:, None], axis=1), N_EXPERTS - 1).astype(jnp.int32)
    n_used = bends[-1].astype(jnp.int32)
    valid = jnp.clip(sizes[block_e] - (blk - bstart[block_e]) * tm, 0, tm)
    valid = jnp.where(blk < n_used, valid, 0).astype(jnp.int32)
    tok = jnp.broadcast_to(jnp.arange(n, dtype=jnp.int32)[:, None], (n, TOP_K))
    keys = jnp.sort((idx * n + tok).reshape(-1))
    tok_sorted = keys - (keys // n) * n
    row = jnp.arange(nb * tm, dtype=jnp.int32)
    row_e = jnp.repeat(block_e, tm)
    off = row - bstart[row_e] * tm
    first = jnp.cumsum(sizes) - sizes
    src = jnp.clip(first[row_e] + off, 0, nk - 1)
    row_tok = jnp.where(off < sizes[row_e], tok_sorted[src], row % n)
    return dest, row_tok, block_e, valid, n_used.reshape(1)


def _pick(n, pref):
    t = min(n, pref)
    while n % t:
        t //= 2
    return t


def kernel(x_prompt, x_sample, cache_k, cache_v, cache_logf, state_conv, state_lru, w_in, b_f, w_conv, b_conv, w_a, b_a, w_i, b_i, lam, w_out, ln1_g, ln1_b, w_router, b_router, w_gu, b_gu, w_down, b_down, ln2_g, ln2_b):
    assert w_in.shape[0] == DEPTH
    bp, tp, d = x_prompt.shape
    bs, ts, _ = x_sample.shape
    past = cache_k.shape[2]
    lru_w = w_conv.shape[-1]
    np_, ns = bp * tp, bs * ts

    win = w_in[0]
    f0 = 3 * FOX_WIDTH
    w_main = jnp.concatenate([win[:, :f0], win[:, f0 + FOX_HEADS:]], axis=1).astype(BF16)
    w_f = jnp.pad(win[:, f0:f0 + FOX_HEADS], ((0, 0), (0, 128 - FOX_HEADS))).astype(BF16)
    bf2 = b_f[0].reshape(1, FOX_HEADS)
    gpb = 4
    bd = w_a.shape[-1]
    eye = jnp.eye(gpb, dtype=F32)

    def blockdiag(w):
        wg = w.reshape(LRU_BLOCKS // gpb, gpb, bd, bd)
        return jnp.einsum("gacd,ab->gacbd", wg, eye).reshape(LRU_BLOCKS // gpb, gpb * bd, gpb * bd)

    w_gate = jnp.concatenate([blockdiag(w_a[0]), blockdiag(w_i[0])], axis=-1).astype(BF16)
    w_top = w_out[0, :FOX_WIDTH].astype(BF16)
    w_bot = w_out[0, FOX_WIDTH:].astype(BF16)
    wr = w_router[0]
    wr_hi = wr.astype(BF16)
    wr_cat = jnp.concatenate([wr_hi, (wr - wr_hi.astype(F32)).astype(BF16)], axis=1)
    row = lambda a: a.reshape(1, -1)

    n_all = np_ + ns
    tm_mix = _pick(ns, 512)

    def mixers(x, batch, seq, fox_fn, conv0, h0, cnt0, x1b_buf, row0):
        n = batch * seq
        q, k, v, kb, vb, xr, yg, logf = _in_proj(x.reshape(n, d), w_main, w_f, bf2, _pick(n, 512))
        fox_o = fox_fn(q, kb, vb, logf.reshape(batch, seq, FOX_HEADS))
        lru_o, conv_new, h_last = _lru(xr, yg, conv0, h0, w_conv[0], row(b_conv[0]), w_gate,
                                       row(b_a[0]), row(b_i[0]), row(lam[0]), batch, seq, _pick(seq, 256))
        routed = _mix_norm(fox_o, lru_o, x.reshape(n, d), w_top, w_bot, row(ln1_g[0]), row(ln1_b[0]),
                           wr_cat, wr_hi, row(b_router[0]), cnt0, tm_mix, n_all, row0 // tm_mix, x1b_buf)
        state = (k.reshape(1, batch, seq, FOX_HEADS, FOX_HEAD_DIM), v.reshape(1, batch, seq, FOX_HEADS, FOX_HEAD_DIM),
                 logf.reshape(1, batch, seq, FOX_HEADS), conv_new[None], h_last.reshape(1, batch, lru_w))
        return routed, state

    def fox_p(q, kb, vb, logf):
        c = _cumsum_time(logf, jnp.zeros((bp, 1, FOX_HEADS), F32))
        return _fox_prompt(q, kb, vb, c * LOG2E, bp, tp, _pick(tp, 512))

    def fox_s(q, kb, vb, logf):
        c_past = _cumsum_time(cache_logf[0], jnp.zeros((bs, 1, FOX_HEADS), F32))
        c_new = _cumsum_time(logf, c_past[:, -1:, :])
        c_all = jnp.concatenate([c_past, c_new], axis=1) * LOG2E
        cache_rows = lambda c: c[0].reshape(bs, past * FOX_HEADS, FOX_HEAD_DIM)
        return _fox_sample(q, kb, vb, cache_rows(cache_k), cache_rows(cache_v), c_all, bs, ts, _pick(past, 512))

    (x1p, x1b, idxp, gp, rankp, cnt_p), state_p = mixers(
        x_prompt, bp, tp, fox_p, jnp.zeros((bp, CONV_WIDTH - 1, lru_w), F32), jnp.zeros((bp, 1, lru_w), F32),
        jnp.zeros((1, N_EXPERTS), F32), None, 0)
    (x1s, x1b, idxs, gs, ranks, cnt_all), state_s = mixers(
        x_sample, bs, ts, fox_s, state_conv[0], state_lru[0].reshape(bs, 1, lru_w), cnt_p, x1b, np_)

    tm_e, tc_e, sub_e = 1024, 512, 256
    idx = jnp.concatenate([idxp, idxs], axis=0)
    rank = jnp.concatenate([rankp, ranks], axis=0)
    dest, row_tok, block_e, block_valid, n_used = _route(idx, rank, cnt_all[0].astype(jnp.int32), tm_e)
    x_pad = x1b[row_tok]
    y_pad = _experts(x_pad, block_e, block_valid, n_used, w_gu[0], b_gu[0].reshape(N_EXPERTS, 1, -1),
                     w_down[0], b_down[0].reshape(N_EXPERTS, 1, -1), tm_e, tc_e, sub_e)
    y_rows = y_pad[dest.T]

    tn = _pick(ns, 512)
    yp = _combine(y_rows, gp, x1p, row(ln2_g[0]), row(ln2_b[0]), 0, tn)
    ys = _combine(y_rows, gs, x1s, row(ln2_g[0]), row(ln2_b[0]), np_ // tn, tn)
    return (yp.reshape(bp, tp, d), ys.reshape(bs, ts, d)) + state_p + state_s
```

```python
import functools

import jax
import jax.numpy as jnp
from jax import lax
from jax.experimental import pallas as pl
from jax.experimental.pallas import tpu as pltpu

F32 = jnp.float32
BF16 = jnp.bfloat16

FOX_HEADS = 8
FOX_HEAD_DIM = 128
FOX_WIDTH = FOX_HEADS * FOX_HEAD_DIM
LRU_BLOCKS = 16
CONV_WIDTH = 4
LRU_C = 8.0
N_EXPERTS = 32
TOP_K = 4
SWIGLU_LIMIT = 7.0
SWIGLU_ALPHA = 1.702
LN_EPS = 1e-5
DEPTH = 1
DEEPNORM_ALPHA = (2.0 * DEPTH) ** 0.25
LOG2E = 1.4426950408889634
Q_SCALE = FOX_HEAD_DIM ** -0.5 * LOG2E

LANES = 128
V7X_VMEM_BYTES = 64 * 1024 * 1024
VMEM_LIMIT = V7X_VMEM_BYTES * 7 // 8

ROWS_IN_PROJ = 512
ROWS_ATTN = 512
ROWS_LRU = 512
ROWS_CUMSUM = 256
ROWS_EXPERT_BLOCK = 1024
ROWS_EXPERT_SUB = 256
COLS_EXPERT_CHUNK = 512


def _cparams(sem):
    return pltpu.CompilerParams(dimension_semantics=sem, vmem_limit_bytes=VMEM_LIMIT)


def _log_sigmoid(x):
    return jnp.minimum(x, 0.0) - jnp.log1p(jnp.exp(-jnp.abs(x)))


def _layer_norm(z, g, b):
    mu = jnp.mean(z, axis=-1, keepdims=True)
    zc = z - mu
    var = jnp.mean(zc * zc, axis=-1, keepdims=True)
    return zc * lax.rsqrt(var + LN_EPS) * g + b


def _in_proj_kernel(x_ref, w_ref, wf_ref, bf_ref, q_ref, k_ref, v_ref, kb_ref, vb_ref, xr_ref, yg_ref, lf_ref):
    wcol = q_ref.shape[1]
    xb = x_ref[...].astype(BF16)
    zf = jnp.dot(xb, wf_ref[...], preferred_element_type=F32)
    lf_ref[...] = _log_sigmoid(zf[:, :FOX_HEADS] + bf_ref[...])

    def proj(j):
        return jnp.dot(xb, w_ref[:, j * wcol:(j + 1) * wcol], preferred_element_type=F32)

    q_ref[...] = (proj(0) * Q_SCALE).astype(BF16)
    z = proj(1)
    k_ref[...] = z
    kb_ref[...] = z.astype(BF16)
    z = proj(2)
    v_ref[...] = z
    vb_ref[...] = z.astype(BF16)
    xr_ref[...] = proj(3)
    yg_ref[...] = proj(4)


def _in_proj(x2d, w_main, w_f, b_f, tm):
    n, d = x2d.shape
    wcol = FOX_WIDTH
    row = lambda i: (i, 0)
    const = lambda i: (0, 0)
    dts = (BF16, F32, F32, BF16, BF16, F32, F32)
    outs = [jax.ShapeDtypeStruct((n, wcol), dt) for dt in dts]
    outs.append(jax.ShapeDtypeStruct((n, FOX_HEADS), F32))
    return pl.pallas_call(
        _in_proj_kernel,
        out_shape=outs,
        grid=(n // tm,),
        in_specs=[
            pl.BlockSpec((tm, d), row),
            pl.BlockSpec(w_main.shape, const, pipeline_mode=pl.Buffered(1)),
            pl.BlockSpec((d, LANES), const, pipeline_mode=pl.Buffered(1)),
            pl.BlockSpec((1, FOX_HEADS), const),
        ],
        out_specs=[pl.BlockSpec((tm, wcol), row)] * len(dts) + [pl.BlockSpec((tm, FOX_HEADS), row)],
        compiler_params=_cparams(("parallel",)),
        name="in_proj",
    )(x2d, w_main, w_f, b_f)


def _cumsum_kernel(s_ref, c0_ref, o_ref, tri_ref):
    t = s_ref.shape[1]
    tc = tri_ref.shape[0]
    h = FOX_HEADS

    @pl.when(pl.program_id(0) == 0)
    def _():
        row = lax.broadcasted_iota(jnp.int32, (tc, tc), 0)
        col = lax.broadcasted_iota(jnp.int32, (tc, tc), 1)
        tri_ref[...] = jnp.where(col <= row, 1.0, 0.0).astype(BF16)

    carry = c0_ref[0]
    for c in range(t // tc):
        rows = slice(c * tc, (c + 1) * tc)
        c3 = jnp.dot(tri_ref[...], s_ref[0, rows, :], preferred_element_type=F32)
        out = c3[:, 0:h] + c3[:, h:2 * h] + c3[:, 2 * h:3 * h] + carry
        o_ref[0, rows, :] = out
        carry = out[tc - 1:tc, :]


def _cumsum_time(x, c0):
    b, t, h = x.shape
    hi = x.astype(BF16)
    r1 = x - hi.astype(F32)
    mid = r1.astype(BF16)
    lo = (r1 - mid.astype(F32)).astype(BF16)
    pieces = jnp.concatenate([hi, mid, lo, jnp.zeros((b, t, LANES - 3 * h), BF16)], axis=-1)
    return pl.pallas_call(
        _cumsum_kernel,
        out_shape=jax.ShapeDtypeStruct((b, t, h), F32),
        grid=(b,),
        in_specs=[
            pl.BlockSpec((1, t, LANES), lambda i: (i, 0, 0)),
            pl.BlockSpec((1, 1, h), lambda i: (i, 0, 0)),
        ],
        out_specs=pl.BlockSpec((1, t, h), lambda i: (i, 0, 0)),
        scratch_shapes=[pltpu.VMEM((_pick(t, ROWS_CUMSUM), _pick(t, ROWS_CUMSUM)), BF16)],
        compiler_params=_cparams(("arbitrary",)),
        name="cumsum_time",
    )(pieces, c0)


_NT = (((1,), (1,)), ((), ()))


def _attn_init(m_ref, l_ref, acc_ref):
    m_ref[...] = jnp.full(m_ref.shape, -jnp.inf, F32)
    l_ref[...] = jnp.zeros(l_ref.shape, F32)
    acc_ref[...] = jnp.zeros(acc_ref.shape, F32)


def _fox_prompt_kernel(qi_ref, ki_ref, q_ref, k_ref, v_ref, ck_ref, o_ref,
                       m_ref, l_ref, acc_ref, s0_ref, s1_ref, p0_ref, p1_ref, a0_ref, a1_ref, *, rs):
    pair = pl.program_id(1)
    qi = qi_ref[pair]
    ki = ki_ref[pair]
    tq = q_ref.shape[0]
    tk = k_ref.shape[0]
    lanes = FOX_HEAD_DIM

    @pl.when(ki == 0)
    def _():
        _attn_init(m_ref, l_ref, acc_ref)

    def head_lanes(h):
        return pl.ds(pl.multiple_of(h * lanes, lanes), lanes)

    def scores(h, s_ref):
        sl = head_lanes(h)
        s_ref[...] = lax.dot_general(q_ref[:, sl], k_ref[:, sl], _NT, preferred_element_type=F32)

    def weighted_values(h, p_ref, a_ref):
        sl = head_lanes(h)
        pv = jnp.dot(p_ref[...], v_ref[:, sl], preferred_element_type=F32)
        acc_ref[:, sl] = a_ref[...] * acc_ref[:, sl] + pv

    def softmax(h, s_ref, p_ref, a_ref, masked):
        ck = ck_ref[0, h]
        for r in range(tq // rs):
            r0 = r * rs
            rsl = slice(r0, r0 + rs)
            chunks = []
            for c in range(tk // lanes):
                if masked and c * lanes > r0 + rs - 1:
                    continue
                s = s_ref[rsl, c * lanes:(c + 1) * lanes] - ck[:, c * lanes:(c + 1) * lanes]
                if masked and (c + 1) * lanes - 1 > r0:
                    row = r0 + lax.broadcasted_iota(jnp.int32, (rs, lanes), 0)
                    col = c * lanes + lax.broadcasted_iota(jnp.int32, (rs, lanes), 1)
                    s = jnp.where(col <= row, s, -jnp.inf)
                chunks.append((c, s))
            mc = chunks[0][1]
            for _, s in chunks[1:]:
                mc = jnp.maximum(mc, s)
            m_prev = m_ref[h, rsl, :]
            m_new = jnp.maximum(m_prev, jnp.max(mc, axis=-1, keepdims=True))
            alpha = jnp.exp2(m_prev - m_new)
            psum = None
            for c, s in chunks:
                p = jnp.exp2(s - m_new)
                p_ref[rsl, c * lanes:(c + 1) * lanes] = p.astype(BF16)
                psum = p if psum is None else psum + p
            for c in range(chunks[-1][0] + 1, tk // lanes):
                p_ref[rsl, c * lanes:(c + 1) * lanes] = jnp.zeros((rs, lanes), BF16)
            l_ref[h, rsl, :] = alpha * l_ref[h, rsl, :] + psum
            m_ref[h, rsl, :] = m_new
            a_ref[rsl, :] = alpha

    def all_heads(masked):
        scores(0, s0_ref)
        p1_ref[...] = jnp.zeros(p1_ref.shape, BF16)
        a1_ref[...] = jnp.ones(a1_ref.shape, F32)

        def two_heads(i, carry):
            h0 = 2 * i
            h1 = h0 + 1
            scores(h1, s1_ref)
            softmax(h0, s0_ref, p0_ref, a0_ref, masked)
            weighted_values(lax.rem(h0 + FOX_HEADS - 1, FOX_HEADS), p1_ref, a1_ref)
            scores(lax.rem(h0 + 2, FOX_HEADS), s0_ref)
            softmax(h1, s1_ref, p1_ref, a1_ref, masked)
            weighted_values(h0, p0_ref, a0_ref)
            return carry

        lax.fori_loop(0, FOX_HEADS // 2, two_heads, 0)
        weighted_values(FOX_HEADS - 1, p1_ref, a1_ref)

    @pl.when(ki < qi)
    def _():
        all_heads(False)

    @pl.when(ki == qi)
    def _():
        all_heads(True)
        for h in range(FOX_HEADS):
            sl = slice(h * lanes, (h + 1) * lanes)
            l_tot = jnp.sum(l_ref[h], axis=-1, keepdims=True)
            o_ref[:, sl] = (acc_ref[:, sl] / l_tot).astype(o_ref.dtype)


def _fox_prompt(q, kb, vb, c2, batch, seq, tq):
    nq = seq // tq
    width = q.shape[1]
    ck = jnp.transpose(c2, (0, 2, 1))[:, :, None, :]
    pairs = [(i, j) for i in range(nq) for j in range(i + 1)]
    qi_tab = jnp.asarray([p[0] for p in pairs], jnp.int32)
    ki_tab = jnp.asarray([p[1] for p in pairs], jnp.int32)
    q_map = lambda b, p, qt, kt: (b * nq + qt[p], 0)
    k_map = lambda b, p, qt, kt: (b * nq + kt[p], 0)
    grid_spec = pltpu.PrefetchScalarGridSpec(
        num_scalar_prefetch=2,
        grid=(batch, len(pairs)),
        in_specs=[
            pl.BlockSpec((tq, width), q_map),
            pl.BlockSpec((tq, width), k_map),
            pl.BlockSpec((tq, width), k_map),
            pl.BlockSpec((1, FOX_HEADS, 1, tq), lambda b, p, qt, kt: (b, 0, 0, kt[p])),
        ],
        out_specs=pl.BlockSpec((tq, width), q_map),
        scratch_shapes=[
            pltpu.VMEM((FOX_HEADS, tq, FOX_HEAD_DIM), F32),
            pltpu.VMEM((FOX_HEADS, tq, FOX_HEAD_DIM), F32),
            pltpu.VMEM((tq, width), F32),
            pltpu.VMEM((tq, tq), F32),
            pltpu.VMEM((tq, tq), F32),
            pltpu.VMEM((tq, tq), BF16),
            pltpu.VMEM((tq, tq), BF16),
            pltpu.VMEM((tq, FOX_HEAD_DIM), F32),
            pltpu.VMEM((tq, FOX_HEAD_DIM), F32),
        ],
    )
    return pl.pallas_call(
        functools.partial(_fox_prompt_kernel, rs=min(64, tq)),
        out_shape=jax.ShapeDtypeStruct(q.shape, BF16),
        grid_spec=grid_spec,
        compiler_params=_cparams(("parallel", "arbitrary")),
        name="fox_prompt",
    )(qi_tab, ki_tab, q, kb, vb, ck)


def _fox_sample_kernel(q_ref, kc_ref, vc_ref, kn_ref, vn_ref, ckc_ref, ckn_ref, o_ref, m_ref, l_ref, acc_ref):
    j = pl.program_id(1)
    tq = q_ref.shape[0]
    tk = kc_ref.shape[1] // FOX_HEADS
    lanes = FOX_HEAD_DIM

    @pl.when(j == 0)
    def _():
        _attn_init(m_ref, l_ref, acc_ref)

    heads = range(FOX_HEADS)
    head_lanes = [slice(h * lanes, (h + 1) * lanes) for h in heads]

    def update(ks, vs, cks, masked):
        ss = [lax.dot_general(q_ref[:, head_lanes[h]], ks[h], _NT, preferred_element_type=F32) - cks[h]
              for h in heads]
        if masked:
            row = lax.broadcasted_iota(jnp.int32, ss[0].shape, 0)
            col = lax.broadcasted_iota(jnp.int32, ss[0].shape, 1)
            ss = [jnp.where(col <= row, s, -jnp.inf) for s in ss]
        m_prev = [m_ref[h] for h in heads]
        m_new = [jnp.maximum(m_prev[h], jnp.max(ss[h], axis=-1, keepdims=True)) for h in heads]
        alpha = [jnp.exp2(m_prev[h] - m_new[h]) for h in heads]
        ps = [jnp.exp2(ss[h] - m_new[h][:, 0:1]) for h in heads]
        pvs = [jnp.dot(ps[h].astype(BF16), vs[h], preferred_element_type=F32) for h in heads]
        for h in heads:
            l_ref[h] = alpha[h] * l_ref[h] + jnp.sum(ps[h], axis=-1, keepdims=True)
            acc_ref[:, head_lanes[h]] = alpha[h] * acc_ref[:, head_lanes[h]] + pvs[h]
            m_ref[h] = m_new[h]

    update([kc_ref[0, pl.ds(h, tk, stride=FOX_HEADS), :].astype(BF16) for h in heads],
           [vc_ref[0, pl.ds(h, tk, stride=FOX_HEADS), :].astype(BF16) for h in heads],
           [ckc_ref[0, h] for h in heads], False)

    @pl.when(j == pl.num_programs(1) - 1)
    def _():
        update([kn_ref[:, sl] for sl in head_lanes], [vn_ref[:, sl] for sl in head_lanes],
               [ckn_ref[0, h] for h in heads], True)
        for h in heads:
            sl = head_lanes[h]
            o_ref[:, sl] = (acc_ref[:, sl] / l_ref[h]).astype(o_ref.dtype)


def _fox_sample(q, kb, vb, cache_k, cache_v, c2_all, batch, seq, tk):
    past = cache_k.shape[1] // FOX_HEADS
    width = q.shape[1]
    ct = jnp.transpose(c2_all, (0, 2, 1))
    ckc = ct[:, :, None, :past]
    ckn = ct[:, :, None, past:]
    new_map = lambda b, j: (b, 0)
    cache_spec = pl.BlockSpec((1, tk * FOX_HEADS, FOX_HEAD_DIM), lambda b, j: (b, j, 0))
    return pl.pallas_call(
        _fox_sample_kernel,
        out_shape=jax.ShapeDtypeStruct(q.shape, BF16),
        grid=(batch, past // tk),
        in_specs=[
            pl.BlockSpec((seq, width), new_map),
            cache_spec,
            cache_spec,
            pl.BlockSpec((seq, width), new_map),
            pl.BlockSpec((seq, width), new_map),
            pl.BlockSpec((1, FOX_HEADS, 1, tk), lambda b, j: (b, 0, 0, j)),
            pl.BlockSpec((1, FOX_HEADS, 1, seq), lambda b, j: (b, 0, 0, 0)),
        ],
        out_specs=pl.BlockSpec((seq, width), new_map),
        scratch_shapes=[
            pltpu.VMEM((FOX_HEADS, seq, FOX_HEAD_DIM), F32),
            pltpu.VMEM((FOX_HEADS, seq, FOX_HEAD_DIM), F32),
            pltpu.VMEM((seq, width), F32),
        ],
        compiler_params=_cparams(("parallel", "arbitrary")),
        name="fox_sample",
    )(q, cache_k, cache_v, kb, vb, ckc, ckn)


_HALO = 8


def _lru_kernel(xr_ref, yg_ref, conv0_ref, h0_ref, wc_ref, bc_ref, wg_ref, ba_ref, bi_ref, lam_ref,
                o_ref, convo_ref, hlast_ref, xp_ref, a_ref, b_ref, h_ref):
    t = pl.program_id(1)
    tt, width = xr_ref.shape
    tail = CONV_WIDTH - 1
    lo = _HALO - tail

    @pl.when(t == 0)
    def _():
        xp_ref[lo:_HALO, :] = conv0_ref[0]
        h_ref[...] = h0_ref[0]

    xp_ref[_HALO:_HALO + tt, :] = xr_ref[...]
    xc = bc_ref[...] + xp_ref[lo:lo + tt, :] * wc_ref[0:1, :]
    for j in range(1, CONV_WIDTH):
        xc = xc + xp_ref[lo + j:lo + j + tt, :] * wc_ref[j:j + 1, :]
    new_tail = xp_ref[lo + tt:_HALO + tt, :]
    xp_ref[lo:_HALO, :] = new_tail

    xcb = xc.astype(BF16)
    gw = wg_ref.shape[1]
    lam = lam_ref[...]
    neg_sp = -(jnp.maximum(-lam, 0.0) + jnp.log1p(jnp.exp(-jnp.abs(lam))))
    for g in range(width // gw):
        sl = slice(g * gw, (g + 1) * gw)
        z = jnp.dot(xcb[:, sl], wg_ref[g], preferred_element_type=F32)
        r = jax.nn.sigmoid(z[:, :gw] + ba_ref[:, sl])
        i = jax.nn.sigmoid(z[:, gw:] + bi_ref[:, sl])
        log_a = LRU_C * r * neg_sp[:, sl]
        a = jnp.exp(log_a)
        a_ref[:, sl] = a
        b_ref[:, sl] = jnp.sqrt(-jnp.tanh(log_a) * (a * a + 1.0)) * i * xc[:, sl]

    def step(s, h):
        row = pl.ds(s, 1)
        h = a_ref[row, :] * h + b_ref[row, :]
        b_ref[row, :] = h
        return h

    h = lax.fori_loop(0, tt, step, h_ref[...], unroll=8)
    h_ref[...] = h
    o_ref[...] = (b_ref[...] * jax.nn.gelu(yg_ref[...])).astype(o_ref.dtype)

    @pl.when(t == pl.num_programs(1) - 1)
    def _():
        convo_ref[0] = new_tail
        hlast_ref[0] = h


def _lru(xr, yg, conv0, h0, w_conv, b_conv, w_gate, b_a, b_i, lam, batch, seq, tt):
    width = xr.shape[1]
    nt = seq // tt
    gw = w_gate.shape[1]
    rows = lambda b, t: (b * nt + t, 0)
    const2 = lambda b, t: (0, 0)
    per_b = lambda b, t: (b, 0, 0)
    tail = CONV_WIDTH - 1
    return pl.pallas_call(
        _lru_kernel,
        out_shape=[
            jax.ShapeDtypeStruct((batch * seq, width), BF16),
            jax.ShapeDtypeStruct((batch, tail, width), F32),
            jax.ShapeDtypeStruct((batch, 1, width), F32),
        ],
        grid=(batch, nt),
        in_specs=[
            pl.BlockSpec((tt, width), rows),
            pl.BlockSpec((tt, width), rows),
            pl.BlockSpec((1, tail, width), per_b),
            pl.BlockSpec((1, 1, width), per_b),
            pl.BlockSpec((CONV_WIDTH, width), const2),
            pl.BlockSpec((1, width), const2),
            pl.BlockSpec((width // gw, gw, 2 * gw), lambda b, t: (0, 0, 0)),
            pl.BlockSpec((1, width), const2),
            pl.BlockSpec((1, width), const2),
            pl.BlockSpec((1, width), const2),
        ],
        out_specs=[
            pl.BlockSpec((tt, width), rows),
            pl.BlockSpec((1, tail, width), per_b),
            pl.BlockSpec((1, 1, width), per_b),
        ],
        scratch_shapes=[
            pltpu.VMEM((_HALO + tt, width), F32),
            pltpu.VMEM((tt, width), F32),
            pltpu.VMEM((tt, width), F32),
            pltpu.VMEM((1, width), F32),
        ],
        compiler_params=_cparams(("parallel", "arbitrary")),
        name="lru",
    )(xr, yg, conv0, h0, w_conv, b_conv, w_gate, b_a, b_i, lam)


def _mix_norm_kernel(*refs, aliased, n_tiles):
    ins = refs[:11]
    outs = refs[12:] if aliased else refs[11:]
    live = pl.program_id(0) < n_tiles

    @pl.when(live)
    def _():
        _mix_norm_body(*ins, *outs)

    @pl.when(jnp.logical_not(live))
    def _():
        outs[1][...] = jnp.zeros(outs[1].shape, outs[1].dtype)


def _mix_norm_body(fox_ref, lru_ref, x_ref, wt_ref, wb_ref, g_ref, b_ref, wrc_ref, wrh_ref, br_ref, cnt0_ref,
                   x1_ref, x1b_ref, idx_ref, gate_ref, rank_ref, cnt_ref):
    tm = x_ref.shape[0]

    mix = jnp.dot(fox_ref[...], wt_ref[...], preferred_element_type=F32)
    mix = mix + jnp.dot(lru_ref[...], wb_ref[...], preferred_element_type=F32)
    x1 = _layer_norm(DEEPNORM_ALPHA * x_ref[...] + mix, g_ref[...], b_ref[...])
    x1_ref[...] = x1
    x1_hi = x1.astype(BF16)
    x1b_ref[...] = x1_hi
    x1_lo = (x1 - x1_hi.astype(F32)).astype(BF16)
    lg2 = jnp.dot(x1_hi, wrc_ref[...], preferred_element_type=F32)
    lg = lg2[:, :N_EXPERTS] + lg2[:, N_EXPERTS:] + jnp.dot(x1_lo, wrh_ref[...], preferred_element_type=F32)
    lg = lg + br_ref[...]
    lanes = lax.broadcasted_iota(jnp.int32, lg.shape, 1)
    vals, picks = [], []
    for k in range(TOP_K):
        m = jnp.max(lg, axis=-1, keepdims=True)
        ix = jnp.min(jnp.where(lg == m, lanes, N_EXPERTS), axis=-1, keepdims=True)
        idx_ref[:, k:k + 1] = ix
        vals.append(m)
        picks.append(lanes == ix)
        lg = jnp.where(picks[-1], -jnp.inf, lg)
    es = [jnp.exp(v - vals[0]) for v in vals]
    denom = es[0] + es[1] + es[2] + es[3]
    for k in range(TOP_K):
        gate_ref[:, k:k + 1] = es[k] / denom

    @pl.when(pl.program_id(0) == 0)
    def _():
        cnt_ref[...] = cnt0_ref[...]

    sel = jnp.where(picks[0], 1.0, 0.0)
    for k in range(1, TOP_K):
        sel = sel + jnp.where(picks[k], 1.0, 0.0)
    earlier = lax.broadcasted_iota(jnp.int32, (tm, tm), 1) < lax.broadcasted_iota(jnp.int32, (tm, tm), 0)
    tri = jnp.where(earlier, 1.0, 0.0).astype(BF16)
    before = jnp.dot(tri, sel.astype(BF16), preferred_element_type=F32) + cnt_ref[...]
    for k in range(TOP_K):
        rank_ref[:, k:k + 1] = jnp.sum(jnp.where(picks[k], before, 0.0), axis=-1, keepdims=True).astype(jnp.int32)
    cnt_ref[...] += jnp.sum(sel, axis=0, keepdims=True)


def _mix_norm(fox_o, lru_o, x2d, w_top, w_bot, ln_g, ln_b, wr_cat, wr_hi, b_router, cnt0, tm, n_all, row_block0,
              x1b_buf=None):
    n, d = x2d.shape
    half = fox_o.shape[1]
    n_tiles = n // tm
    aliased = x1b_buf is not None
    steps = n_tiles if aliased else n_all // tm - row_block0
    row = lambda i: (jnp.minimum(i, n_tiles - 1), 0)
    const = lambda i: (0, 0)
    in_specs = [
        pl.BlockSpec((tm, half), row),
        pl.BlockSpec((tm, half), row),
        pl.BlockSpec((tm, d), row),
        pl.BlockSpec((half, d), const),
        pl.BlockSpec((half, d), const),
        pl.BlockSpec((1, d), const),
        pl.BlockSpec((1, d), const),
        pl.BlockSpec((d, 2 * N_EXPERTS), const),
        pl.BlockSpec((d, N_EXPERTS), const),
        pl.BlockSpec((1, N_EXPERTS), const),
        pl.BlockSpec((1, N_EXPERTS), const),
    ]
    args = [fox_o, lru_o, x2d, w_top, w_bot, ln_g, ln_b, wr_cat, wr_hi, b_router, cnt0]
    if aliased:
        in_specs.append(pl.BlockSpec(memory_space=pl.ANY))
        args.append(x1b_buf)
    return pl.pallas_call(
        functools.partial(_mix_norm_kernel, aliased=aliased, n_tiles=n_tiles),
        out_shape=[
            jax.ShapeDtypeStruct((n, d), F32),
            jax.ShapeDtypeStruct((n_all, d), BF16),
            jax.ShapeDtypeStruct((n, TOP_K), jnp.int32),
            jax.ShapeDtypeStruct((n, TOP_K), F32),
            jax.ShapeDtypeStruct((n, TOP_K), jnp.int32),
            jax.ShapeDtypeStruct((1, N_EXPERTS), F32),
        ],
        grid=(steps,),
        in_specs=in_specs,
        out_specs=[
            pl.BlockSpec((tm, d), row),
            pl.BlockSpec((tm, d), lambda i: (row_block0 + i, 0)),
            pl.BlockSpec((tm, TOP_K), row),
            pl.BlockSpec((tm, TOP_K), row),
            pl.BlockSpec((tm, TOP_K), row),
            pl.BlockSpec((1, N_EXPERTS), const),
        ],
        input_output_aliases={11: 1} if aliased else {},
        compiler_params=_cparams(("arbitrary",)),
        name="mix_norm",
    )(*args)


_RING = 6
_AHEAD = 4


def _expert_kernel(be_ref, bv_ref, nu_ref, x_ref, bgu_ref, bd_ref, wgu_hbm, wd_hbm, o_ref,
                   act_ref, ring_ref, sem_ref, *, sub, tc):
    rb = pl.program_id(0)
    valid = bv_ref[rb]
    tm, d = x_ref.shape
    d_ff = act_ref.shape[1]
    nfc = d_ff // tc
    nnc = d // tc
    per_block = 2 * nfc + nnc
    total = nu_ref[0] * per_block

    def chunk_copy(src, slot):
        return pltpu.make_async_copy(src, ring_ref.at[slot], sem_ref.at[slot])

    def start(cg):
        @pl.when(cg < total)
        def _():
            blk = cg // per_block
            j = cg - blk * per_block
            e = be_ref[blk]
            slot = lax.rem(cg, _RING)

            @pl.when(j < 2 * nfc)
            def _():
                col = pl.multiple_of((lax.rem(j, 2) * nfc + j // 2) * tc, tc)
                chunk_copy(wgu_hbm.at[e, :, pl.ds(col, tc)], slot).start()

            @pl.when(j >= 2 * nfc)
            def _():
                col = pl.multiple_of((j - 2 * nfc) * tc, tc)
                chunk_copy(wd_hbm.at[e, :, pl.ds(col, tc)], slot).start()

    def wait(cg):
        slot = lax.rem(cg, _RING)
        chunk_copy(wgu_hbm.at[0, :, pl.ds(0, tc)], slot).wait()
        return slot

    @pl.when(rb == 0)
    def _():
        for c in range(_AHEAD):
            start(c)

    base = rb * per_block

    def block(nrows):
        def up(j, carry):
            cg = base + 2 * j
            start(cg + _AHEAD)
            start(cg + 1 + _AHEAD)
            gate_slot = wait(cg)
            up_slot = wait(cg + 1)
            col = pl.ds(pl.multiple_of(j * tc, tc), tc)
            ucol = pl.ds(pl.multiple_of(d_ff + j * tc, tc), tc)
            x = x_ref[0:nrows, :]
            hg = jnp.dot(x, ring_ref[gate_slot].astype(BF16), preferred_element_type=F32) + bgu_ref[0, :, col]
            hu = jnp.dot(x, ring_ref[up_slot].astype(BF16), preferred_element_type=F32) + bgu_ref[0, :, ucol]
            gate = jnp.minimum(hg, SWIGLU_LIMIT)
            upv = jnp.clip(hu, -SWIGLU_LIMIT, SWIGLU_LIMIT)
            act = (upv + 1.0) * gate * jax.nn.sigmoid(SWIGLU_ALPHA * gate)
            act_ref[0:nrows, col] = act.astype(BF16)
            return carry

        lax.fori_loop(0, nfc, up, 0)

        def down(n, carry):
            cg = base + 2 * nfc + n
            start(cg + _AHEAD)
            slot = wait(cg)
            col = pl.ds(pl.multiple_of(n * tc, tc), tc)
            y = jnp.dot(act_ref[0:nrows, :], ring_ref[slot].astype(BF16), preferred_element_type=F32)
            o_ref[0:nrows, col] = (y + bd_ref[0, :, col]).astype(o_ref.dtype)
            return carry

        lax.fori_loop(0, nnc, down, 0)
        if nrows < tm:
            o_ref[nrows:tm, :] = jnp.zeros((tm - nrows, d), o_ref.dtype)

    for nrows in range(sub, tm + 1, sub):
        @pl.when(jnp.logical_and(valid > nrows - sub, valid <= nrows))
        def _(nrows=nrows):
            block(nrows)

    @pl.when(valid == 0)
    def _():
        o_ref[...] = jnp.zeros(o_ref.shape, o_ref.dtype)


def _experts(x_pad, block_e, block_valid, n_used, w_gu, b_gu, w_down, b_down, tm, tc, sub):
    p, d = x_pad.shape
    d_ff = w_down.shape[1]
    assert d_ff == d and _AHEAD <= _RING - 2
    nb = p // tm

    def rbc(rb, nu):
        return jnp.minimum(rb, nu[0] - 1)

    grid_spec = pltpu.PrefetchScalarGridSpec(
        num_scalar_prefetch=3,
        grid=(nb,),
        in_specs=[
            pl.BlockSpec((tm, d), lambda rb, be, bv, nu: (rbc(rb, nu), 0)),
            pl.BlockSpec((1, 1, 2 * d_ff), lambda rb, be, bv, nu: (be[rbc(rb, nu)], 0, 0)),
            pl.BlockSpec((1, 1, d), lambda rb, be, bv, nu: (be[rbc(rb, nu)], 0, 0)),
            pl.BlockSpec(memory_space=pl.ANY),
            pl.BlockSpec(memory_space=pl.ANY),
        ],
        out_specs=pl.BlockSpec((tm, d), lambda rb, be, bv, nu: (rb, 0)),
        scratch_shapes=[
            pltpu.VMEM((tm, d_ff), BF16),
            pltpu.VMEM((_RING, d, tc), F32),
            pltpu.SemaphoreType.DMA((_RING,)),
        ],
    )
    return pl.pallas_call(
        functools.partial(_expert_kernel, sub=sub, tc=tc),
        out_shape=jax.ShapeDtypeStruct((p, d), BF16),
        grid_spec=grid_spec,
        compiler_params=_cparams(("arbitrary",)),
        name="experts",
    )(block_e, block_valid, n_used, x_pad, b_gu, b_down, w_gu, w_down)


def _combine_kernel(y_ref, gate_ref, x1_ref, g_ref, b_ref, o_ref):
    y = y_ref[0].astype(F32) * gate_ref[:, 0:1]
    for k in range(1, TOP_K):
        y = y + y_ref[k].astype(F32) * gate_ref[:, k:k + 1]
    o_ref[...] = _layer_norm(DEEPNORM_ALPHA * x1_ref[...] + y, g_ref[...], b_ref[...])


def _combine(y_rows, gates, x1, ln_g, ln_b, row_block0, tn):
    n, d = x1.shape
    const = lambda i: (0, 0)
    return pl.pallas_call(
        _combine_kernel,
        out_shape=jax.ShapeDtypeStruct((n, d), F32),
        grid=(n // tn,),
        in_specs=[
            pl.BlockSpec((TOP_K, tn, d), lambda i: (0, row_block0 + i, 0)),
            pl.BlockSpec((tn, TOP_K), lambda i: (i, 0)),
            pl.BlockSpec((tn, d), lambda i: (i, 0)),
            pl.BlockSpec((1, d), const),
            pl.BlockSpec((1, d), const),
        ],
        out_specs=pl.BlockSpec((tn, d), lambda i: (i, 0)),
        compiler_params=_cparams(("parallel",)),
        name="combine",
    )(y_rows, gates, x1, ln_g, ln_b)


def _route(idx, rank, sizes, tm):
    n = idx.shape[0]
    nk = n * TOP_K
    nblk = (sizes + tm - 1) // tm
    bends = jnp.cumsum(nblk)
    bstart = bends - nblk
    dest = bstart[idx] * tm + rank
    nb = -(-nk // tm) + N_EXPERTS
    blk = jnp.arange(nb, dtype=jnp.int32)
    block_e = jnp.minimum(jnp.sum(bends[None, :] <= blk[:, None], axis=1), N_EXPERTS - 1).astype(jnp.int32)
    n_used = bends[-1].astype(jnp.int32)
    valid = jnp.clip(sizes[block_e] - (blk - bstart[block_e]) * tm, 0, tm)
    valid = jnp.where(blk < n_used, valid, 0).astype(jnp.int32)
    tok = jnp.broadcast_to(jnp.arange(n, dtype=jnp.int32)[:, None], (n, TOP_K))
    keys = jnp.sort((idx * n + tok).reshape(-1))
    tok_sorted = keys - (keys // n) * n
    row = jnp.arange(nb * tm, dtype=jnp.int32)
    row_e = jnp.repeat(block_e, tm)
    off = row - bstart[row_e] * tm
    first = jnp.cumsum(sizes) - sizes
    src = jnp.clip(first[row_e] + off, 0, nk - 1)
    row_tok = jnp.where(off < sizes[row_e], tok_sorted[src], row % n)
    return dest, row_tok, block_e, valid, n_used.reshape(1)


def _pick(n, pref):
    t = min(n, pref)
    while n % t:
        t //= 2
    return t


def kernel(x_prompt, x_sample, cache_k, cache_v, cache_logf, state_conv, state_lru, w_in, b_f, w_conv, b_conv, w_a, b_a, w_i, b_i, lam, w_out, ln1_g, ln1_b, w_router, b_router, w_gu, b_gu, w_down, b_down, ln2_g, ln2_b):
    assert w_in.shape[0] == DEPTH
    bp, tp, d = x_prompt.shape
    bs, ts, _ = x_sample.shape
    past = cache_k.shape[2]
    lru_w = w_conv.shape[-1]
    np_, ns = bp * tp, bs * ts

    win = w_in[0]
    f0 = 3 * FOX_WIDTH
    w_main = jnp.concatenate([win[:, :f0], win[:, f0 + FOX_HEADS:]], axis=1).astype(BF16)
    w_f = jnp.pad(win[:, f0:f0 + FOX_HEADS], ((0, 0), (0, LANES - FOX_HEADS))).astype(BF16)
    bf2 = b_f[0].reshape(1, FOX_HEADS)
    gpb = 4
    bd = w_a.shape[-1]
    eye = jnp.eye(gpb, dtype=F32)

    def blockdiag(w):
        wg = w.reshape(LRU_BLOCKS // gpb, gpb, bd, bd)
        return jnp.einsum("gacd,ab->gacbd", wg, eye).reshape(LRU_BLOCKS // gpb, gpb * bd, gpb * bd)

    w_gate = jnp.concatenate([blockdiag(w_a[0]), blockdiag(w_i[0])], axis=-1).astype(BF16)
    w_top = w_out[0, :FOX_WIDTH].astype(BF16)
    w_bot = w_out[0, FOX_WIDTH:].astype(BF16)
    wr = w_router[0]
    wr_hi = wr.astype(BF16)
    wr_cat = jnp.concatenate([wr_hi, (wr - wr_hi.astype(F32)).astype(BF16)], axis=1)
    row = lambda a: a.reshape(1, -1)

    n_all = np_ + ns
    tm_mix = _pick(ns, ROWS_IN_PROJ)

    def mixers(x, batch, seq, fox_fn, conv0, h0, cnt0, x1b_buf, row0):
        n = batch * seq
        q, k, v, kb, vb, xr, yg, logf = _in_proj(x.reshape(n, d), w_main, w_f, bf2, _pick(n, ROWS_IN_PROJ))
        fox_o = fox_fn(q, kb, vb, logf.reshape(batch, seq, FOX_HEADS))
        lru_o, conv_new, h_last = _lru(xr, yg, conv0, h0, w_conv[0], row(b_conv[0]), w_gate,
                                       row(b_a[0]), row(b_i[0]), row(lam[0]), batch, seq, _pick(seq, ROWS_LRU))
        routed = _mix_norm(fox_o, lru_o, x.reshape(n, d), w_top, w_bot, row(ln1_g[0]), row(ln1_b[0]),
                           wr_cat, wr_hi, row(b_router[0]), cnt0, tm_mix, n_all, row0 // tm_mix, x1b_buf)
        state = (k.reshape(1, batch, seq, FOX_HEADS, FOX_HEAD_DIM), v.reshape(1, batch, seq, FOX_HEADS, FOX_HEAD_DIM),
                 logf.reshape(1, batch, seq, FOX_HEADS), conv_new[None], h_last.reshape(1, batch, lru_w))
        return routed, state

    def fox_p(q, kb, vb, logf):
        c = _cumsum_time(logf, jnp.zeros((bp, 1, FOX_HEADS), F32))
        return _fox_prompt(q, kb, vb, c * LOG2E, bp, tp, _pick(tp, ROWS_ATTN))

    def fox_s(q, kb, vb, logf):
        c_past = _cumsum_time(cache_logf[0], jnp.zeros((bs, 1, FOX_HEADS), F32))
        c_new = _cumsum_time(logf, c_past[:, -1:, :])
        c_all = jnp.concatenate([c_past, c_new], axis=1) * LOG2E
        cache_rows = lambda c: c[0].reshape(bs, past * FOX_HEADS, FOX_HEAD_DIM)
        return _fox_sample(q, kb, vb, cache_rows(cache_k), cache_rows(cache_v), c_all, bs, ts, _pick(past, ROWS_ATTN))

    (x1p, x1b, idxp, gp, rankp, cnt_p), state_p = mixers(
        x_prompt, bp, tp, fox_p, jnp.zeros((bp, CONV_WIDTH - 1, lru_w), F32), jnp.zeros((bp, 1, lru_w), F32),
        jnp.zeros((1, N_EXPERTS), F32), None, 0)
    (x1s, x1b, idxs, gs, ranks, cnt_all), state_s = mixers(
        x_sample, bs, ts, fox_s, state_conv[0], state_lru[0].reshape(bs, 1, lru_w), cnt_p, x1b, np_)

    tm_e, tc_e, sub_e = ROWS_EXPERT_BLOCK, COLS_EXPERT_CHUNK, ROWS_EXPERT_SUB
    idx = jnp.concatenate([idxp, idxs], axis=0)
    rank = jnp.concatenate([rankp, ranks], axis=0)
    dest, row_tok, block_e, block_valid, n_used = _route(idx, rank, cnt_all[0].astype(jnp.int32), tm_e)
    x_pad = x1b[row_tok]
    y_pad = _experts(x_pad, block_e, block_valid, n_used, w_gu[0], b_gu[0].reshape(N_EXPERTS, 1, -1),
                     w_down[0], b_down[0].reshape(N_EXPERTS, 1, -1), tm_e, tc_e, sub_e)
    y_rows = y_pad[dest.T]

    tn = _pick(ns, ROWS_IN_PROJ)
    yp = _combine(y_rows, gp, x1p, row(ln2_g[0]), row(ln2_b[0]), 0, tn)
    ys = _combine(y_rows, gs, x1s, row(ln2_g[0]), row(ln2_b[0]), np_ // tn, tn)
    return (yp.reshape(bp, tp, d), ys.reshape(bs, ts, d)) + state_p + state_s
```

```python
import functools

import jax
import jax.numpy as jnp
from jax import lax
from jax.experimental import pallas as pl
from jax.experimental.pallas import tpu as pltpu

F32 = jnp.float32
BF16 = jnp.bfloat16

FOX_HEADS = 8
FOX_HEAD_DIM = 128
FOX_WIDTH = FOX_HEADS * FOX_HEAD_DIM
LRU_BLOCKS = 16
CONV_WIDTH = 4
LRU_C = 8.0
N_EXPERTS = 32
TOP_K = 4
SWIGLU_LIMIT = 7.0
SWIGLU_ALPHA = 1.702
LN_EPS = 1e-5
DEPTH = 1
DEEPNORM_ALPHA = (2.0 * DEPTH) ** 0.25
LOG2E = 1.4426950408889634
Q_SCALE = FOX_HEAD_DIM ** -0.5 * LOG2E

LANES = 128
V7X_VMEM_BYTES = 64 * 1024 * 1024
VMEM_LIMIT = V7X_VMEM_BYTES * 7 // 8

ROWS_IN_PROJ = 512
ROWS_ATTN = 512
ROWS_LRU = 512
ROWS_CUMSUM = 256
ROWS_EXPERT_BLOCK = 1024
ROWS_EXPERT_SUB = 256
COLS_EXPERT_CHUNK = 512


def _cparams(sem):
    return pltpu.CompilerParams(dimension_semantics=sem, vmem_limit_bytes=VMEM_LIMIT)


def _log_sigmoid(x):
    return jnp.minimum(x, 0.0) - jnp.log1p(jnp.exp(-jnp.abs(x)))


def _layer_norm(z, g, b):
    mu = jnp.mean(z, axis=-1, keepdims=True)
    zc = z - mu
    var = jnp.mean(zc * zc, axis=-1, keepdims=True)
    return zc * lax.rsqrt(var + LN_EPS) * g + b


def _in_proj_kernel(x_ref, w_ref, wf_ref, bf_ref, q_ref, k_ref, v_ref, kb_ref, vb_ref, xr_ref, yg_ref, lf_ref):
    wcol = q_ref.shape[1]
    xb = x_ref[...].astype(BF16)
    zf = jnp.dot(xb, wf_ref[...], preferred_element_type=F32)
    lf_ref[...] = _log_sigmoid(zf[:, :FOX_HEADS] + bf_ref[...])

    def proj(j):
        return jnp.dot(xb, w_ref[:, j * wcol:(j + 1) * wcol], preferred_element_type=F32)

    q_ref[...] = (proj(0) * Q_SCALE).astype(BF16)
    z = proj(1)
    k_ref[...] = z
    kb_ref[...] = z.astype(BF16)
    z = proj(2)
    v_ref[...] = z
    vb_ref[...] = z.astype(BF16)
    xr_ref[...] = proj(3)
    yg_ref[...] = proj(4)


def _in_proj(x2d, w_main, w_f, b_f, tm):
    n, d = x2d.shape
    wcol = FOX_WIDTH
    row = lambda i: (i, 0)
    const = lambda i: (0, 0)
    dts = (BF16, F32, F32, BF16, BF16, F32, F32)
    outs = [jax.ShapeDtypeStruct((n, wcol), dt) for dt in dts]
    outs.append(jax.ShapeDtypeStruct((n, FOX_HEADS), F32))
    return pl.pallas_call(
        _in_proj_kernel,
        out_shape=outs,
        grid=(n // tm,),
        in_specs=[
            pl.BlockSpec((tm, d), row),
            pl.BlockSpec(w_main.shape, const, pipeline_mode=pl.Buffered(1)),
            pl.BlockSpec((d, LANES), const, pipeline_mode=pl.Buffered(1)),
            pl.BlockSpec((1, FOX_HEADS), const),
        ],
        out_specs=[pl.BlockSpec((tm, wcol), row)] * len(dts) + [pl.BlockSpec((tm, FOX_HEADS), row)],
        compiler_params=_cparams(("parallel",)),
        name="in_proj",
    )(x2d, w_main, w_f, b_f)


def _cumsum_kernel(s_ref, c0_ref, o_ref, tri_ref):
    t = s_ref.shape[1]
    tc = tri_ref.shape[0]
    h = FOX_HEADS

    @pl.when(pl.program_id(0) == 0)
    def _():
        row = lax.broadcasted_iota(jnp.int32, (tc, tc), 0)
        col = lax.broadcasted_iota(jnp.int32, (tc, tc), 1)
        tri_ref[...] = jnp.where(col <= row, 1.0, 0.0).astype(BF16)

    carry = c0_ref[0]
    for c in range(t // tc):
        rows = slice(c * tc, (c + 1) * tc)
        c3 = jnp.dot(tri_ref[...], s_ref[0, rows, :], preferred_element_type=F32)
        out = c3[:, 0:h] + c3[:, h:2 * h] + c3[:, 2 * h:3 * h] + carry
        o_ref[0, rows, :] = out
        carry = out[tc - 1:tc, :]


def _cumsum_time(x, c0):
    b, t, h = x.shape
    hi = x.astype(BF16)
    r1 = x - hi.astype(F32)
    mid = r1.astype(BF16)
    lo = (r1 - mid.astype(F32)).astype(BF16)
    pieces = jnp.concatenate([hi, mid, lo, jnp.zeros((b, t, LANES - 3 * h), BF16)], axis=-1)
    return pl.pallas_call(
        _cumsum_kernel,
        out_shape=jax.ShapeDtypeStruct((b, t, h), F32),
        grid=(b,),
        in_specs=[
            pl.BlockSpec((1, t, LANES), lambda i: (i, 0, 0)),
            pl.BlockSpec((1, 1, h), lambda i: (i, 0, 0)),
        ],
        out_specs=pl.BlockSpec((1, t, h), lambda i: (i, 0, 0)),
        scratch_shapes=[pltpu.VMEM((_pick(t, ROWS_CUMSUM), _pick(t, ROWS_CUMSUM)), BF16)],
        compiler_params=_cparams(("arbitrary",)),
        name="cumsum_time",
    )(pieces, c0)


_NT = (((1,), (1,)), ((), ()))


def _attn_init(m_ref, l_ref, acc_ref):
    m_ref[...] = jnp.full(m_ref.shape, -jnp.inf, F32)
    l_ref[...] = jnp.zeros(l_ref.shape, F32)
    acc_ref[...] = jnp.zeros(acc_ref.shape, F32)


def _fox_prompt_kernel(qi_ref, ki_ref, q_ref, k_ref, v_ref, ck_ref, o_ref,
                       m_ref, l_ref, acc_ref, s0_ref, s1_ref, p0_ref, p1_ref, a0_ref, a1_ref, *, rs):
    pair = pl.program_id(1)
    qi = qi_ref[pair]
    ki = ki_ref[pair]
    tq = q_ref.shape[0]
    tk = k_ref.shape[0]
    lanes = FOX_HEAD_DIM

    @pl.when(ki == 0)
    def _():
        _attn_init(m_ref, l_ref, acc_ref)

    def head_lanes(h):
        return pl.ds(pl.multiple_of(h * lanes, lanes), lanes)

    def scores(h, s_ref):
        sl = head_lanes(h)
        s_ref[...] = lax.dot_general(q_ref[:, sl], k_ref[:, sl], _NT, preferred_element_type=F32)

    def weighted_values(h, p_ref, a_ref):
        sl = head_lanes(h)
        pv = jnp.dot(p_ref[...], v_ref[:, sl], preferred_element_type=F32)
        acc_ref[:, sl] = a_ref[...] * acc_ref[:, sl] + pv

    def softmax(h, s_ref, p_ref, a_ref, masked):
        ck = ck_ref[0, h]
        for r in range(tq // rs):
            r0 = r * rs
            rsl = slice(r0, r0 + rs)
            chunks = []
            for c in range(tk // lanes):
                if masked and c * lanes > r0 + rs - 1:
                    continue
                s = s_ref[rsl, c * lanes:(c + 1) * lanes] - ck[:, c * lanes:(c + 1) * lanes]
                if masked and (c + 1) * lanes - 1 > r0:
                    row = r0 + lax.broadcasted_iota(jnp.int32, (rs, lanes), 0)
                    col = c * lanes + lax.broadcasted_iota(jnp.int32, (rs, lanes), 1)
                    s = jnp.where(col <= row, s, -jnp.inf)
                chunks.append((c, s))
            mc = chunks[0][1]
            for _, s in chunks[1:]:
                mc = jnp.maximum(mc, s)
            m_prev = m_ref[h, rsl, :]
            m_new = jnp.maximum(m_prev, jnp.max(mc, axis=-1, keepdims=True))
            alpha = jnp.exp2(m_prev - m_new)
            psum = None
            for c, s in chunks:
                p = jnp.exp2(s - m_new)
                p_ref[rsl, c * lanes:(c + 1) * lanes] = p.astype(BF16)
                psum = p if psum is None else psum + p
            for c in range(chunks[-1][0] + 1, tk // lanes):
                p_ref[rsl, c * lanes:(c + 1) * lanes] = jnp.zeros((rs, lanes), BF16)
            l_ref[h, rsl, :] = alpha * l_ref[h, rsl, :] + psum
            m_ref[h, rsl, :] = m_new
            a_ref[rsl, :] = alpha

    def all_heads(masked):
        scores(0, s0_ref)
        p1_ref[...] = jnp.zeros(p1_ref.shape, BF16)
        a1_ref[...] = jnp.ones(a1_ref.shape, F32)

        def two_heads(i, carry):
            h0 = 2 * i
            h1 = h0 + 1
            scores(h1, s1_ref)
            softmax(h0, s0_ref, p0_ref, a0_ref, masked)
            weighted_values(lax.rem(h0 + FOX_HEADS - 1, FOX_HEADS), p1_ref, a1_ref)
            scores(lax.rem(h0 + 2, FOX_HEADS), s0_ref)
            softmax(h1, s1_ref, p1_ref, a1_ref, masked)
            weighted_values(h0, p0_ref, a0_ref)
            return carry

        lax.fori_loop(0, FOX_HEADS // 2, two_heads, 0)
        weighted_values(FOX_HEADS - 1, p1_ref, a1_ref)

    @pl.when(ki < qi)
    def _():
        all_heads(False)

    @pl.when(ki == qi)
    def _():
        all_heads(True)
        for h in range(FOX_HEADS):
            sl = slice(h * lanes, (h + 1) * lanes)
            l_tot = jnp.sum(l_ref[h], axis=-1, keepdims=True)
            o_ref[:, sl] = (acc_ref[:, sl] / l_tot).astype(o_ref.dtype)


def _fox_prompt(q, kb, vb, c2, batch, seq, tq):
    nq = seq // tq
    width = q.shape[1]
    ck = jnp.transpose(c2, (0, 2, 1))[:, :, None, :]
    pairs = [(i, j) for i in range(nq) for j in range(i + 1)]
    qi_tab = jnp.asarray([p[0] for p in pairs], jnp.int32)
    ki_tab = jnp.asarray([p[1] for p in pairs], jnp.int32)
    q_map = lambda b, p, qt, kt: (b * nq + qt[p], 0)
    k_map = lambda b, p, qt, kt: (b * nq + kt[p], 0)
    grid_spec = pltpu.PrefetchScalarGridSpec(
        num_scalar_prefetch=2,
        grid=(batch, len(pairs)),
        in_specs=[
            pl.BlockSpec((tq, width), q_map),
            pl.BlockSpec((tq, width), k_map),
            pl.BlockSpec((tq, width), k_map),
            pl.BlockSpec((1, FOX_HEADS, 1, tq), lambda b, p, qt, kt: (b, 0, 0, kt[p])),
        ],
        out_specs=pl.BlockSpec((tq, width), q_map),
        scratch_shapes=[
            pltpu.VMEM((FOX_HEADS, tq, FOX_HEAD_DIM), F32),
            pltpu.VMEM((FOX_HEADS, tq, FOX_HEAD_DIM), F32),
            pltpu.VMEM((tq, width), F32),
            pltpu.VMEM((tq, tq), F32),
            pltpu.VMEM((tq, tq), F32),
            pltpu.VMEM((tq, tq), BF16),
            pltpu.VMEM((tq, tq), BF16),
            pltpu.VMEM((tq, FOX_HEAD_DIM), F32),
            pltpu.VMEM((tq, FOX_HEAD_DIM), F32),
        ],
    )
    return pl.pallas_call(
        functools.partial(_fox_prompt_kernel, rs=min(64, tq)),
        out_shape=jax.ShapeDtypeStruct(q.shape, BF16),
        grid_spec=grid_spec,
        compiler_params=_cparams(("parallel", "arbitrary")),
        name="fox_prompt",
    )(qi_tab, ki_tab, q, kb, vb, ck)


def _fox_sample_kernel(q_ref, kc_ref, vc_ref, kn_ref, vn_ref, ckc_ref, ckn_ref, o_ref, m_ref, l_ref, acc_ref):
    j = pl.program_id(1)
    tq = q_ref.shape[0]
    tk = kc_ref.shape[1] // FOX_HEADS
    lanes = FOX_HEAD_DIM

    @pl.when(j == 0)
    def _():
        _attn_init(m_ref, l_ref, acc_ref)

    heads = range(FOX_HEADS)
    head_lanes = [slice(h * lanes, (h + 1) * lanes) for h in heads]

    def update(ks, vs, cks, masked):
        ss = [lax.dot_general(q_ref[:, head_lanes[h]], ks[h], _NT, preferred_element_type=F32) - cks[h]
              for h in heads]
        if masked:
            row = lax.broadcasted_iota(jnp.int32, ss[0].shape, 0)
            col = lax.broadcasted_iota(jnp.int32, ss[0].shape, 1)
            ss = [jnp.where(col <= row, s, -jnp.inf) for s in ss]
        m_prev = [m_ref[h] for h in heads]
        m_new = [jnp.maximum(m_prev[h], jnp.max(ss[h], axis=-1, keepdims=True)) for h in heads]
        alpha = [jnp.exp2(m_prev[h] - m_new[h]) for h in heads]
        ps = [jnp.exp2(ss[h] - m_new[h][:, 0:1]) for h in heads]
        pvs = [jnp.dot(ps[h].astype(BF16), vs[h], preferred_element_type=F32) for h in heads]
        for h in heads:
            l_ref[h] = alpha[h] * l_ref[h] + jnp.sum(ps[h], axis=-1, keepdims=True)
            acc_ref[:, head_lanes[h]] = alpha[h] * acc_ref[:, head_lanes[h]] + pvs[h]
            m_ref[h] = m_new[h]

    update([kc_ref[0, pl.ds(h, tk, stride=FOX_HEADS), :].astype(BF16) for h in heads],
           [vc_ref[0, pl.ds(h, tk, stride=FOX_HEADS), :].astype(BF16) for h in heads],
           [ckc_ref[0, h] for h in heads], False)

    @pl.when(j == pl.num_programs(1) - 1)
    def _():
        update([kn_ref[:, sl] for sl in head_lanes], [vn_ref[:, sl] for sl in head_lanes],
               [ckn_ref[0, h] for h in heads], True)
        for h in heads:
            sl = head_lanes[h]
            o_ref[:, sl] = (acc_ref[:, sl] / l_ref[h]).astype(o_ref.dtype)


def _fox_sample(q, kb, vb, cache_k, cache_v, c2_all, batch, seq, tk):
    past = cache_k.shape[1] // FOX_HEADS
    width = q.shape[1]
    ct = jnp.transpose(c2_all, (0, 2, 1))
    ckc = ct[:, :, None, :past]
    ckn = ct[:, :, None, past:]
    new_map = lambda b, j: (b, 0)
    cache_spec = pl.BlockSpec((1, tk * FOX_HEADS, FOX_HEAD_DIM), lambda b, j: (b, j, 0))
    return pl.pallas_call(
        _fox_sample_kernel,
        out_shape=jax.ShapeDtypeStruct(q.shape, BF16),
        grid=(batch, past // tk),
        in_specs=[
            pl.BlockSpec((seq, width), new_map),
            cache_spec,
            cache_spec,
            pl.BlockSpec((seq, width), new_map),
            pl.BlockSpec((seq, width), new_map),
            pl.BlockSpec((1, FOX_HEADS, 1, tk), lambda b, j: (b, 0, 0, j)),
            pl.BlockSpec((1, FOX_HEADS, 1, seq), lambda b, j: (b, 0, 0, 0)),
        ],
        out_specs=pl.BlockSpec((seq, width), new_map),
        scratch_shapes=[
            pltpu.VMEM((FOX_HEADS, seq, FOX_HEAD_DIM), F32),
            pltpu.VMEM((FOX_HEADS, seq, FOX_HEAD_DIM), F32),
            pltpu.VMEM((seq, width), F32),
        ],
        compiler_params=_cparams(("parallel", "arbitrary")),
        name="fox_sample",
    )(q, cache_k, cache_v, kb, vb, ckc, ckn)


_HALO = 8


def _lru_kernel(xr_ref, yg_ref, conv0_ref, h0_ref, wc_ref, bc_ref, wg_ref, ba_ref, bi_ref, lam_ref,
                o_ref, convo_ref, hlast_ref, xp_ref, a_ref, b_ref, h_ref):
    t = pl.program_id(1)
    tt, width = xr_ref.shape
    tail = CONV_WIDTH - 1
    lo = _HALO - tail

    @pl.when(t == 0)
    def _():
        xp_ref[lo:_HALO, :] = conv0_ref[0]
        h_ref[...] = h0_ref[0]

    xp_ref[_HALO:_HALO + tt, :] = xr_ref[...]
    xc = bc_ref[...] + xp_ref[lo:lo + tt, :] * wc_ref[0:1, :]
    for j in range(1, CONV_WIDTH):
        xc = xc + xp_ref[lo + j:lo + j + tt, :] * wc_ref[j:j + 1, :]
    new_tail = xp_ref[lo + tt:_HALO + tt, :]
    xp_ref[lo:_HALO, :] = new_tail

    xcb = xc.astype(BF16)
    gw = wg_ref.shape[1]
    lam = lam_ref[...]
    neg_sp = -(jnp.maximum(-lam, 0.0) + jnp.log1p(jnp.exp(-jnp.abs(lam))))
    for g in range(width // gw):
        sl = slice(g * gw, (g + 1) * gw)
        z = jnp.dot(xcb[:, sl], wg_ref[g], preferred_element_type=F32)
        r = jax.nn.sigmoid(z[:, :gw] + ba_ref[:, sl])
        i = jax.nn.sigmoid(z[:, gw:] + bi_ref[:, sl])
        log_a = LRU_C * r * neg_sp[:, sl]
        a = jnp.exp(log_a)
        a_ref[:, sl] = a
        b_ref[:, sl] = jnp.sqrt(-jnp.tanh(log_a) * (a * a + 1.0)) * i * xc[:, sl]

    def step(s, h):
        row = pl.ds(s, 1)
        h = a_ref[row, :] * h + b_ref[row, :]
        b_ref[row, :] = h
        return h

    h = lax.fori_loop(0, tt, step, h_ref[...], unroll=8)
    h_ref[...] = h
    o_ref[...] = (b_ref[...] * jax.nn.gelu(yg_ref[...])).astype(o_ref.dtype)

    @pl.when(t == pl.num_programs(1) - 1)
    def _():
        convo_ref[0] = new_tail
        hlast_ref[0] = h


def _lru(xr, yg, conv0, h0, w_conv, b_conv, w_gate, b_a, b_i, lam, batch, seq, tt):
    width = xr.shape[1]
    nt = seq // tt
    gw = w_gate.shape[1]
    rows = lambda b, t: (b * nt + t, 0)
    const2 = lambda b, t: (0, 0)
    per_b = lambda b, t: (b, 0, 0)
    tail = CONV_WIDTH - 1
    return pl.pallas_call(
        _lru_kernel,
        out_shape=[
            jax.ShapeDtypeStruct((batch * seq, width), BF16),
            jax.ShapeDtypeStruct((batch, tail, width), F32),
            jax.ShapeDtypeStruct((batch, 1, width), F32),
        ],
        grid=(batch, nt),
        in_specs=[
            pl.BlockSpec((tt, width), rows),
            pl.BlockSpec((tt, width), rows),
            pl.BlockSpec((1, tail, width), per_b),
            pl.BlockSpec((1, 1, width), per_b),
            pl.BlockSpec((CONV_WIDTH, width), const2),
            pl.BlockSpec((1, width), const2),
            pl.BlockSpec((width // gw, gw, 2 * gw), lambda b, t: (0, 0, 0)),
            pl.BlockSpec((1, width), const2),
            pl.BlockSpec((1, width), const2),
            pl.BlockSpec((1, width), const2),
        ],
        out_specs=[
            pl.BlockSpec((tt, width), rows),
            pl.BlockSpec((1, tail, width), per_b),
            pl.BlockSpec((1, 1, width), per_b),
        ],
        scratch_shapes=[
            pltpu.VMEM((_HALO + tt, width), F32),
            pltpu.VMEM((tt, width), F32),
            pltpu.VMEM((tt, width), F32),
            pltpu.VMEM((1, width), F32),
        ],
        compiler_params=_cparams(("parallel", "arbitrary")),
        name="lru",
    )(xr, yg, conv0, h0, w_conv, b_conv, w_gate, b_a, b_i, lam)


def _mix_norm_kernel(*refs, aliased, n_tiles):
    ins = refs[:10]
    outs = refs[11:] if aliased else refs[10:]
    live = pl.program_id(0) < n_tiles

    @pl.when(live)
    def _():
        _mix_norm_body(*ins, *outs)

    @pl.when(jnp.logical_not(live))
    def _():
        outs[1][...] = jnp.zeros(outs[1].shape, outs[1].dtype)


def _mix_norm_body(fox_ref, lru_ref, x_ref, w_ref, g_ref, b_ref, wrc_ref, wrh_ref, br_ref, cnt0_ref,
                   x1_ref, x1b_ref, idx_ref, gate_ref, rank_ref, cnt_ref):
    tm = x_ref.shape[0]

    heads_out = jnp.concatenate([fox_ref[...], lru_ref[...]], axis=1)
    mix = jnp.dot(heads_out, w_ref[...], preferred_element_type=F32)
    x1 = _layer_norm(DEEPNORM_ALPHA * x_ref[...] + mix, g_ref[...], b_ref[...])
    x1_ref[...] = x1
    x1_hi = x1.astype(BF16)
    x1b_ref[...] = x1_hi
    x1_lo = (x1 - x1_hi.astype(F32)).astype(BF16)
    lg2 = jnp.dot(x1_hi, wrc_ref[...], preferred_element_type=F32)
    lg = lg2[:, :N_EXPERTS] + lg2[:, N_EXPERTS:] + jnp.dot(x1_lo, wrh_ref[...], preferred_element_type=F32)
    lg = lg + br_ref[...]
    lanes = lax.broadcasted_iota(jnp.int32, lg.shape, 1)
    vals, picks = [], []
    for k in range(TOP_K):
        m = jnp.max(lg, axis=-1, keepdims=True)
        ix = jnp.min(jnp.where(lg == m, lanes, N_EXPERTS), axis=-1, keepdims=True)
        idx_ref[:, k:k + 1] = ix
        vals.append(m)
        picks.append(lanes == ix)
        lg = jnp.where(picks[-1], -jnp.inf, lg)
    es = [jnp.exp(v - vals[0]) for v in vals]
    denom = es[0] + es[1] + es[2] + es[3]
    for k in range(TOP_K):
        gate_ref[:, k:k + 1] = es[k] / denom

    @pl.when(pl.program_id(0) == 0)
    def _():
        cnt_ref[...] = cnt0_ref[...]

    sel = jnp.where(picks[0], 1.0, 0.0)
    for k in range(1, TOP_K):
        sel = sel + jnp.where(picks[k], 1.0, 0.0)
    earlier = lax.broadcasted_iota(jnp.int32, (tm, tm), 1) < lax.broadcasted_iota(jnp.int32, (tm, tm), 0)
    tri = jnp.where(earlier, 1.0, 0.0).astype(BF16)
    before = jnp.dot(tri, sel.astype(BF16), preferred_element_type=F32) + cnt_ref[...]
    for k in range(TOP_K):
        rank_ref[:, k:k + 1] = jnp.sum(jnp.where(picks[k], before, 0.0), axis=-1, keepdims=True).astype(jnp.int32)
    cnt_ref[...] += jnp.sum(sel, axis=0, keepdims=True)


def _mix_norm(fox_o, lru_o, x2d, w_mix, ln_g, ln_b, wr_cat, wr_hi, b_router, cnt0, tm, n_all, row_block0,
              x1b_buf=None):
    n, d = x2d.shape
    half = fox_o.shape[1]
    n_tiles = n // tm
    aliased = x1b_buf is not None
    steps = n_tiles if aliased else n_all // tm - row_block0
    row = lambda i: (jnp.minimum(i, n_tiles - 1), 0)
    const = lambda i: (0, 0)
    in_specs = [
        pl.BlockSpec((tm, half), row),
        pl.BlockSpec((tm, half), row),
        pl.BlockSpec((tm, d), row),
        pl.BlockSpec((2 * half, d), const),
        pl.BlockSpec((1, d), const),
        pl.BlockSpec((1, d), const),
        pl.BlockSpec((d, 2 * N_EXPERTS), const),
        pl.BlockSpec((d, N_EXPERTS), const),
        pl.BlockSpec((1, N_EXPERTS), const),
        pl.BlockSpec((1, N_EXPERTS), const),
    ]
    args = [fox_o, lru_o, x2d, w_mix, ln_g, ln_b, wr_cat, wr_hi, b_router, cnt0]
    if aliased:
        in_specs.append(pl.BlockSpec(memory_space=pl.ANY))
        args.append(x1b_buf)
    return pl.pallas_call(
        functools.partial(_mix_norm_kernel, aliased=aliased, n_tiles=n_tiles),
        out_shape=[
            jax.ShapeDtypeStruct((n, d), F32),
            jax.ShapeDtypeStruct((n_all, d), BF16),
            jax.ShapeDtypeStruct((n, TOP_K), jnp.int32),
            jax.ShapeDtypeStruct((n, TOP_K), F32),
            jax.ShapeDtypeStruct((n, TOP_K), jnp.int32),
            jax.ShapeDtypeStruct((1, N_EXPERTS), F32),
        ],
        grid=(steps,),
        in_specs=in_specs,
        out_specs=[
            pl.BlockSpec((tm, d), row),
            pl.BlockSpec((tm, d), lambda i: (row_block0 + i, 0)),
            pl.BlockSpec((tm, TOP_K), row),
            pl.BlockSpec((tm, TOP_K), row),
            pl.BlockSpec((tm, TOP_K), row),
            pl.BlockSpec((1, N_EXPERTS), const),
        ],
        input_output_aliases={10: 1} if aliased else {},
        compiler_params=_cparams(("arbitrary",)),
        name="mix_norm",
    )(*args)


_RING = 6
_AHEAD = 4


def _expert_kernel(be_ref, bv_ref, nu_ref, x_ref, bgu_ref, bd_ref, wgu_hbm, wd_hbm, o_ref,
                   act_ref, ring_ref, sem_ref, *, sub, tc):
    rb = pl.program_id(0)
    valid = bv_ref[rb]
    tm, d = x_ref.shape
    d_ff = act_ref.shape[1]
    nfc = d_ff // tc
    nnc = d // tc
    per_block = 2 * nfc + nnc
    total = nu_ref[0] * per_block

    def chunk_copy(src, slot):
        return pltpu.make_async_copy(src, ring_ref.at[slot], sem_ref.at[slot])

    def start(cg):
        @pl.when(cg < total)
        def _():
            blk = cg // per_block
            j = cg - blk * per_block
            e = be_ref[blk]
            slot = lax.rem(cg, _RING)

            @pl.when(j < 2 * nfc)
            def _():
                col = pl.multiple_of((lax.rem(j, 2) * nfc + j // 2) * tc, tc)
                chunk_copy(wgu_hbm.at[e, :, pl.ds(col, tc)], slot).start()

            @pl.when(j >= 2 * nfc)
            def _():
                col = pl.multiple_of((j - 2 * nfc) * tc, tc)
                chunk_copy(wd_hbm.at[e, :, pl.ds(col, tc)], slot).start()

    def wait(cg):
        slot = lax.rem(cg, _RING)
        chunk_copy(wgu_hbm.at[0, :, pl.ds(0, tc)], slot).wait()
        return slot

    @pl.when(rb == 0)
    def _():
        for c in range(_AHEAD):
            start(c)

    base = rb * per_block

    def block(nrows):
        def up(j, carry):
            cg = base + 2 * j
            start(cg + _AHEAD)
            start(cg + 1 + _AHEAD)
            gate_slot = wait(cg)
            up_slot = wait(cg + 1)
            col = pl.ds(pl.multiple_of(j * tc, tc), tc)
            ucol = pl.ds(pl.multiple_of(d_ff + j * tc, tc), tc)
            x = x_ref[0:nrows, :]
            hg = jnp.dot(x, ring_ref[gate_slot].astype(BF16), preferred_element_type=F32) + bgu_ref[0, :, col]
            hu = jnp.dot(x, ring_ref[up_slot].astype(BF16), preferred_element_type=F32) + bgu_ref[0, :, ucol]
            gate = jnp.minimum(hg, SWIGLU_LIMIT)
            upv = jnp.clip(hu, -SWIGLU_LIMIT, SWIGLU_LIMIT)
            act = (upv + 1.0) * gate * jax.nn.sigmoid(SWIGLU_ALPHA * gate)
            act_ref[0:nrows, col] = act.astype(BF16)
            return carry

        lax.fori_loop(0, nfc, up, 0)

        def down(n, carry):
            cg = base + 2 * nfc + n
            start(cg + _AHEAD)
            slot = wait(cg)
            col = pl.ds(pl.multiple_of(n * tc, tc), tc)
            y = jnp.dot(act_ref[0:nrows, :], ring_ref[slot].astype(BF16), preferred_element_type=F32)
            o_ref[0:nrows, col] = (y + bd_ref[0, :, col]).astype(o_ref.dtype)
            return carry

        lax.fori_loop(0, nnc, down, 0)
        if nrows < tm:
            o_ref[nrows:tm, :] = jnp.zeros((tm - nrows, d), o_ref.dtype)

    for nrows in range(sub, tm + 1, sub):
        @pl.when(jnp.logical_and(valid > nrows - sub, valid <= nrows))
        def _(nrows=nrows):
            block(nrows)

    @pl.when(valid == 0)
    def _():
        o_ref[...] = jnp.zeros(o_ref.shape, o_ref.dtype)


def _experts(x_pad, block_e, block_valid, n_used, w_gu, b_gu, w_down, b_down, tm, tc, sub):
    p, d = x_pad.shape
    d_ff = w_down.shape[1]
    assert d_ff == d and _AHEAD <= _RING - 2
    nb = p // tm

    def rbc(rb, nu):
        return jnp.minimum(rb, nu[0] - 1)

    grid_spec = pltpu.PrefetchScalarGridSpec(
        num_scalar_prefetch=3,
        grid=(nb,),
        in_specs=[
            pl.BlockSpec((tm, d), lambda rb, be, bv, nu: (rbc(rb, nu), 0)),
            pl.BlockSpec((1, 1, 2 * d_ff), lambda rb, be, bv, nu: (be[rbc(rb, nu)], 0, 0)),
            pl.BlockSpec((1, 1, d), lambda rb, be, bv, nu: (be[rbc(rb, nu)], 0, 0)),
            pl.BlockSpec(memory_space=pl.ANY),
            pl.BlockSpec(memory_space=pl.ANY),
        ],
        out_specs=pl.BlockSpec((tm, d), lambda rb, be, bv, nu: (rb, 0)),
        scratch_shapes=[
            pltpu.VMEM((tm, d_ff), BF16),
            pltpu.VMEM((_RING, d, tc), F32),
            pltpu.SemaphoreType.DMA((_RING,)),
        ],
    )
    return pl.pallas_call(
        functools.partial(_expert_kernel, sub=sub, tc=tc),
        out_shape=jax.ShapeDtypeStruct((p, d), BF16),
        grid_spec=grid_spec,
        compiler_params=_cparams(("arbitrary",)),
        name="experts",
    )(block_e, block_valid, n_used, x_pad, b_gu, b_down, w_gu, w_down)


def _combine_kernel(y_ref, gate_ref, x1_ref, g_ref, b_ref, o_ref):
    y = y_ref[0].astype(F32) * gate_ref[:, 0:1]
    for k in range(1, TOP_K):
        y = y + y_ref[k].astype(F32) * gate_ref[:, k:k + 1]
    o_ref[...] = _layer_norm(DEEPNORM_ALPHA * x1_ref[...] + y, g_ref[...], b_ref[...])


def _combine(y_rows, gates, x1, ln_g, ln_b, row_block0, tn):
    n, d = x1.shape
    const = lambda i: (0, 0)
    return pl.pallas_call(
        _combine_kernel,
        out_shape=jax.ShapeDtypeStruct((n, d), F32),
        grid=(n // tn,),
        in_specs=[
            pl.BlockSpec((TOP_K, tn, d), lambda i: (0, row_block0 + i, 0)),
            pl.BlockSpec((tn, TOP_K), lambda i: (i, 0)),
            pl.BlockSpec((tn, d), lambda i: (i, 0)),
            pl.BlockSpec((1, d), const),
            pl.BlockSpec((1, d), const),
        ],
        out_specs=pl.BlockSpec((tn, d), lambda i: (i, 0)),
        compiler_params=_cparams(("parallel",)),
        name="combine",
    )(y_rows, gates, x1, ln_g, ln_b)


def _route(idx, rank, sizes, tm):
    n = idx.shape[0]
    nk = n * TOP_K
    nblk = (sizes + tm - 1) // tm
    bends = jnp.cumsum(nblk)
    bstart = bends - nblk
    dest = bstart[idx] * tm + rank
    nb = -(-nk // tm) + N_EXPERTS
    blk = jnp.arange(nb, dtype=jnp.int32)
    block_e = jnp.minimum(jnp.sum(bends[None, :] <= blk[:, None], axis=1), N_EXPERTS - 1).astype(jnp.int32)
    n_used = bends[-1].astype(jnp.int32)
    valid = jnp.clip(sizes[block_e] - (blk - bstart[block_e]) * tm, 0, tm)
    valid = jnp.where(blk < n_used, valid, 0).astype(jnp.int32)
    tok = jnp.broadcast_to(jnp.arange(n, dtype=jnp.int32)[:, None], (n, TOP_K))
    keys = jnp.sort((idx * n + tok).reshape(-1))
    tok_sorted = keys - (keys // n) * n
    row = jnp.arange(nb * tm, dtype=jnp.int32)
    row_e = jnp.repeat(block_e, tm)
    off = row - bstart[row_e] * tm
    first = jnp.cumsum(sizes) - sizes
    src = jnp.clip(first[row_e] + off, 0, nk - 1)
    row_tok = jnp.where(off < sizes[row_e], tok_sorted[src], row % n)
    return dest, row_tok, block_e, valid, n_used.reshape(1)


def _pick(n, pref):
    t = min(n, pref)
    while n % t:
        t //= 2
    return t


def kernel(x_prompt, x_sample, cache_k, cache_v, cache_logf, state_conv, state_lru, w_in, b_f, w_conv, b_conv, w_a, b_a, w_i, b_i, lam, w_out, ln1_g, ln1_b, w_router, b_router, w_gu, b_gu, w_down, b_down, ln2_g, ln2_b):
    assert w_in.shape[0] == DEPTH
    bp, tp, d = x_prompt.shape
    bs, ts, _ = x_sample.shape
    past = cache_k.shape[2]
    lru_w = w_conv.shape[-1]
    np_, ns = bp * tp, bs * ts

    win = w_in[0]
    f0 = 3 * FOX_WIDTH
    w_main = jnp.concatenate([win[:, :f0], win[:, f0 + FOX_HEADS:]], axis=1).astype(BF16)
    w_f = jnp.pad(win[:, f0:f0 + FOX_HEADS], ((0, 0), (0, LANES - FOX_HEADS))).astype(BF16)
    bf2 = b_f[0].reshape(1, FOX_HEADS)
    gpb = 4
    bd = w_a.shape[-1]
    eye = jnp.eye(gpb, dtype=F32)

    def blockdiag(w):
        wg = w.reshape(LRU_BLOCKS // gpb, gpb, bd, bd)
        return jnp.einsum("gacd,ab->gacbd", wg, eye).reshape(LRU_BLOCKS // gpb, gpb * bd, gpb * bd)

    w_gate = jnp.concatenate([blockdiag(w_a[0]), blockdiag(w_i[0])], axis=-1).astype(BF16)
    w_mix = w_out[0].astype(BF16)
    wr = w_router[0]
    wr_hi = wr.astype(BF16)
    wr_cat = jnp.concatenate([wr_hi, (wr - wr_hi.astype(F32)).astype(BF16)], axis=1)
    row = lambda a: a.reshape(1, -1)

    n_all = np_ + ns
    tm_mix = _pick(ns, ROWS_IN_PROJ)

    def mixers(x, batch, seq, fox_fn, conv0, h0, cnt0, x1b_buf, row0):
        n = batch * seq
        q, k, v, kb, vb, xr, yg, logf = _in_proj(x.reshape(n, d), w_main, w_f, bf2, _pick(n, ROWS_IN_PROJ))
        fox_o = fox_fn(q, kb, vb, logf.reshape(batch, seq, FOX_HEADS))
        lru_o, conv_new, h_last = _lru(xr, yg, conv0, h0, w_conv[0], row(b_conv[0]), w_gate,
                                       row(b_a[0]), row(b_i[0]), row(lam[0]), batch, seq, _pick(seq, ROWS_LRU))
        routed = _mix_norm(fox_o, lru_o, x.reshape(n, d), w_mix, row(ln1_g[0]), row(ln1_b[0]),
                           wr_cat, wr_hi, row(b_router[0]), cnt0, tm_mix, n_all, row0 // tm_mix, x1b_buf)
        state = (k.reshape(1, batch, seq, FOX_HEADS, FOX_HEAD_DIM), v.reshape(1, batch, seq, FOX_HEADS, FOX_HEAD_DIM),
                 logf.reshape(1, batch, seq, FOX_HEADS), conv_new[None], h_last.reshape(1, batch, lru_w))
        return routed, state

    def fox_p(q, kb, vb, logf):
        c = _cumsum_time(logf, jnp.zeros((bp, 1, FOX_HEADS), F32))
        return _fox_prompt(q, kb, vb, c * LOG2E, bp, tp, _pick(tp, ROWS_ATTN))

    def fox_s(q, kb, vb, logf):
        c_past = _cumsum_time(cache_logf[0], jnp.zeros((bs, 1, FOX_HEADS), F32))
        c_new = _cumsum_time(logf, c_past[:, -1:, :])
        c_all = jnp.concatenate([c_past, c_new], axis=1) * LOG2E
        cache_rows = lambda c: c[0].reshape(bs, past * FOX_HEADS, FOX_HEAD_DIM)
        return _fox_sample(q, kb, vb, cache_rows(cache_k), cache_rows(cache_v), c_all, bs, ts, _pick(past, ROWS_ATTN))

    (x1p, x1b, idxp, gp, rankp, cnt_p), state_p = mixers(
        x_prompt, bp, tp, fox_p, jnp.zeros((bp, CONV_WIDTH - 1, lru_w), F32), jnp.zeros((bp, 1, lru_w), F32),
        jnp.zeros((1, N_EXPERTS), F32), None, 0)
    (x1s, x1b, idxs, gs, ranks, cnt_all), state_s = mixers(
        x_sample, bs, ts, fox_s, state_conv[0], state_lru[0].reshape(bs, 1, lru_w), cnt_p, x1b, np_)

    tm_e, tc_e, sub_e = ROWS_EXPERT_BLOCK, COLS_EXPERT_CHUNK, ROWS_EXPERT_SUB
    idx = jnp.concatenate([idxp, idxs], axis=0)
    rank = jnp.concatenate([rankp, ranks], axis=0)
    dest, row_tok, block_e, block_valid, n_used = _route(idx, rank, cnt_all[0].astype(jnp.int32), tm_e)
    x_pad = x1b[row_tok]
    y_pad = _experts(x_pad, block_e, block_valid, n_used, w_gu[0], b_gu[0].reshape(N_EXPERTS, 1, -1),
                     w_down[0], b_down[0].reshape(N_EXPERTS, 1, -1), tm_e, tc_e, sub_e)
    y_rows = y_pad[dest.T]

    tn = _pick(ns, ROWS_IN_PROJ)
    yp = _combine(y_rows, gp, x1p, row(ln2_g[0]), row(ln2_b[0]), 0, tn)
    ys = _combine(y_rows, gs, x1s, row(ln2_g[0]), row(ln2_b[0]), np_ // tn, tn)
    return (yp.reshape(bp, tp, d), ys.reshape(bs, ts, d)) + state_p + state_s
```
